```python
import jax, jax.numpy as jnp
from jax import lax
import numpy as np

D_MODEL = 2048
BATCH = 2
SEQ = 8192
DEPTH = 4

N_HEADS = 16
HEAD_DIM = D_MODEL // N_HEADS
N_KV_GROUPS = 4
CMP_BLOCK = 32
CMP_STRIDE = 16
SLC_BLOCK = 64
N_SELECT = 16
WINDOW = 512
Q_BLOCK = 64
PHI_HIDDEN = HEAD_DIM
FORCE_BONUS = 1.0e4
ROPE_THETA = 10000.0
CONV_DIM = D_MODEL
CONV_WIDTH = 3
D_INNER = D_MODEL
SSD_HEAD_DIM = 64
SSD_HEADS = D_INNER // SSD_HEAD_DIM
SSD_GROUPS = 4
SSD_STATE = 128
SSD_CONV = 4
SSD_CHUNK = 128
D_FF = 4 * D_MODEL
PLE_DIM = 256
N_BRANCHES = 3
EPS = 1e-6

KV_DIM = N_KV_GROUPS * HEAD_DIM
SSD_XBC = D_INNER + 2 * SSD_GROUPS * SSD_STATE
PROJ_WIDTHS = (N_HEADS * HEAD_DIM, KV_DIM, KV_DIM, KV_DIM, KV_DIM, KV_DIM, KV_DIM, N_HEADS * 3,
               CONV_DIM, CONV_DIM, CONV_DIM, D_INNER, SSD_XBC, SSD_HEADS, N_BRANCHES * D_MODEL)
PROJ_TOTAL = sum(PROJ_WIDTHS)

kernel_name = 'hybrid_nsa_shortconv_ssd_block'


def rms_norm(x, g):
    xf = x.astype(jnp.float32)
    y = xf * lax.rsqrt(jnp.mean(xf * xf, axis=-1, keepdims=True) + EPS)
    return (y * g.astype(jnp.float32)).astype(x.dtype)


def rope_tables(positions):
    inv_freq = 1.0 / (ROPE_THETA ** (jnp.arange(0, HEAD_DIM, 2, dtype=jnp.float32) / HEAD_DIM))
    ang = positions.astype(jnp.float32)[..., None] * inv_freq
    return jnp.cos(ang)[:, :, None, :], jnp.sin(ang)[:, :, None, :]


def apply_rope(x, cos, sin):
    x1, x2 = jnp.split(x.astype(jnp.float32), 2, axis=-1)
    return jnp.concatenate([x1 * cos - x2 * sin, x2 * cos + x1 * sin], axis=-1).astype(x.dtype)


def masked_softmax(s, mask):
    s = jnp.where(mask, s.astype(jnp.float32), -jnp.inf)
    m = jnp.max(s, axis=-1, keepdims=True)
    e = jnp.where(mask, jnp.exp(s - jnp.where(jnp.isfinite(m), m, 0.0)), 0.0)
    d = jnp.sum(e, axis=-1, keepdims=True)
    return e / jnp.where(d > 0, d, 1.0)


def causal_depthwise_conv(u, w):
    k, T = w.shape[0], u.shape[1]
    up = jnp.pad(u, ((0, 0), (k - 1, 0), (0, 0)))
    y = up[:, 0:T] * w[0]
    for j in range(1, k):
        y = y + up[:, j:j + T] * w[j]
    return y


def compress_blocks(k, pe, w1, b1, w2, b2):
    bsz, T, G, dk = k.shape
    n_cmp = (T - CMP_BLOCK) // CMP_STRIDE + 1
    idx = jnp.arange(n_cmp)[:, None] * CMP_STRIDE + jnp.arange(CMP_BLOCK)[None, :]
    blk = k[:, idx] + pe[:, None, :]
    blk = blk.transpose(0, 3, 1, 2, 4).reshape(bsz, G, n_cmp, CMP_BLOCK * dk)
    return jax.nn.silu(blk @ w1 + b1) @ w2 + b2


def nsa_mixer(q, k_c, v_c, k_s, v_s, k_w, v_w, gate, pe_k, pe_v,
              pk_w1, pk_b1, pk_w2, pk_b2, pv_w1, pv_b1, pv_w2, pv_b2):
    bsz, T, H, dk = q.shape
    G = k_c.shape[2]
    R = H // G
    scale = dk ** -0.5
    n_cmp = (T - CMP_BLOCK) // CMP_STRIDE + 1
    n_slc = T // SLC_BLOCK
    n_sel = min(N_SELECT, n_slc)
    kcmp = compress_blocks(k_c, pe_k, pk_w1, pk_b1, pk_w2, pk_b2)
    vcmp = compress_blocks(v_c, pe_v, pv_w1, pv_b1, pv_w2, pv_b2)
    cstart = jnp.arange(n_cmp) * CMP_STRIDE
    sstart = jnp.arange(n_slc) * SLC_BLOCK
    cend = cstart + CMP_BLOCK - 1
    overlap = ((cstart[:, None] < sstart[None, :] + SLC_BLOCK)
               & (cstart[:, None] + CMP_BLOCK > sstart[None, :])).astype(jnp.float32)
    kblk = k_s.reshape(bsz, n_slc, SLC_BLOCK, G, dk).transpose(0, 3, 1, 2, 4)
    vblk = v_s.reshape(bsz, n_slc, SLC_BLOCK, G, dk).transpose(0, 3, 1, 2, 4)
    kw_pad = jnp.pad(k_w, ((0, 0), (WINDOW, 0), (0, 0), (0, 0)))
    vw_pad = jnp.pad(v_w, ((0, 0), (WINDOW, 0), (0, 0), (0, 0)))
    b_idx = jnp.arange(bsz)[:, None, None, None]
    g_idx = jnp.arange(G)[None, :, None, None]
    blk = jnp.arange(n_slc)

    def query_block(i):
        s = i * Q_BLOCK
        t = s + jnp.arange(Q_BLOCK)
        qb = lax.dynamic_slice_in_dim(q, s, Q_BLOCK, axis=1).reshape(bsz, Q_BLOCK, G, R, dk)
        gb = lax.dynamic_slice_in_dim(gate, s, Q_BLOCK, axis=1).reshape(bsz, Q_BLOCK, G, R, 3)
        p_c = masked_softmax(jnp.einsum('bqgrd,bgnd->bgrqn', qb, kcmp) * scale,
                             cend[None, :] <= t[:, None])
        o_c = jnp.einsum('bgrqn,bgnd->bqgrd', p_c, vcmp)
        imp = jnp.einsum('bgrqn,nj->bgqj', p_c, overlap)
        valid = sstart[None, :] <= t[:, None]
        cur = (t // SLC_BLOCK)[:, None]
        forced = (blk[None, :] == 0) | (blk[None, :] == cur) | (blk[None, :] == cur - 1)
        score = jnp.where(valid, imp + jnp.where(forced, FORCE_BONUS, 0.0), -jnp.inf)
        _, sel = lax.top_k(score, n_sel)
        kg = kblk[b_idx, g_idx, sel].reshape(bsz, G, Q_BLOCK, n_sel * SLC_BLOCK, dk)
        vg = vblk[b_idx, g_idx, sel].reshape(bsz, G, Q_BLOCK, n_sel * SLC_BLOCK, dk)
        kpos = (sel[..., None] * SLC_BLOCK + jnp.arange(SLC_BLOCK)).reshape(
            bsz, G, 1, Q_BLOCK, n_sel * SLC_BLOCK)
        p_s = masked_softmax(jnp.einsum('bqgrd,bgqkd->bgrqk', qb, kg) * scale, kpos <= t[:, None])
        o_s = jnp.einsum('bgrqk,bgqkd->bqgrd', p_s, vg)
        kwb = lax.dynamic_slice_in_dim(kw_pad, s, WINDOW + Q_BLOCK, axis=1)
        vwb = lax.dynamic_slice_in_dim(vw_pad, s, WINDOW + Q_BLOCK, axis=1)
        kidx = s - WINDOW + jnp.arange(WINDOW + Q_BLOCK)
        wmask = ((kidx[None, :] <= t[:, None]) & (kidx[None, :] > t[:, None] - WINDOW)
                 & (kidx[None, :] >= 0))
        p_w = masked_softmax(jnp.einsum('bqgrd,bkgd->bgrqk', qb, kwb) * scale, wmask)
        o_w = jnp.einsum('bgrqk,bkgd->bqgrd', p_w, vwb)
        o = gb[..., 0:1] * o_c + gb[..., 1:2] * o_s + gb[..., 2:3] * o_w
        return o.reshape(bsz, Q_BLOCK, H * dk).astype(q.dtype)

    out = lax.map(query_block, jnp.arange(T // Q_BLOCK))
    return out.transpose(1, 0, 2, 3).reshape(bsz, T, H * dk)


def short_conv_mixer(gate_b, gate_c, u, w):
    return gate_b * causal_depthwise_conv(gate_c * u, w)


def ssd_mixer(z, xbc, dt_raw, conv_w, conv_b, dt_bias, a_log, d_skip, norm_g):
    f32 = jnp.float32
    bsz, T, _ = xbc.shape
    R = SSD_HEADS // SSD_GROUPS
    nc = T // SSD_CHUNK
    xbc = jax.nn.silu(causal_depthwise_conv(xbc, conv_w) + conv_b)
    xs, bm, cm = jnp.split(xbc, [D_INNER, D_INNER + SSD_GROUPS * SSD_STATE], axis=-1)
    xs = xs.reshape(bsz, T, SSD_GROUPS, R, SSD_HEAD_DIM).astype(f32)
    bm = bm.reshape(bsz, nc, SSD_CHUNK, SSD_GROUPS, SSD_STATE).astype(f32)
    cm = cm.reshape(bsz, nc, SSD_CHUNK, SSD_GROUPS, SSD_STATE).astype(f32)
    dt = jax.nn.softplus(dt_raw.astype(f32) + dt_bias.astype(f32)).reshape(bsz, T, SSD_GROUPS, R)
    a = -jnp.exp(a_log.astype(f32)).reshape(SSD_GROUPS, R)
    xdt = (xs * dt[..., None]).reshape(bsz, nc, SSD_CHUNK, SSD_GROUPS, R, SSD_HEAD_DIM)
    da = (dt * a).reshape(bsz, nc, SSD_CHUNK, SSD_GROUPS, R).transpose(0, 3, 4, 1, 2)
    acs = jnp.cumsum(da, axis=-1)
    causal = jnp.tril(jnp.ones((SSD_CHUNK, SSD_CHUNK), dtype=bool))
    seg = jnp.exp(jnp.where(causal, acs[..., :, None] - acs[..., None, :], -jnp.inf))
    cb = jnp.einsum('bclgn,bcsgn->bgcls', cm, bm)
    y_diag = jnp.einsum('bgrcls,bcsgrp->bclgrp', cb[:, :, None] * seg, xdt)
    decay_to_end = jnp.exp(acs[..., -1:] - acs)
    chunk_states = jnp.einsum('bclgn,bgrcl,bclgrp->cbgrpn', bm, decay_to_end, xdt)
    chunk_decay = jnp.exp(acs[..., -1]).transpose(3, 0, 1, 2)

    def step(h, inp):
        st, dec = inp
        return h * dec[..., None, None] + st, h

    h0 = jnp.zeros(chunk_states.shape[1:], f32)
    _, states_in = lax.scan(step, h0, (chunk_states, chunk_decay))
    y_off = jnp.einsum('bclgn,cbgrpn,bgrcl->bclgrp', cm, states_in, jnp.exp(acs))
    y = (y_diag + y_off).reshape(bsz, T, SSD_GROUPS, R, SSD_HEAD_DIM) \
        + d_skip.astype(f32).reshape(SSD_GROUPS, R, 1) * xs
    y = y.reshape(bsz, T, D_INNER) * jax.nn.silu(z.astype(f32))
    yg = y.reshape(bsz, T, SSD_GROUPS, D_INNER // SSD_GROUPS)
    yg = yg * lax.rsqrt(jnp.mean(yg * yg, axis=-1, keepdims=True) + EPS)
    return (yg.reshape(bsz, T, D_INNER) * norm_g.astype(f32)).astype(z.dtype)


def setup_inputs(seed: int = 0) -> dict:
    key = jax.random.key(seed)
    ks = jax.random.split(key, 32)
    L = DEPTH

    def nrm(k, shape, s):
        return jax.random.normal(k, shape, jnp.float32) * s

    def gain(k, n):
        return 1.0 + nrm(k, (L, n), 0.02)

    dt0 = jnp.exp(jax.random.uniform(ks[17], (L, SSD_HEADS), jnp.float32,
                                     minval=float(np.log(1e-3)), maxval=float(np.log(1e-1))))
    offs = jax.random.randint(ks[2], (BATCH, 1), 0, 2048, dtype=jnp.int32)
    return {
        'x': nrm(ks[0], (BATCH, SEQ, D_MODEL), 1.0),
        'p': nrm(ks[1], (DEPTH, BATCH, SEQ, PLE_DIM), 1.0),
        'positions': jnp.arange(SEQ, dtype=jnp.int32)[None, :] + offs,
        'g_mix': gain(ks[3], D_MODEL),
        'w_in': nrm(ks[4], (L, D_MODEL, PROJ_TOTAL), D_MODEL ** -0.5),
        'nsa_pe_k': nrm(ks[5], (L, CMP_BLOCK, HEAD_DIM), 0.1),
        'nsa_pe_v': nrm(ks[6], (L, CMP_BLOCK, HEAD_DIM), 0.1),
        'phi_k_w1': nrm(ks[7], (L, CMP_BLOCK * HEAD_DIM, PHI_HIDDEN), (CMP_BLOCK * HEAD_DIM) ** -0.5),
        'phi_k_b1': nrm(ks[8], (L, PHI_HIDDEN), 0.01),
        'phi_k_w2': nrm(ks[9], (L, PHI_HIDDEN, HEAD_DIM), PHI_HIDDEN ** -0.5),
        'phi_k_b2': nrm(ks[10], (L, HEAD_DIM), 0.01),
        'phi_v_w1': nrm(ks[11], (L, CMP_BLOCK * HEAD_DIM, PHI_HIDDEN), (CMP_BLOCK * HEAD_DIM) ** -0.5),
        'phi_v_b1': nrm(ks[12], (L, PHI_HIDDEN), 0.01),
        'phi_v_w2': nrm(ks[13], (L, PHI_HIDDEN, HEAD_DIM), PHI_HIDDEN ** -0.5),
        'phi_v_b2': nrm(ks[14], (L, HEAD_DIM), 0.01),
        'sconv_w': nrm(ks[15], (L, CONV_WIDTH, CONV_DIM), CONV_WIDTH ** -0.5),
        'ssd_conv_w': nrm(ks[16], (L, SSD_CONV, SSD_XBC), SSD_CONV ** -0.5),
        'ssd_conv_b': nrm(ks[18], (L, SSD_XBC), 0.01),
        'ssd_dt_bias': dt0 + jnp.log(-jnp.expm1(-dt0)),
        'ssd_a_log': jnp.log(jax.random.uniform(ks[19], (L, SSD_HEADS), jnp.float32, minval=1.0, maxval=16.0)),
        'ssd_d': 1.0 + nrm(ks[20], (L, SSD_HEADS), 0.1),
        'ssd_norm_g': gain(ks[21], D_INNER),
        'w_o': nrm(ks[22], (L, D_MODEL, D_MODEL), D_MODEL ** -0.5),
        'g_mlp': gain(ks[23], D_MODEL),
        'w_up': nrm(ks[24], (L, D_MODEL, D_FF), D_MODEL ** -0.5),
        'w_down': nrm(ks[25], (L, D_FF, D_MODEL), D_FF ** -0.5),
        'g_ple': gain(ks[26], D_MODEL),
        'w_ple': nrm(ks[27], (L, PLE_DIM, D_MODEL), PLE_DIM ** -0.5),
        'w_ple_gate': nrm(ks[28], (L, D_MODEL, D_MODEL), D_MODEL ** -0.5),
        'g_final': 1.0 + nrm(ks[29], (D_MODEL,), 0.02),
    }


def reference(x, p, positions, g_mix, w_in, nsa_pe_k, nsa_pe_v,
              phi_k_w1, phi_k_b1, phi_k_w2, phi_k_b2, phi_v_w1, phi_v_b1, phi_v_w2, phi_v_b2,
              sconv_w, ssd_conv_w, ssd_conv_b, ssd_dt_bias, ssd_a_log, ssd_d, ssd_norm_g,
              w_o, g_mlp, w_up, w_down, g_ple, w_ple, w_ple_gate, g_final):
    bsz, T, _ = x.shape
    cos, sin = rope_tables(positions)
    splits = [int(v) for v in np.cumsum(PROJ_WIDTHS)[:-1]]
    for i in range(DEPTH):
        h = rms_norm(x, g_mix[i])
        proj = h @ w_in[i]
        (q, k_c, v_c, k_s, v_s, k_w, v_w, g_nsa, cb_gate, cc_gate, c_u,
         s_z, s_xbc, s_dt, g_merge) = jnp.split(proj, splits, axis=-1)
        q = apply_rope(q.reshape(bsz, T, N_HEADS, HEAD_DIM), cos, sin)
        kv = lambda a: a.reshape(bsz, T, N_KV_GROUPS, HEAD_DIM)
        k_c = apply_rope(kv(k_c), cos, sin)
        k_s = apply_rope(kv(k_s), cos, sin)
        k_w = apply_rope(kv(k_w), cos, sin)
        y_a = nsa_mixer(q, k_c, kv(v_c), k_s, kv(v_s), k_w, kv(v_w),
                        jax.nn.sigmoid(g_nsa).reshape(bsz, T, N_HEADS, 3),
                        nsa_pe_k[i], nsa_pe_v[i],
                        phi_k_w1[i], phi_k_b1[i], phi_k_w2[i], phi_k_b2[i],
                        phi_v_w1[i], phi_v_b1[i], phi_v_w2[i], phi_v_b2[i]).astype(x.dtype)
        y_b = short_conv_mixer(cb_gate, cc_gate, c_u, sconv_w[i]).astype(x.dtype)
        y_c = ssd_mixer(s_z, s_xbc, s_dt, ssd_conv_w[i], ssd_conv_b[i], ssd_dt_bias[i],
                        ssd_a_log[i], ssd_d[i], ssd_norm_g[i]).astype(x.dtype)
        gm = jax.nn.sigmoid(g_merge).reshape(bsz, T, N_BRANCHES, D_MODEL)
        merged = gm[:, :, 0] * y_a + gm[:, :, 1] * y_b + gm[:, :, 2] * y_c
        x = x + (merged @ w_o[i]).astype(x.dtype)
        h = rms_norm(x, g_mlp[i])
        x = x + (jnp.square(jax.nn.relu(h @ w_up[i])) @ w_down[i]).astype(x.dtype)
        gate = jax.nn.sigmoid(rms_norm(x, g_ple[i]) @ w_ple_gate[i])
        x = x + ((p[i] @ w_ple[i]) * gate).astype(x.dtype)
    return rms_norm(x, g_final)
```

```python
import functools

import numpy as np
import jax
import jax.numpy as jnp
from jax import lax
from jax.experimental import pallas as pl
from jax.experimental.pallas import tpu as pltpu

f32 = jnp.float32
bf16 = jnp.bfloat16

D = 2048
N_HEADS = 16
DK = 128
G = 4
R = N_HEADS // G
CMP_BLOCK = 32
CMP_STRIDE = 16
SLC = 64
N_SELECT = 16
WINDOW = 512
FORCE_BONUS = 1.0e4
CONV_W = 3
SSD_HEADS = 32
SSD_P = 64
SSD_G = 4
SSD_N = 128
SSD_CONV = 4
SSD_L = 128
D_FF = 4 * D
PLE = 256
EPS = 1e-6
NSB = 128
NEG = -1e30

VMEM_LIMIT = 56 * 1024 * 1024

_SRC = {
    'q': (0, 2048), 'k_c': (2048, 512), 'v_c': (2560, 512), 'k_s': (3072, 512), 'v_s': (3584, 512),
    'k_w': (4096, 512), 'v_w': (4608, 512), 'g_nsa': (5120, 48),
    'cb': (5168, 2048), 'cc': (7216, 2048), 'cu': (9264, 2048),
    's_z': (11312, 2048), 'xs': (13360, 2048), 'bm': (15408, 512), 'cm': (15920, 512),
    's_dt': (16432, 32), 'gm0': (16464, 2048), 'gm1': (18512, 2048), 'gm2': (20560, 2048),
}
_ORDER = ['q', 'cb', 'cc', 'cu', 's_z', 'xs', 'gm0', 'gm1', 'gm2',
          'k_c', 'v_c', 'k_s', 'v_s', 'k_w', 'v_w', 'bm', 'cm']
_OFF = {}
_o = 0
for _n in _ORDER:
    _OFF[_n] = _o
    _o += _SRC[_n][1]
N_MAIN = _o
N_SMALL = G * 128 + 128


def _blk(name, width):
    off = _OFF[name]
    assert off % width == 0
    return off // width


def _cparams(sem):
    return pltpu.CompilerParams(dimension_semantics=sem, vmem_limit_bytes=VMEM_LIMIT)


def _dot(a, b):
    return jnp.dot(a, b, preferred_element_type=f32)


def _dot_nt(a, b):
    return lax.dot_general(a, b, (((1,), (1,)), ((), ())), preferred_element_type=f32)


def _sigmoid(x):
    return 1.0 / (1.0 + jnp.exp(-x))


def _split2(x):
    hi = x.astype(bf16)
    lo = (x - hi.astype(f32)).astype(bf16)
    return hi, lo


def _split3(x):
    hi = x.astype(bf16)
    r1 = x - hi.astype(f32)
    mid = r1.astype(bf16)
    lo = (r1 - mid.astype(f32)).astype(bf16)
    return hi, mid, lo


def _rms(x, g):
    ms = jnp.mean(x * x, axis=-1, keepdims=True)
    return x * lax.rsqrt(ms + EPS) * g


def _proj_kernel(x_ref, g_ref, w_ref, ws_ref, o_ref, os_ref, h_ref):
    @pl.when(pl.program_id(1) == 0)
    def _():
        h = _rms(x_ref[...], g_ref[...]).astype(bf16)
        h_ref[...] = h
        os_ref[...] = _dot(h, ws_ref[...])

    o_ref[...] = _dot(h_ref[...], w_ref[...])


def _proj(x2, g, w_main, w_small, tm, tn):
    M = x2.shape[0]
    return pl.pallas_call(
        _proj_kernel,
        grid=(M // tm, N_MAIN // tn),
        in_specs=[
            pl.BlockSpec((tm, D), lambda i, j: (i, 0)),
            pl.BlockSpec((1, D), lambda i, j: (0, 0)),
            pl.BlockSpec((D, tn), lambda i, j: (0, j)),
            pl.BlockSpec((D, N_SMALL), lambda i, j: (0, 0)),
        ],
        out_specs=[
            pl.BlockSpec((tm, tn), lambda i, j: (i, j)),
            pl.BlockSpec((tm, N_SMALL), lambda i, j: (i, 0)),
        ],
        out_shape=[jax.ShapeDtypeStruct((M, N_MAIN), f32),
                   jax.ShapeDtypeStruct((M, N_SMALL), f32)],
        scratch_shapes=[pltpu.VMEM((tm, D), bf16)],
        compiler_params=_cparams(("parallel", "arbitrary")),
        name="proj",
    )(x2, g, w_main, w_small)


def _rope(x, cosf, sinf):
    return x * cosf + pltpu.roll(x, DK // 2, axis=1) * sinf


def _prep_kernel(kc_ref, vc_ref, ks_ref, vs_ref, kw_ref, vw_ref, cos_ref, sin_ref,
                 kco_ref, vco_ref, ksa_ref, vso_ref, kwo_ref, vwo_ref, *, tt):
    cosf = cos_ref[...]
    sinf = sin_ref[...]
    kco_ref[...] = _rope(kc_ref[...], cosf, sinf)
    vco_ref[...] = vc_ref[...]
    ksa_ref[:, 0:DK] = _rope(ks_ref[...], cosf, sinf).astype(bf16)
    t = pl.program_id(2) * tt + lax.broadcasted_iota(jnp.int32, (tt, NSB), 0)
    j = lax.broadcasted_iota(jnp.int32, (tt, NSB), 1)
    ksa_ref[:, DK:DK + NSB] = jnp.where(t // SLC == j, 1.0, 0.0).astype(bf16)
    vso_ref[...] = vs_ref[...].astype(bf16)
    kwo_ref[...] = _rope(kw_ref[...], cosf, sinf).astype(bf16)
    vwo_ref[...] = vw_ref[...].astype(bf16)


def _prep(proj, cosf, sinf, B, T, tt):
    nt = T // tt

    def col(name):
        base = _blk(name, DK)
        return pl.BlockSpec((tt, DK), lambda b, g, i: (b * nt + i, base + g))

    tab = pl.BlockSpec((tt, DK), lambda b, g, i: (b * nt + i, 0))

    def out(w):
        return pl.BlockSpec((None, None, tt, w), lambda b, g, i: (b, g, i, 0))

    return pl.pallas_call(
        functools.partial(_prep_kernel, tt=tt),
        grid=(B, G, nt),
        in_specs=[col('k_c'), col('v_c'), col('k_s'), col('v_s'), col('k_w'), col('v_w'), tab, tab],
        out_specs=[out(DK), out(DK), out(DK + NSB), out(DK), out(DK), out(DK)],
        out_shape=[jax.ShapeDtypeStruct((B, G, T, DK), f32),
                   jax.ShapeDtypeStruct((B, G, T, DK), f32),
                   jax.ShapeDtypeStruct((B, G, T, DK + NSB), bf16),
                   jax.ShapeDtypeStruct((B, G, T, DK), bf16),
                   jax.ShapeDtypeStruct((B, G, T, DK), bf16),
                   jax.ShapeDtypeStruct((B, G, T, DK), bf16)],
        compiler_params=_cparams(("parallel", "parallel", "parallel")),
        name="kv_prep",
    )(proj, proj, proj, proj, proj, proj, cosf, sinf)


def _compress_one(x_ref, pe_ref, w1_ref, b1_ref, w2_ref, b2_ref, o_ref):
    nc = x_ref.shape[0]
    half = CMP_STRIDE * DK
    x = x_ref[...].astype(bf16)
    first = _dot(x, w1_ref[0:half, :])
    second = _dot(x, w1_ref[half:2 * half, :])
    second = pltpu.roll(second, nc - 1, axis=0)
    pe = _dot(pe_ref[...], w1_ref[...])[0:1, :]
    h = first + second + pe + b1_ref[...]
    h = h * _sigmoid(h)
    o_ref[...] = (_dot(h.astype(bf16), w2_ref[...]) + b2_ref[...]).astype(bf16)


def _compress_kernel(k_ref, v_ref, pek, w1k, b1k, w2k, b2k, pev, w1v, b1v, w2v, b2v, ko_ref, vo_ref):
    _compress_one(k_ref, pek, w1k, b1k, w2k, b2k, ko_ref)
    _compress_one(v_ref, pev, w1v, b1v, w2v, b2v, vo_ref)


def _compress(kc2, vc2, wk, wv):
    B, _, nc, width = kc2.shape
    xin = pl.BlockSpec((None, None, nc, width), lambda b, g: (b, g, 0, 0))

    def full(a):
        return pl.BlockSpec(a.shape, lambda b, g: (0,) * a.ndim)

    out = pl.BlockSpec((None, None, nc, DK), lambda b, g: (b, g, 0, 0))
    return pl.pallas_call(
        _compress_kernel,
        grid=(B, G),
        in_specs=[xin, xin] + [full(a) for a in wk] + [full(a) for a in wv],
        out_specs=[out, out],
        out_shape=[jax.ShapeDtypeStruct((B, G, nc, DK), bf16)] * 2,
        compiler_params=_cparams(("parallel", "parallel")),
        name="compress",
    )(kc2, vc2, *wk, *wv)


def _nsa_kernel(q_ref, cos_ref, sin_ref, gate_ref, ovl_ref, kcmp_ref, vcmp_ref,
                ksa_ref, vs_ref, kw_ref, vw_ref, o_ref,
                qa_ref, m_ref, l_ref, acc_ref, out_ref, *, tq, tk, n_sel):
    q0 = pl.program_id(2) * tq
    rows = R * tq
    scale = DK ** -0.5
    cosf = cos_ref[...]
    sinf = sin_ref[...]
    for r in range(R):
        qr = _rope(q_ref[:, r * DK:(r + 1) * DK], cosf, sinf) * scale
        qa_ref[r * tq:(r + 1) * tq, 0:DK] = qr.astype(bf16)
    qs = qa_ref[:, 0:DK]
    t4 = q0 + (lax.broadcasted_iota(jnp.int32, (rows, 1), 0) & (tq - 1))
    t1 = q0 + lax.broadcasted_iota(jnp.int32, (tq, 1), 0)
    gate = _sigmoid(gate_ref[...])

    def head_gate(c):
        return jnp.concatenate([gate[:, 3 * r + c:3 * r + c + 1] for r in range(R)], axis=0)

    nc = kcmp_ref.shape[0]
    s = _dot_nt(qs, kcmp_ref[...])
    n_idx = lax.broadcasted_iota(jnp.int32, (1, nc), 1)
    vis = (n_idx * CMP_STRIDE + (CMP_BLOCK - 1)) <= t4
    s = jnp.where(vis, s, NEG)
    m = jnp.max(s, axis=-1, keepdims=True)
    e = jnp.where(vis, jnp.exp(s - m), 0.0)
    d = jnp.sum(e, axis=-1, keepdims=True)
    p = e / jnp.where(d > 0, d, 1.0)
    out_ref[...] = head_gate(0) * _dot(p.astype(bf16), vcmp_ref[...])
    psum = p[0:tq]
    for r in range(1, R):
        psum = psum + p[r * tq:(r + 1) * tq]
    p_hi, p_lo = _split2(psum)
    imp = _dot(p_hi, ovl_ref[...]) + _dot(p_lo, ovl_ref[...])

    j_idx = lax.broadcasted_iota(jnp.int32, (1, NSB), 1)
    j_f = j_idx.astype(f32)
    cur = t1 // SLC
    valid = (j_idx * SLC) <= t1
    forced = (j_idx == 0) | (j_idx == cur) | (j_idx == cur - 1)
    score = jnp.where(valid, imp + jnp.where(forced, FORCE_BONUS, 0.0), -jnp.inf)
    sel = jnp.zeros((tq, NSB), f32)
    for _ in range(n_sel):
        mx = jnp.max(score, axis=-1, keepdims=True)
        first = jnp.min(jnp.where(score == mx, j_f, float(NSB)), axis=-1, keepdims=True)
        hit = j_f == first
        sel = jnp.where(hit, 1.0, sel)
        score = jnp.where(hit, -jnp.inf, score)
    nsel = jnp.where(sel > 0.5, 0.0, NEG).astype(bf16)
    for r in range(R):
        qa_ref[r * tq:(r + 1) * tq, DK:DK + NSB] = nsel

    def reset():
        m_ref[...] = jnp.full((rows, 1), NEG, f32)
        l_ref[...] = jnp.zeros((rows, 1), f32)
        acc_ref[...] = jnp.zeros((rows, DK), f32)

    def update(s, v):
        m_prev = m_ref[...]
        m_new = jnp.maximum(m_prev, jnp.max(s, axis=-1, keepdims=True))
        alpha = jnp.exp(m_prev - m_new)
        pt = jnp.exp(s - m_new)
        l_ref[...] = alpha * l_ref[...] + jnp.sum(pt, axis=-1, keepdims=True)
        acc_ref[...] = alpha * acc_ref[...] + _dot(pt.astype(bf16), v)
        m_ref[...] = m_new

    reset()
    kd = q0 // tk
    kpos = lax.broadcasted_iota(jnp.int32, (1, tk), 1)
    start = pl.multiple_of(kd * tk, tk)
    s = _dot_nt(qa_ref[...], ksa_ref[pl.ds(start, tk), :])
    s = jnp.where(start + kpos <= t4, s, NEG)
    update(s, vs_ref[pl.ds(start, tk), :])

    def sel_body(kt, carry):
        st = pl.multiple_of(kt * tk, tk)
        update(_dot_nt(qa_ref[...], ksa_ref[pl.ds(st, tk), :]), vs_ref[pl.ds(st, tk), :])
        return carry

    lax.fori_loop(0, kd, sel_body, 0)
    out_ref[...] += head_gate(1) * (acc_ref[...] / l_ref[...])

    reset()
    kq = lax.broadcasted_iota(jnp.int32, (1, tq), 1)
    st0 = pl.multiple_of(q0, tq)
    s = _dot_nt(qs, kw_ref[pl.ds(st0, tq), :])
    s = jnp.where(q0 + kq <= t4, s, NEG)
    update(s, vw_ref[pl.ds(st0, tq), :])
    for w in range(1, -(-(WINDOW - 1) // tq) + 1):
        @pl.when(q0 - w * tq >= 0)
        def _():
            stw = pl.multiple_of(q0 - w * tq, tq)
            sw = _dot_nt(qs, kw_ref[pl.ds(stw, tq), :])
            sw = jnp.where(stw + kq > t4 - WINDOW, sw, NEG)
            update(sw, vw_ref[pl.ds(stw, tq), :])
    out_ref[...] += head_gate(2) * (acc_ref[...] / l_ref[...])

    for r in range(R):
        o_ref[:, r * DK:(r + 1) * DK] = out_ref[r * tq:(r + 1) * tq, :]


def _nsa(proj, small, cosf, sinf, ovl, kcmp, vcmp, ksa, vs, kw, vw, B, T, tq, tk):
    nq = T // tq
    nc = kcmp.shape[2]
    rows = R * tq
    n_sel = min(N_SELECT, T // SLC)

    def per_bg(n, w):
        return pl.BlockSpec((None, None, n, w), lambda b, g, i: (b, g, 0, 0))

    tab = pl.BlockSpec((tq, DK), lambda b, g, i: (b * nq + i, 0))
    return pl.pallas_call(
        functools.partial(_nsa_kernel, tq=tq, tk=tk, n_sel=n_sel),
        grid=(B, G, nq),
        in_specs=[
            pl.BlockSpec((tq, R * DK), lambda b, g, i: (b * nq + i, g)),
            tab, tab,
            pl.BlockSpec((tq, 128), lambda b, g, i: (b * nq + i, g)),
            pl.BlockSpec((nc, NSB), lambda b, g, i: (0, 0)),
            per_bg(nc, DK), per_bg(nc, DK),
            per_bg(T, DK + NSB), per_bg(T, DK), per_bg(T, DK), per_bg(T, DK),
        ],
        out_specs=pl.BlockSpec((tq, R * DK), lambda b, g, i: (b * nq + i, g)),
        out_shape=jax.ShapeDtypeStruct((B * T, D), f32),
        scratch_shapes=[pltpu.VMEM((rows, DK + NSB), bf16),
                        pltpu.VMEM((rows, 1), f32),
                        pltpu.VMEM((rows, 1), f32),
                        pltpu.VMEM((rows, DK), f32),
                        pltpu.VMEM((rows, DK), f32)],
        compiler_params=_cparams(("parallel", "parallel", "arbitrary")),
        name="nsa_attention",
    )(proj, cosf, sinf, small, ovl, kcmp, vcmp, ksa, vs, kw, vw)


def _ssd_kernel(z_ref, xs_ref, bm_ref, cm_ref, dt_ref, xsh_ref, bmh_ref, cmh_ref,
                wx_ref, bx_ref, wb_ref, bb_ref, wc_ref, bc_ref,
                dtb_ref, alog_ref, dskip_ref, ng_ref, eh_ref,
                o_ref, st_ref, ext_ref):
    c = pl.program_id(1)
    L = SSD_L
    HP = 2 * SSD_P
    GW = D // SSD_G

    @pl.when(c == 0)
    def _():
        st_ref[...] = jnp.zeros_like(st_ref)

    def conv_silu(u_ref, h_ref, w_ref, b_ref):
        width = u_ref.shape[1]
        ext_ref[0:8, 0:width] = jnp.where(c == 0, 0.0, h_ref[...])
        ext_ref[8:8 + L, 0:width] = u_ref[...]
        y = b_ref[...]
        for j in range(SSD_CONV):
            lo = 8 - (SSD_CONV - 1) + j
            y = y + w_ref[j:j + 1, :] * ext_ref[lo:lo + L, 0:width]
        return y * _sigmoid(y)

    xs = conv_silu(xs_ref, xsh_ref, wx_ref, bx_ref)
    bm = conv_silu(bm_ref, bmh_ref, wb_ref, bb_ref)
    cm = conv_silu(cm_ref, cmh_ref, wc_ref, bc_ref)

    lane = lax.broadcasted_iota(jnp.int32, (1, 128), 1)
    xdt_in = dt_ref[...] + dtb_ref[...]
    dt = jnp.maximum(xdt_in, 0.0) + jnp.log1p(jnp.exp(-jnp.abs(xdt_in)))
    dt = jnp.where(lane < SSD_HEADS, dt, 0.0)
    da = dt * (-jnp.exp(alog_ref[...]))

    row = lax.broadcasted_iota(jnp.int32, (L, L), 0)
    colm = lax.broadcasted_iota(jnp.int32, (L, L), 1)
    causal = colm <= row
    tril = jnp.where(causal, 1.0, 0.0).astype(bf16)
    d_hi, d_mid, d_lo = _split3(da)
    acs = _dot(tril, d_hi) + _dot(tril, d_mid) + _dot(tril, d_lo)
    acs_t = acs.T
    last = acs[L - 1:L, :]
    eacs = jnp.exp(acs)
    dec = jnp.exp(last - acs)
    cdec = jnp.exp(jnp.broadcast_to(last, (8, 128)))

    eh = eh_ref[...]
    dt_x = _dot(dt.astype(bf16), eh)
    eacs_x = _dot(eacs.astype(bf16), eh)
    dec_x = _dot(dec.astype(bf16), eh)
    c_hi, c_lo = _split2(cdec)
    cdec_x = (_dot(c_hi, eh) + _dot(c_lo, eh))[0:1, :]

    xdt = xs * dt_x
    xdt_b = xdt.astype(bf16)
    xdec_b = (xdt * dec_x).astype(bf16)
    lane_hp = lax.broadcasted_iota(jnp.int32, (L, HP), 1)

    for g in range(SSD_G):
        cm_g = cm[:, g * SSD_N:(g + 1) * SSD_N].astype(bf16)
        bm_g = bm[:, g * SSD_N:(g + 1) * SSD_N]
        cb = _dot_nt(cm_g, bm_g.astype(bf16))
        s_in = st_ref[g]
        y_off = _dot(cm_g, s_in.astype(bf16))
        s_new = _dot(bm_g.T.astype(bf16), xdec_b[:, g * GW:(g + 1) * GW])
        st_ref[g] = s_in * cdec_x[:, g * GW:(g + 1) * GW] + s_new
        for pp in range(GW // HP):
            h0 = g * (GW // SSD_P) + 2 * pp
            c0 = g * GW + pp * HP
            ms = []
            for h in (h0, h0 + 1):
                diff = acs[:, h:h + 1] - acs_t[h:h + 1, :]
                ms.append(jnp.where(causal, cb * jnp.exp(jnp.where(causal, diff, NEG)), 0.0))
            lhs = jnp.concatenate(ms, axis=1).astype(bf16)
            xp = xdt_b[:, c0:c0 + HP]
            zero = jnp.zeros_like(xp)
            rhs = jnp.concatenate([jnp.where(lane_hp < SSD_P, xp, zero),
                                   jnp.where(lane_hp >= SSD_P, xp, zero)], axis=0)
            y = (_dot(lhs, rhs) + eacs_x[:, c0:c0 + HP] * y_off[:, pp * HP:(pp + 1) * HP]
                 + dskip_ref[:, c0:c0 + HP] * xs[:, c0:c0 + HP])
            o_ref[:, c0:c0 + HP] = y

    z = z_ref[...]
    y = o_ref[...] * (z * _sigmoid(z))
    for g in range(SSD_G):
        yg = y[:, g * GW:(g + 1) * GW]
        ms = jnp.mean(yg * yg, axis=-1, keepdims=True)
        o_ref[:, g * GW:(g + 1) * GW] = yg * lax.rsqrt(ms + EPS) * ng_ref[:, g * GW:(g + 1) * GW]


def _ssd(proj, small, wts, B, T):
    nch = T // SSD_L
    L = SSD_L

    def main(name, w):
        base = _blk(name, w)
        return pl.BlockSpec((L, w), lambda b, c: (b * nch + c, base))

    def halo(name, w):
        base = _blk(name, w)
        return pl.BlockSpec((8, w), lambda b, c: (jnp.maximum((b * nch + c) * (L // 8) - 1, 0), base))

    def full(a):
        return pl.BlockSpec(a.shape, lambda b, c: (0,) * a.ndim)

    gn = SSD_G * SSD_N
    return pl.pallas_call(
        _ssd_kernel,
        grid=(B, nch),
        in_specs=[main('s_z', D), main('xs', D), main('bm', gn), main('cm', gn),
                  pl.BlockSpec((L, 128), lambda b, c: (b * nch + c, G)),
                  halo('xs', D), halo('bm', gn), halo('cm', gn)] + [full(a) for a in wts],
        out_specs=pl.BlockSpec((L, D), lambda b, c: (b * nch + c, 0)),
        out_shape=jax.ShapeDtypeStruct((B * T, D), f32),
        scratch_shapes=[pltpu.VMEM((SSD_G, SSD_N, D // SSD_G), f32),
                        pltpu.VMEM((8 + L, D), f32)],
        compiler_params=_cparams(("parallel", "arbitrary")),
        name="ssd",
    )(proj, proj, proj, proj, small, proj, proj, proj, *wts)


def _merge_kernel(ya_ref, cb_ref, cc_ref, cu_ref, cch_ref, cuh_ref, yc_ref,
                  g0_ref, g1_ref, g2_ref, x_ref, cw_ref, wo_ref, o_ref, ext_ref, *, tm, tiles_per_seq):
    first = (pl.program_id(0) % tiles_per_seq) == 0
    ext_ref[0:8, :] = jnp.where(first, 0.0, cch_ref[...] * cuh_ref[...])
    ext_ref[8:8 + tm, :] = cc_ref[...] * cu_ref[...]
    conv = cw_ref[0:1, :] * ext_ref[6:6 + tm, :]
    for j in range(1, CONV_W):
        conv = conv + cw_ref[j:j + 1, :] * ext_ref[6 + j:6 + j + tm, :]
    yb = cb_ref[...] * conv
    merged = (_sigmoid(g0_ref[...]) * ya_ref[...] + _sigmoid(g1_ref[...]) * yb
              + _sigmoid(g2_ref[...]) * yc_ref[...])
    o_ref[...] = x_ref[...] + _dot(merged.astype(bf16), wo_ref[...])


def _merge(ya, yc, proj, x2, cw, wo, T, tm):
    M = x2.shape[0]

    def col(name):
        base = _blk(name, D)
        return pl.BlockSpec((tm, D), lambda i: (i, base))

    def halo(name):
        base = _blk(name, D)
        return pl.BlockSpec((8, D), lambda i: (jnp.maximum(i * (tm // 8) - 1, 0), base))

    row = pl.BlockSpec((tm, D), lambda i: (i, 0))
    return pl.pallas_call(
        functools.partial(_merge_kernel, tm=tm, tiles_per_seq=T // tm),
        grid=(M // tm,),
        in_specs=[row, col('cb'), col('cc'), col('cu'), halo('cc'), halo('cu'), row,
                  col('gm0'), col('gm1'), col('gm2'), row,
                  pl.BlockSpec((CONV_W, D), lambda i: (0, 0)),
                  pl.BlockSpec((D, D), lambda i: (0, 0))],
        out_specs=row,
        out_shape=jax.ShapeDtypeStruct((M, D), f32),
        scratch_shapes=[pltpu.VMEM((8 + tm, D), f32)],
        compiler_params=_cparams(("parallel",)),
        name="merge_oproj",
    )(ya, proj, proj, proj, proj, proj, yc, proj, proj, proj, x2, cw, wo)


def _ffn_kernel(x_ref, g_ref, wu_ref, wd_ref, o_ref, h_ref):
    @pl.when(pl.program_id(1) == 0)
    def _():
        x = x_ref[...]
        h_ref[...] = _rms(x, g_ref[...]).astype(bf16)
        o_ref[...] = x

    u = jnp.maximum(_dot(h_ref[...], wu_ref[...]), 0.0)
    o_ref[...] += _dot((u * u).astype(bf16), wd_ref[...])


def _ffn(x2, g, wu, wd, tm, tf):
    M = x2.shape[0]
    return pl.pallas_call(
        _ffn_kernel,
        grid=(M // tm, D_FF // tf),
        in_specs=[pl.BlockSpec((tm, D), lambda i, j: (i, 0)),
                  pl.BlockSpec((1, D), lambda i, j: (0, 0)),
                  pl.BlockSpec((D, tf), lambda i, j: (0, j)),
                  pl.BlockSpec((tf, D), lambda i, j: (j, 0))],
        out_specs=pl.BlockSpec((tm, D), lambda i, j: (i, 0)),
        out_shape=jax.ShapeDtypeStruct((M, D), f32),
        scratch_shapes=[pltpu.VMEM((tm, D), bf16)],
        compiler_params=_cparams(("parallel", "arbitrary")),
        name="ffn",
    )(x2, g, wu, wd)


def _ple_kernel(x_ref, p_ref, g_ref, wp_ref, wg_ref, gf_ref, o_ref, *, final):
    x = x_ref[...]
    gate = _sigmoid(_dot(_rms(x, g_ref[...]).astype(bf16), wg_ref[...]))
    y = x + _dot(p_ref[...].astype(bf16), wp_ref[...]) * gate
    if final:
        y = _rms(y, gf_ref[...])
    o_ref[...] = y


def _ple(x2, p2, g, wp, wg, gf, tm, final):
    M = x2.shape[0]
    row = pl.BlockSpec((tm, D), lambda i: (i, 0))
    vec = pl.BlockSpec((1, D), lambda i: (0, 0))
    return pl.pallas_call(
        functools.partial(_ple_kernel, final=final),
        grid=(M // tm,),
        in_specs=[row, pl.BlockSpec((tm, PLE), lambda i: (i, 0)), vec,
                  pl.BlockSpec((PLE, D), lambda i: (0, 0)),
                  pl.BlockSpec((D, D), lambda i: (0, 0)), vec],
        out_specs=row,
        out_shape=jax.ShapeDtypeStruct((M, D), f32),
        compiler_params=_cparams(("parallel",)),
        name="ple",
    )(x2, p2, g, wp, wg, gf)


def _prep_w_in(w):
    main = jnp.concatenate([w[:, _SRC[n][0]:_SRC[n][0] + _SRC[n][1]] for n in _ORDER], axis=1)
    a, wd = _SRC['g_nsa']
    gn = w[:, a:a + wd].reshape(D, G, R * 3)
    gn = jnp.pad(gn, ((0, 0), (0, 0), (0, 128 - R * 3))).reshape(D, G * 128)
    a, wd = _SRC['s_dt']
    dt = jnp.pad(w[:, a:a + wd], ((0, 0), (0, 128 - wd)))
    return main.astype(bf16), jnp.concatenate([gn, dt], axis=1).astype(bf16)


def _overlap_matrix(nc):
    i = np.arange(nc)[:, None]
    j = np.arange(NSB)[None, :]
    ovl = (i * CMP_STRIDE < j * SLC + SLC) & (i * CMP_STRIDE + CMP_BLOCK > j * SLC)
    return jnp.asarray(ovl.astype(np.float32), dtype=bf16)


def _head_expand():
    h = np.arange(128)[:, None]
    ch = np.arange(D)[None, :]
    return jnp.asarray((ch // SSD_P == h).astype(np.float32), dtype=bf16)


def kernel(x, p, positions, g_mix, w_in, nsa_pe_k, nsa_pe_v, phi_k_w1, phi_k_b1, phi_k_w2, phi_k_b2,
           phi_v_w1, phi_v_b1, phi_v_w2, phi_v_b2, sconv_w, ssd_conv_w, ssd_conv_b, ssd_dt_bias,
           ssd_a_log, ssd_d, ssd_norm_g, w_o, g_mlp, w_up, w_down, g_ple, w_ple, w_ple_gate, g_final):
    B, T, _ = x.shape
    depth = w_in.shape[0]
    M = B * T
    assert T % 256 == 0 and T // SLC <= NSB
    tq = tk = 256
    tm_proj = 1024 if M % 1024 == 0 else 256
    tm = 512 if T % 512 == 0 else 256
    tt = 512 if T % 512 == 0 else 256
    nc = T // CMP_STRIDE

    inv_freq = 1.0 / (10000.0 ** (jnp.arange(0, DK, 2, dtype=f32) / DK))
    ang = positions.astype(f32)[..., None] * inv_freq
    cosf = jnp.concatenate([jnp.cos(ang), jnp.cos(ang)], axis=-1).reshape(M, DK)
    sinf = jnp.concatenate([-jnp.sin(ang), jnp.sin(ang)], axis=-1).reshape(M, DK)
    ovl = _overlap_matrix(nc)
    eh = _head_expand()

    def vec(a, n=None):
        a = a.reshape(1, -1).astype(f32)
        return a if n is None else jnp.pad(a, ((0, 0), (0, n - a.shape[1])))

    x2 = x.reshape(M, D)
    for i in range(depth):
        w_main, w_small = _prep_w_in(w_in[i])
        proj, small = _proj(x2, vec(g_mix[i]), w_main, w_small, tm_proj, 1024)

        kc, vc, ksa, vs, kw, vw = _prep(proj, cosf, sinf, B, T, tt)

        def phi(pe, w1, b1, w2, b2):
            pe8 = jnp.broadcast_to(pe.reshape(1, -1), (8, CMP_BLOCK * DK)).astype(bf16)
            return (pe8, w1.astype(bf16), vec(b1), w2.astype(bf16), vec(b2))

        kcmp, vcmp = _compress(kc.reshape(B, G, nc, CMP_STRIDE * DK), vc.reshape(B, G, nc, CMP_STRIDE * DK),
                               phi(nsa_pe_k[i], phi_k_w1[i], phi_k_b1[i], phi_k_w2[i], phi_k_b2[i]),
                               phi(nsa_pe_v[i], phi_v_w1[i], phi_v_b1[i], phi_v_w2[i], phi_v_b2[i]))
        ya = _nsa(proj, small, cosf, sinf, ovl, kcmp, vcmp, ksa, vs, kw, vw, B, T, tq, tk)

        cw, cbias = ssd_conv_w[i], ssd_conv_b[i]
        gn = SSD_G * SSD_N
        ssd_w = (cw[:, :D], vec(cbias[:D]), cw[:, D:D + gn], vec(cbias[D:D + gn]),
                 cw[:, D + gn:], vec(cbias[D + gn:]),
                 vec(ssd_dt_bias[i], 128), vec(ssd_a_log[i], 128),
                 vec(jnp.repeat(ssd_d[i], SSD_P)), vec(ssd_norm_g[i]), eh)
        yc = _ssd(proj, small, ssd_w, B, T)

        x2 = _merge(ya, yc, proj, x2, sconv_w[i], w_o[i].astype(bf16), T, 256)
        x2 = _ffn(x2, vec(g_mlp[i]), w_up[i].astype(bf16), w_down[i].astype(bf16), tm, 1024)
        x2 = _ple(x2, p[i].reshape(M, PLE), vec(g_ple[i]), w_ple[i].astype(bf16),
                  w_ple_gate[i].astype(bf16), vec(g_final), tm, final=(i == depth - 1))
    return x2.reshape(B, T, D)
```

```python
import functools

import numpy as np
import jax
import jax.numpy as jnp
from jax import lax
from jax.experimental import pallas as pl
from jax.experimental.pallas import tpu as pltpu

f32 = jnp.float32
bf16 = jnp.bfloat16

D = 2048
N_HEADS = 16
DK = 128
G = 4
R = N_HEADS // G
CMP_BLOCK = 32
CMP_STRIDE = 16
SLC = 64
N_SELECT = 16
WINDOW = 512
FORCE_BONUS = 1.0e4
CONV_W = 3
SSD_HEADS = 32
SSD_P = 64
SSD_G = 4
SSD_N = 128
SSD_CONV = 4
SSD_L = 128
D_FF = 4 * D
PLE = 256
EPS = 1e-6
NSB = 128
NEG = -1e30
LOG2E = 1.4426950408889634

VMEM_LIMIT = 56 * 1024 * 1024

_SRC = {
    'q': (0, 2048), 'k_c': (2048, 512), 'v_c': (2560, 512), 'k_s': (3072, 512), 'v_s': (3584, 512),
    'k_w': (4096, 512), 'v_w': (4608, 512), 'g_nsa': (5120, 48),
    'cb': (5168, 2048), 'cc': (7216, 2048), 'cu': (9264, 2048),
    's_z': (11312, 2048), 'xs': (13360, 2048), 'bm': (15408, 512), 'cm': (15920, 512),
    's_dt': (16432, 32), 'gm0': (16464, 2048), 'gm1': (18512, 2048), 'gm2': (20560, 2048),
}
_ORDER = ['q', 'cb', 'cc', 'cu', 's_z', 'xs', 'gm0', 'gm1', 'gm2',
          'k_c', 'v_c', 'k_s', 'v_s', 'k_w', 'v_w', 'bm', 'cm']
_OFF = {}
_o = 0
for _n in _ORDER:
    _OFF[_n] = _o
    _o += _SRC[_n][1]
N_MAIN = _o
N_SMALL = G * 128 + 128


def _blk(name, width):
    off = _OFF[name]
    assert off % width == 0
    return off // width


def _cparams(sem):
    return pltpu.CompilerParams(dimension_semantics=sem, vmem_limit_bytes=VMEM_LIMIT)


def _dot(a, b):
    return jnp.dot(a, b, preferred_element_type=f32)


def _dot_nt(a, b):
    return lax.dot_general(a, b, (((1,), (1,)), ((), ())), preferred_element_type=f32)


def _sigmoid(x):
    return 1.0 / (1.0 + jnp.exp(-x))


def _split2(x):
    hi = x.astype(bf16)
    lo = (x - hi.astype(f32)).astype(bf16)
    return hi, lo


def _split3(x):
    hi = x.astype(bf16)
    r1 = x - hi.astype(f32)
    mid = r1.astype(bf16)
    lo = (r1 - mid.astype(f32)).astype(bf16)
    return hi, mid, lo


def _rms(x, g):
    ms = jnp.mean(x * x, axis=-1, keepdims=True)
    return x * lax.rsqrt(ms + EPS) * g


def _proj_kernel(x_ref, g_ref, w_ref, ws_ref, o_ref, os_ref, h_ref):
    @pl.when(pl.program_id(1) == 0)
    def _():
        h = _rms(x_ref[...], g_ref[...]).astype(bf16)
        h_ref[...] = h
        os_ref[...] = _dot(h, ws_ref[...])

    o_ref[...] = _dot(h_ref[...], w_ref[...])


def _proj(x2, g, w_main, w_small, tm, tn):
    M = x2.shape[0]
    return pl.pallas_call(
        _proj_kernel,
        grid=(M // tm, N_MAIN // tn),
        in_specs=[
            pl.BlockSpec((tm, D), lambda i, j: (i, 0)),
            pl.BlockSpec((1, D), lambda i, j: (0, 0)),
            pl.BlockSpec((D, tn), lambda i, j: (0, j)),
            pl.BlockSpec((D, N_SMALL), lambda i, j: (0, 0)),
        ],
        out_specs=[
            pl.BlockSpec((tm, tn), lambda i, j: (i, j)),
            pl.BlockSpec((tm, N_SMALL), lambda i, j: (i, 0)),
        ],
        out_shape=[jax.ShapeDtypeStruct((M, N_MAIN), f32),
                   jax.ShapeDtypeStruct((M, N_SMALL), f32)],
        scratch_shapes=[pltpu.VMEM((tm, D), bf16)],
        compiler_params=_cparams(("parallel", "arbitrary")),
        name="proj",
    )(x2, g, w_main, w_small)


def _rope(x, cosf, sinf):
    return x * cosf + pltpu.roll(x, DK // 2, axis=1) * sinf


def _prep_kernel(kc_ref, vc_ref, ks_ref, vs_ref, kw_ref, vw_ref, cos_ref, sin_ref,
                 kco_ref, vco_ref, ksa_ref, vso_ref, kwo_ref, vwo_ref, *, tt):
    cosf = cos_ref[...]
    sinf = sin_ref[...]
    kco_ref[...] = _rope(kc_ref[...], cosf, sinf)
    vco_ref[...] = vc_ref[...]
    ksa_ref[:, 0:DK] = _rope(ks_ref[...], cosf, sinf).astype(bf16)
    t = pl.program_id(2) * tt + lax.broadcasted_iota(jnp.int32, (tt, NSB), 0)
    j = lax.broadcasted_iota(jnp.int32, (tt, NSB), 1)
    ksa_ref[:, DK:DK + NSB] = jnp.where(t // SLC == j, 1.0, 0.0).astype(bf16)
    vso_ref[...] = vs_ref[...].T.astype(bf16)
    kwo_ref[...] = _rope(kw_ref[...], cosf, sinf).astype(bf16)
    vwo_ref[...] = vw_ref[...].T.astype(bf16)


def _prep(proj, cosf, sinf, B, T, tt):
    nt = T // tt

    def col(name):
        base = _blk(name, DK)
        return pl.BlockSpec((tt, DK), lambda b, g, i: (b * nt + i, base + g))

    tab = pl.BlockSpec((tt, DK), lambda b, g, i: (b * nt + i, 0))

    def out(w):
        return pl.BlockSpec((None, None, tt, w), lambda b, g, i: (b, g, i, 0))

    out_t = pl.BlockSpec((None, None, None, DK, tt), lambda b, g, i: (b, g, i, 0, 0))
    return pl.pallas_call(
        functools.partial(_prep_kernel, tt=tt),
        grid=(B, G, nt),
        in_specs=[col('k_c'), col('v_c'), col('k_s'), col('v_s'), col('k_w'), col('v_w'), tab, tab],
        out_specs=[out(DK), out(DK), out(DK + NSB), out_t, out(DK), out_t],
        out_shape=[jax.ShapeDtypeStruct((B, G, T, DK), f32),
                   jax.ShapeDtypeStruct((B, G, T, DK), f32),
                   jax.ShapeDtypeStruct((B, G, T, DK + NSB), bf16),
                   jax.ShapeDtypeStruct((B, G, nt, DK, tt), bf16),
                   jax.ShapeDtypeStruct((B, G, T, DK), bf16),
                   jax.ShapeDtypeStruct((B, G, nt, DK, tt), bf16)],
        compiler_params=_cparams(("parallel", "parallel", "parallel")),
        name="kv_prep",
    )(proj, proj, proj, proj, proj, proj, cosf, sinf)


def _compress_one(x_ref, pe_ref, w1_ref, b1_ref, w2_ref, b2_ref, o_ref, transpose):
    nc = x_ref.shape[0]
    half = CMP_STRIDE * DK
    x = x_ref[...].astype(bf16)
    first = _dot(x, w1_ref[0:half, :])
    second = _dot(x, w1_ref[half:2 * half, :])
    second = pltpu.roll(second, nc - 1, axis=0)
    pe = _dot(pe_ref[...], w1_ref[...])[0:1, :]
    h = first + second + pe + b1_ref[...]
    h = h * _sigmoid(h)
    y = _dot(h.astype(bf16), w2_ref[...]) + b2_ref[...]
    o_ref[...] = (y.T if transpose else y).astype(bf16)


def _compress_kernel(k_ref, v_ref, pek, w1k, b1k, w2k, b2k, pev, w1v, b1v, w2v, b2v, ko_ref, vo_ref):
    _compress_one(k_ref, pek, w1k, b1k, w2k, b2k, ko_ref, False)
    _compress_one(v_ref, pev, w1v, b1v, w2v, b2v, vo_ref, True)


def _compress(kc2, vc2, wk, wv):
    B, _, nc, width = kc2.shape
    xin = pl.BlockSpec((None, None, nc, width), lambda b, g: (b, g, 0, 0))

    def full(a):
        return pl.BlockSpec(a.shape, lambda b, g: (0,) * a.ndim)

    out = pl.BlockSpec((None, None, nc, DK), lambda b, g: (b, g, 0, 0))
    out_t = pl.BlockSpec((None, None, DK, nc), lambda b, g: (b, g, 0, 0))
    return pl.pallas_call(
        _compress_kernel,
        grid=(B, G),
        in_specs=[xin, xin] + [full(a) for a in wk] + [full(a) for a in wv],
        out_specs=[out, out_t],
        out_shape=[jax.ShapeDtypeStruct((B, G, nc, DK), bf16),
                   jax.ShapeDtypeStruct((B, G, DK, nc), bf16)],
        compiler_params=_cparams(("parallel", "parallel")),
        name="compress",
    )(kc2, vc2, *wk, *wv)


def _nsa_kernel(q_ref, cos_ref, sin_ref, gate_ref, ovl_ref, kcmp_ref, vcmp_ref,
                ksa_ref, vs_ref, kw_ref, vw_ref, o_ref,
                qa_ref, ms_ref, ls_ref, accs_ref, mw_ref, lw_ref, accw_ref, out_ref, sc_ref, mx_ref,
                *, tq, n_sel):
    qi = pl.program_id(2)
    q0 = qi * tq
    heads = [slice(r * tq, (r + 1) * tq) for r in range(R)]
    qscale = DK ** -0.5 * LOG2E
    cosf = cos_ref[...]
    sinf = sin_ref[...]
    for r in range(R):
        qr = _rope(q_ref[:, r * DK:(r + 1) * DK], cosf, sinf) * qscale
        qa_ref[0:DK, heads[r]] = qr.T.astype(bf16)
    t1 = q0 + lax.broadcasted_iota(jnp.int32, (1, tq), 1)
    gate_t = _sigmoid(gate_ref[...]).T

    def gate_row(c, r):
        return gate_t[3 * r + c:3 * r + c + 1, :]

    nc = kcmp_ref.shape[0]
    n_idx = lax.broadcasted_iota(jnp.int32, (nc, 1), 0)
    vis = (n_idx * CMP_STRIDE + (CMP_BLOCK - 1)) <= t1
    psum = jnp.zeros((nc, tq), f32)
    for r in range(R):
        s = jnp.where(vis, _dot(kcmp_ref[...], qa_ref[0:DK, heads[r]]), NEG)
        m = jnp.max(s, axis=0, keepdims=True)
        e = jnp.where(vis, jnp.exp2(s - m), 0.0)
        d = jnp.sum(e, axis=0, keepdims=True)
        p = e * (1.0 / jnp.where(d > 0, d, 1.0))
        out_ref[:, heads[r]] = gate_row(0, r) * _dot(vcmp_ref[...], p.astype(bf16))
        psum = psum + p
    p_hi, p_lo = _split2(psum)
    imp = _dot(ovl_ref[...], p_hi) + _dot(ovl_ref[...], p_lo)

    def reset(m_ref, l_ref, acc_ref):
        m_ref[...] = jnp.full(m_ref.shape, NEG, f32)
        l_ref[...] = jnp.zeros(l_ref.shape, f32)
        acc_ref[...] = jnp.zeros(acc_ref.shape, f32)

    def qk(k_rows, q_rows, mask):
        scores = []
        for r in range(R):
            s = _dot(k_rows, qa_ref[q_rows, heads[r]])
            scores.append(s if mask is None else jnp.where(mask, s, NEG))
        return scores, [jnp.max(s, axis=0, keepdims=True) for s in scores]

    def softmax_pv_head(state, r, s, smax, v_tiles):
        m_ref, l_ref, acc_ref = state
        m_prev = m_ref[:, heads[r]]
        m_new = jnp.maximum(m_prev, smax)
        alpha = jnp.exp2(m_prev - m_new)
        pt = jnp.exp2(s - m_new)
        ptb = pt.astype(bf16)
        pv = _dot(v_tiles[0], ptb[0:tq])
        for i in range(1, len(v_tiles)):
            pv = pv + _dot(v_tiles[i], ptb[i * tq:(i + 1) * tq])
        l_ref[:, heads[r]] = alpha * l_ref[:, heads[r]] + jnp.sum(pt, axis=0, keepdims=True)
        acc_ref[:, heads[r]] = alpha * acc_ref[:, heads[r]] + pv
        m_ref[:, heads[r]] = m_new

    def finish(state, c):
        m_ref, l_ref, acc_ref = state
        for r in range(R):
            out_ref[:, heads[r]] += (gate_row(c, r) * (1.0 / l_ref[:, heads[r]])) * acc_ref[:, heads[r]]

    plain = slice(0, DK)
    aug = slice(0, DK + NSB)

    def wide_tile(state, k_ref, v_ref, first_tile, n_tiles, q_rows, mask):
        k_rows = k_ref[pl.ds(pl.multiple_of(first_tile * tq, tq), n_tiles * tq), :]
        scores, maxes = qk(k_rows, q_rows, mask)
        v_tiles = [v_ref[first_tile + i] for i in range(n_tiles)]
        for r in range(R):
            softmax_pv_head(state, r, scores[r], maxes[r], v_tiles)

    nw = WINDOW // tq + 1
    wt = jnp.maximum(qi - WINDOW // tq, 0)
    kidx = wt * tq + lax.broadcasted_iota(jnp.int32, (nw * tq, 1), 0)
    win = (mw_ref, lw_ref, accw_ref)
    reset(*win)
    wide_tile(win, kw_ref, vw_ref, wt, nw, plain, (kidx <= t1) & (kidx > t1 - WINDOW))
    finish(win, 2)

    j_idx = lax.broadcasted_iota(jnp.int32, (NSB, 1), 0)
    j_f = j_idx.astype(f32)
    cur = t1 // SLC
    valid = (j_idx * SLC) <= t1
    forced = (j_idx == 0) | (j_idx == cur) | (j_idx == cur - 1)
    score = jnp.where(valid, imp + jnp.where(forced, FORCE_BONUS, 0.0), -jnp.inf)
    sel = jnp.zeros((NSB, tq), f32)
    for _ in range(n_sel):
        mx = jnp.max(score, axis=0, keepdims=True)
        first = jnp.min(jnp.where(score == mx, j_f, float(NSB)), axis=0, keepdims=True)
        hit = j_f == first
        sel = jnp.where(hit, 1.0, sel)
        score = jnp.where(hit, -jnp.inf, score)
    nsel = jnp.where(sel > 0.5, 0.0, NEG).astype(bf16)
    for r in range(R):
        qa_ref[DK:DK + NSB, heads[r]] = nsel

    selst = (ms_ref, ls_ref, accs_ref)
    reset(*selst)
    n2 = qi // 2

    def sel_scores(k2, r):
        s = _dot(ksa_ref[pl.ds(pl.multiple_of(k2 * (2 * tq), 2 * tq), 2 * tq), :], qa_ref[aug, heads[r]])
        sc_ref[:, heads[r]] = s
        mx_ref[:, heads[r]] = jnp.max(s, axis=0, keepdims=True)

    for r in range(R):
        sel_scores(0, r)
    kidx2 = n2 * (2 * tq) + lax.broadcasted_iota(jnp.int32, (2 * tq, 1), 0)
    wide_tile(selst, ksa_ref, vs_ref, 2 * n2, 2, aug, kidx2 <= t1)

    def sel_body(k2, carry):
        nxt = jnp.minimum(k2 + 1, jnp.maximum(n2 - 1, 0))
        for r in range(R):
            softmax_pv_head(selst, r, sc_ref[:, heads[r]], mx_ref[:, heads[r]],
                            [vs_ref[2 * k2], vs_ref[2 * k2 + 1]])
            sel_scores(nxt, r)
        return carry

    lax.fori_loop(0, n2, sel_body, 0)
    finish(selst, 1)

    for r in range(R):
        o_ref[:, r * DK:(r + 1) * DK] = out_ref[:, heads[r]].T


def _nsa(proj, small, cosf, sinf, ovl_t, kcmp, vcmp_t, ksa, vs_t, kw, vw_t, B, T, tq):
    nq = T // tq
    nc = kcmp.shape[2]
    rows = R * tq
    n_sel = min(N_SELECT, T // SLC)
    assert WINDOW % tq == 0 and nq % 2 == 0 and nq > WINDOW // tq

    def per_bg(*shape):
        return pl.BlockSpec((None, None) + shape, lambda b, g, i: (b, g) + (0,) * len(shape))

    tab = pl.BlockSpec((tq, DK), lambda b, g, i: (b * nq + i, 0))
    return pl.pallas_call(
        functools.partial(_nsa_kernel, tq=tq, n_sel=n_sel),
        grid=(B, G, nq),
        in_specs=[
            pl.BlockSpec((tq, R * DK), lambda b, g, i: (b * nq + i, g)),
            tab, tab,
            pl.BlockSpec((tq, 128), lambda b, g, i: (b * nq + i, g)),
            pl.BlockSpec((NSB, nc), lambda b, g, i: (0, 0)),
            per_bg(nc, DK), per_bg(DK, nc),
            per_bg(T, DK + NSB), per_bg(nq, DK, tq), per_bg(T, DK), per_bg(nq, DK, tq),
        ],
        out_specs=pl.BlockSpec((tq, R * DK), lambda b, g, i: (b * nq + i, g)),
        out_shape=jax.ShapeDtypeStruct((B * T, D), f32),
        scratch_shapes=[pltpu.VMEM((DK + NSB, rows), bf16)]
        + [pltpu.VMEM((1, rows), f32), pltpu.VMEM((1, rows), f32), pltpu.VMEM((DK, rows), f32)] * 2
        + [pltpu.VMEM((DK, rows), f32), pltpu.VMEM((2 * tq, rows), f32), pltpu.VMEM((1, rows), f32)],
        compiler_params=_cparams(("parallel", "parallel", "arbitrary")),
        name="nsa_attention",
    )(proj, cosf, sinf, small, ovl_t, kcmp, vcmp_t, ksa, vs_t, kw, vw_t)


def _ssd_kernel(z_ref, xs_ref, bm_ref, cm_ref, dt_ref, xsh_ref, bmh_ref, cmh_ref,
                wx_ref, bx_ref, wb_ref, bb_ref, wc_ref, bc_ref,
                dtb_ref, alog_ref, dskip_ref, ng_ref, eh_ref,
                o_ref, st_ref, ext_ref):
    c = pl.program_id(1)
    L = SSD_L
    HP = 2 * SSD_P
    GW = D // SSD_G

    @pl.when(c == 0)
    def _():
        st_ref[...] = jnp.zeros_like(st_ref)

    def conv_silu(u_ref, h_ref, w_ref, b_ref):
        width = u_ref.shape[1]
        ext_ref[0:8, 0:width] = jnp.where(c == 0, 0.0, h_ref[...])
        ext_ref[8:8 + L, 0:width] = u_ref[...]
        y = b_ref[...]
        for j in range(SSD_CONV):
            lo = 8 - (SSD_CONV - 1) + j
            y = y + w_ref[j:j + 1, :] * ext_ref[lo:lo + L, 0:width]
        return y * _sigmoid(y)

    xs = conv_silu(xs_ref, xsh_ref, wx_ref, bx_ref)
    bm = conv_silu(bm_ref, bmh_ref, wb_ref, bb_ref)
    cm = conv_silu(cm_ref, cmh_ref, wc_ref, bc_ref)

    lane = lax.broadcasted_iota(jnp.int32, (1, 128), 1)
    xdt_in = dt_ref[...] + dtb_ref[...]
    dt = jnp.maximum(xdt_in, 0.0) + jnp.log1p(jnp.exp(-jnp.abs(xdt_in)))
    dt = jnp.where(lane < SSD_HEADS, dt, 0.0)
    da = dt * (-jnp.exp(alog_ref[...]))

    row = lax.broadcasted_iota(jnp.int32, (L, L), 0)
    colm = lax.broadcasted_iota(jnp.int32, (L, L), 1)
    causal = colm <= row
    tril = jnp.where(causal, 1.0, 0.0).astype(bf16)
    d_hi, d_mid, d_lo = _split3(da)
    acs = _dot(tril, d_hi) + _dot(tril, d_mid) + _dot(tril, d_lo)
    acs_t = acs.T
    last = acs[L - 1:L, :]
    eacs = jnp.exp(acs)
    dec = jnp.exp(last - acs)
    cdec = jnp.exp(jnp.broadcast_to(last, (8, 128)))

    eh = eh_ref[...]
    dt_x = _dot(dt.astype(bf16), eh)
    eacs_x = _dot(eacs.astype(bf16), eh)
    dec_x = _dot(dec.astype(bf16), eh)
    c_hi, c_lo = _split2(cdec)
    cdec_x = (_dot(c_hi, eh) + _dot(c_lo, eh))[0:1, :]

    xdt = xs * dt_x
    xdt_b = xdt.astype(bf16)
    xdec_b = (xdt * dec_x).astype(bf16)
    lane_hp = lax.broadcasted_iota(jnp.int32, (L, HP), 1)

    for g in range(SSD_G):
        cm_g = cm[:, g * SSD_N:(g + 1) * SSD_N].astype(bf16)
        bm_g = bm[:, g * SSD_N:(g + 1) * SSD_N]
        cb = _dot_nt(cm_g, bm_g.astype(bf16))
        s_in = st_ref[g]
        y_off = _dot(cm_g, s_in.astype(bf16))
        s_new = _dot(bm_g.T.astype(bf16), xdec_b[:, g * GW:(g + 1) * GW])
        st_ref[g] = s_in * cdec_x[:, g * GW:(g + 1) * GW] + s_new
        for pp in range(GW // HP):
            h0 = g * (GW // SSD_P) + 2 * pp
            c0 = g * GW + pp * HP
            ms = []
            for h in (h0, h0 + 1):
                diff = acs[:, h:h + 1] - acs_t[h:h + 1, :]
                ms.append(jnp.where(causal, cb * jnp.exp(jnp.where(causal, diff, NEG)), 0.0))
            lhs = jnp.concatenate(ms, axis=1).astype(bf16)
            xp = xdt_b[:, c0:c0 + HP]
            zero = jnp.zeros_like(xp)
            rhs = jnp.concatenate([jnp.where(lane_hp < SSD_P, xp, zero),
                                   jnp.where(lane_hp >= SSD_P, xp, zero)], axis=0)
            y = (_dot(lhs, rhs) + eacs_x[:, c0:c0 + HP] * y_off[:, pp * HP:(pp + 1) * HP]
                 + dskip_ref[:, c0:c0 + HP] * xs[:, c0:c0 + HP])
            o_ref[:, c0:c0 + HP] = y

    z = z_ref[...]
    y = o_ref[...] * (z * _sigmoid(z))
    for g in range(SSD_G):
        yg = y[:, g * GW:(g + 1) * GW]
        ms = jnp.mean(yg * yg, axis=-1, keepdims=True)
        o_ref[:, g * GW:(g + 1) * GW] = yg * lax.rsqrt(ms + EPS) * ng_ref[:, g * GW:(g + 1) * GW]


def _ssd(proj, small, wts, B, T):
    nch = T // SSD_L
    L = SSD_L

    def main(name, w):
        base = _blk(name, w)
        return pl.BlockSpec((L, w), lambda b, c: (b * nch + c, base))

    def halo(name, w):
        base = _blk(name, w)
        return pl.BlockSpec((8, w), lambda b, c: (jnp.maximum((b * nch + c) * (L // 8) - 1, 0), base))

    def full(a):
        return pl.BlockSpec(a.shape, lambda b, c: (0,) * a.ndim)

    gn = SSD_G * SSD_N
    return pl.pallas_call(
        _ssd_kernel,
        grid=(B, nch),
        in_specs=[main('s_z', D), main('xs', D), main('bm', gn), main('cm', gn),
                  pl.BlockSpec((L, 128), lambda b, c: (b * nch + c, G)),
                  halo('xs', D), halo('bm', gn), halo('cm', gn)] + [full(a) for a in wts],
        out_specs=pl.BlockSpec((L, D), lambda b, c: (b * nch + c, 0)),
        out_shape=jax.ShapeDtypeStruct((B * T, D), f32),
        scratch_shapes=[pltpu.VMEM((SSD_G, SSD_N, D // SSD_G), f32),
                        pltpu.VMEM((8 + L, D), f32)],
        compiler_params=_cparams(("parallel", "arbitrary")),
        name="ssd",
    )(proj, proj, proj, proj, small, proj, proj, proj, *wts)


def _merge_kernel(ya_ref, cb_ref, cc_ref, cu_ref, cch_ref, cuh_ref, yc_ref,
                  g0_ref, g1_ref, g2_ref, x_ref, cw_ref, wo_ref, o_ref, ext_ref, *, tm, tiles_per_seq):
    first = (pl.program_id(0) % tiles_per_seq) == 0
    ext_ref[0:8, :] = jnp.where(first, 0.0, cch_ref[...] * cuh_ref[...])
    ext_ref[8:8 + tm, :] = cc_ref[...] * cu_ref[...]
    conv = cw_ref[0:1, :] * ext_ref[6:6 + tm, :]
    for j in range(1, CONV_W):
        conv = conv + cw_ref[j:j + 1, :] * ext_ref[6 + j:6 + j + tm, :]
    yb = cb_ref[...] * conv
    merged = (_sigmoid(g0_ref[...]) * ya_ref[...] + _sigmoid(g1_ref[...]) * yb
              + _sigmoid(g2_ref[...]) * yc_ref[...])
    o_ref[...] = x_ref[...] + _dot(merged.astype(bf16), wo_ref[...])


def _merge(ya, yc, proj, x2, cw, wo, T, tm):
    M = x2.shape[0]

    def col(name):
        base = _blk(name, D)
        return pl.BlockSpec((tm, D), lambda i: (i, base))

    def halo(name):
        base = _blk(name, D)
        return pl.BlockSpec((8, D), lambda i: (jnp.maximum(i * (tm // 8) - 1, 0), base))

    row = pl.BlockSpec((tm, D), lambda i: (i, 0))
    return pl.pallas_call(
        functools.partial(_merge_kernel, tm=tm, tiles_per_seq=T // tm),
        grid=(M // tm,),
        in_specs=[row, col('cb'), col('cc'), col('cu'), halo('cc'), halo('cu'), row,
                  col('gm0'), col('gm1'), col('gm2'), row,
                  pl.BlockSpec((CONV_W, D), lambda i: (0, 0)),
                  pl.BlockSpec((D, D), lambda i: (0, 0))],
        out_specs=row,
        out_shape=jax.ShapeDtypeStruct((M, D), f32),
        scratch_shapes=[pltpu.VMEM((8 + tm, D), f32)],
        compiler_params=_cparams(("parallel",)),
        name="merge_oproj",
    )(ya, proj, proj, proj, proj, proj, yc, proj, proj, proj, x2, cw, wo)


def _ffn_kernel(x_ref, g_ref, wu_ref, wd_ref, o_ref, h_ref):
    @pl.when(pl.program_id(1) == 0)
    def _():
        x = x_ref[...]
        h_ref[...] = _rms(x, g_ref[...]).astype(bf16)
        o_ref[...] = x

    u = jnp.maximum(_dot(h_ref[...], wu_ref[...]), 0.0)
    o_ref[...] += _dot((u * u).astype(bf16), wd_ref[...])


def _ffn(x2, g, wu, wd, tm, tf):
    M = x2.shape[0]
    return pl.pallas_call(
        _ffn_kernel,
        grid=(M // tm, D_FF // tf),
        in_specs=[pl.BlockSpec((tm, D), lambda i, j: (i, 0)),
                  pl.BlockSpec((1, D), lambda i, j: (0, 0)),
                  pl.BlockSpec((D, tf), lambda i, j: (0, j)),
                  pl.BlockSpec((tf, D), lambda i, j: (j, 0))],
        out_specs=pl.BlockSpec((tm, D), lambda i, j: (i, 0)),
        out_shape=jax.ShapeDtypeStruct((M, D), f32),
        scratch_shapes=[pltpu.VMEM((tm, D), bf16)],
        compiler_params=_cparams(("parallel", "arbitrary")),
        name="ffn",
    )(x2, g, wu, wd)


def _ple_kernel(x_ref, p_ref, g_ref, wp_ref, wg_ref, gf_ref, o_ref, *, final):
    x = x_ref[...]
    gate = _sigmoid(_dot(_rms(x, g_ref[...]).astype(bf16), wg_ref[...]))
    y = x + _dot(p_ref[...].astype(bf16), wp_ref[...]) * gate
    if final:
        y = _rms(y, gf_ref[...])
    o_ref[...] = y


def _ple(x2, p2, g, wp, wg, gf, tm, final):
    M = x2.shape[0]
    row = pl.BlockSpec((tm, D), lambda i: (i, 0))
    vec = pl.BlockSpec((1, D), lambda i: (0, 0))
    return pl.pallas_call(
        functools.partial(_ple_kernel, final=final),
        grid=(M // tm,),
        in_specs=[row, pl.BlockSpec((tm, PLE), lambda i: (i, 0)), vec,
                  pl.BlockSpec((PLE, D), lambda i: (0, 0)),
                  pl.BlockSpec((D, D), lambda i: (0, 0)), vec],
        out_specs=row,
        out_shape=jax.ShapeDtypeStruct((M, D), f32),
        compiler_params=_cparams(("parallel",)),
        name="ple",
    )(x2, p2, g, wp, wg, gf)


def _prep_w_in(w):
    main = jnp.concatenate([w[:, _SRC[n][0]:_SRC[n][0] + _SRC[n][1]] for n in _ORDER], axis=1)
    a, wd = _SRC['g_nsa']
    gn = w[:, a:a + wd].reshape(D, G, R * 3)
    gn = jnp.pad(gn, ((0, 0), (0, 0), (0, 128 - R * 3))).reshape(D, G * 128)
    a, wd = _SRC['s_dt']
    dt = jnp.pad(w[:, a:a + wd], ((0, 0), (0, 128 - wd)))
    return main.astype(bf16), jnp.concatenate([gn, dt], axis=1).astype(bf16)


def _overlap_matrix(nc):
    i = np.arange(nc)[:, None]
    j = np.arange(NSB)[None, :]
    ovl = (i * CMP_STRIDE < j * SLC + SLC) & (i * CMP_STRIDE + CMP_BLOCK > j * SLC)
    return jnp.asarray(ovl.T.astype(np.float32), dtype=bf16)


def _head_expand():
    h = np.arange(128)[:, None]
    ch = np.arange(D)[None, :]
    return jnp.asarray((ch // SSD_P == h).astype(np.float32), dtype=bf16)


def kernel(x, p, positions, g_mix, w_in, nsa_pe_k, nsa_pe_v, phi_k_w1, phi_k_b1, phi_k_w2, phi_k_b2,
           phi_v_w1, phi_v_b1, phi_v_w2, phi_v_b2, sconv_w, ssd_conv_w, ssd_conv_b, ssd_dt_bias,
           ssd_a_log, ssd_d, ssd_norm_g, w_o, g_mlp, w_up, w_down, g_ple, w_ple, w_ple_gate, g_final):
    B, T, _ = x.shape
    depth = w_in.shape[0]
    M = B * T
    assert T % 256 == 0 and T // SLC <= NSB
    tq = 256
    tm_proj = 1024 if M % 1024 == 0 else 256
    tm = 512 if T % 512 == 0 else 256
    nc = T // CMP_STRIDE

    inv_freq = 1.0 / (10000.0 ** (jnp.arange(0, DK, 2, dtype=f32) / DK))
    ang = positions.astype(f32)[..., None] * inv_freq
    cosf = jnp.concatenate([jnp.cos(ang), jnp.cos(ang)], axis=-1).reshape(M, DK)
    sinf = jnp.concatenate([-jnp.sin(ang), jnp.sin(ang)], axis=-1).reshape(M, DK)
    ovl = _overlap_matrix(nc)
    eh = _head_expand()

    def vec(a, n=None):
        a = a.reshape(1, -1).astype(f32)
        return a if n is None else jnp.pad(a, ((0, 0), (0, n - a.shape[1])))

    x2 = x.reshape(M, D)
    for i in range(depth):
        w_main, w_small = _prep_w_in(w_in[i])
        proj, small = _proj(x2, vec(g_mix[i]), w_main, w_small, tm_proj, 1024)

        kc, vc, ksa, vs, kw, vw = _prep(proj, cosf, sinf, B, T, tq)

        def phi(pe, w1, b1, w2, b2):
            pe8 = jnp.broadcast_to(pe.reshape(1, -1), (8, CMP_BLOCK * DK)).astype(bf16)
            return (pe8, w1.astype(bf16), vec(b1), w2.astype(bf16), vec(b2))

        kcmp, vcmp = _compress(kc.reshape(B, G, nc, CMP_STRIDE * DK), vc.reshape(B, G, nc, CMP_STRIDE * DK),
                               phi(nsa_pe_k[i], phi_k_w1[i], phi_k_b1[i], phi_k_w2[i], phi_k_b2[i]),
                               phi(nsa_pe_v[i], phi_v_w1[i], phi_v_b1[i], phi_v_w2[i], phi_v_b2[i]))
        ya = _nsa(proj, small, cosf, sinf, ovl, kcmp, vcmp, ksa, vs, kw, vw, B, T, tq)

        cw, cbias = ssd_conv_w[i], ssd_conv_b[i]
        gn = SSD_G * SSD_N
        ssd_w = (cw[:, :D], vec(cbias[:D]), cw[:, D:D + gn], vec(cbias[D:D + gn]),
                 cw[:, D + gn:], vec(cbias[D + gn:]),
                 vec(ssd_dt_bias[i], 128), vec(ssd_a_log[i], 128),
                 vec(jnp.repeat(ssd_d[i], SSD_P)), vec(ssd_norm_g[i]), eh)
        yc = _ssd(proj, small, ssd_w, B, T)

        x2 = _merge(ya, yc, proj, x2, sconv_w[i], w_o[i].astype(bf16), T, 256)
        x2 = _ffn(x2, vec(g_mlp[i]), w_up[i].astype(bf16), w_down[i].astype(bf16), tm, 1024)
        x2 = _ple(x2, p[i].reshape(M, PLE), vec(g_ple[i]), w_ple[i].astype(bf16),
                  w_ple_gate[i].astype(bf16), vec(g_final), tm, final=(i == depth - 1))
    return x2.reshape(B, T, D)
```

```python
import functools

import numpy as np
import jax
import jax.numpy as jnp
from jax import lax
from jax.experimental import pallas as pl
from jax.experimental.pallas import tpu as pltpu

f32 = jnp.float32
bf16 = jnp.bfloat16

D = 2048
N_HEADS = 16
DK = 128
G = 4
R = N_HEADS // G
CMP_BLOCK = 32
CMP_STRIDE = 16
SLC = 64
N_SELECT = 16
WINDOW = 512
FORCE_BONUS = 1.0e4
CONV_W = 3
SSD_HEADS = 32
SSD_P = 64
SSD_G = 4
SSD_N = 128
SSD_CONV = 4
SSD_L = 128
D_FF = 4 * D
PLE = 256
EPS = 1e-6
NSB = 128
HALO = 16
NEG = -1e30
LOG2E = 1.4426950408889634

VMEM_LIMIT = 56 * 1024 * 1024

_SRC = {
    'q': (0, 2048), 'k_c': (2048, 512), 'v_c': (2560, 512), 'k_s': (3072, 512), 'v_s': (3584, 512),
    'k_w': (4096, 512), 'v_w': (4608, 512), 'g_nsa': (5120, 48),
    'cb': (5168, 2048), 'cc': (7216, 2048), 'cu': (9264, 2048),
    's_z': (11312, 2048), 'xs': (13360, 2048), 'bm': (15408, 512), 'cm': (15920, 512),
    's_dt': (16432, 32), 'gm0': (16464, 2048), 'gm1': (18512, 2048), 'gm2': (20560, 2048),
}
_ORDER = ['q', 'cb', 'cc', 'cu', 's_z', 'xs', 'gm0', 'gm1', 'gm2',
          'k_c', 'v_c', 'k_s', 'v_s', 'k_w', 'v_w', 'bm', 'cm']
_OFF = {}
_o = 0
for _n in _ORDER:
    _OFF[_n] = _o
    _o += _SRC[_n][1]
N_MAIN = _o
N_SMALL = G * 128 + 128


def _blk(name, width):
    off = _OFF[name]
    assert off % width == 0
    return off // width


def _cparams(sem):
    return pltpu.CompilerParams(dimension_semantics=sem, vmem_limit_bytes=VMEM_LIMIT)


def _dot(a, b):
    return jnp.dot(a, b, preferred_element_type=f32)


def _dot_nt(a, b):
    return lax.dot_general(a, b, (((1,), (1,)), ((), ())), preferred_element_type=f32)


def _sigmoid(x):
    return 1.0 / (1.0 + jnp.exp(-x))


def _split2(x):
    hi = x.astype(bf16)
    lo = (x - hi.astype(f32)).astype(bf16)
    return hi, lo


def _split3(x):
    hi = x.astype(bf16)
    r1 = x - hi.astype(f32)
    mid = r1.astype(bf16)
    lo = (r1 - mid.astype(f32)).astype(bf16)
    return hi, mid, lo


def _rms(x, g):
    ms = jnp.mean(x * x, axis=-1, keepdims=True)
    return x * lax.rsqrt(ms + EPS) * g


def _proj_kernel(x_ref, g_ref, w_ref, ws_ref, o_ref, os_ref, h_ref):
    @pl.when(pl.program_id(1) == 0)
    def _():
        h = _rms(x_ref[...], g_ref[...]).astype(bf16)
        h_ref[...] = h
        os_ref[...] = _dot(h, ws_ref[...])

    o_ref[...] = _dot(h_ref[...], w_ref[...]).astype(bf16)


def _proj(x2, g, w_main, w_small, tm, tn):
    M = x2.shape[0]
    return pl.pallas_call(
        _proj_kernel,
        grid=(M // tm, N_MAIN // tn),
        in_specs=[
            pl.BlockSpec((tm, D), lambda i, j: (i, 0), pipeline_mode=pl.Buffered(1)),
            pl.BlockSpec((1, D), lambda i, j: (0, 0)),
            pl.BlockSpec((D, tn), lambda i, j: (0, j)),
            pl.BlockSpec((D, N_SMALL), lambda i, j: (0, 0), pipeline_mode=pl.Buffered(1)),
        ],
        out_specs=[
            pl.BlockSpec((tm, tn), lambda i, j: (i, j)),
            pl.BlockSpec((tm, N_SMALL), lambda i, j: (i, 0)),
        ],
        out_shape=[jax.ShapeDtypeStruct((M, N_MAIN), bf16),
                   jax.ShapeDtypeStruct((M, N_SMALL), f32)],
        scratch_shapes=[pltpu.VMEM((tm, D), bf16)],
        compiler_params=_cparams(("parallel", "arbitrary")),
        name="proj",
    )(x2, g, w_main, w_small)


def _rope(x, cosf, sinf):
    x = x.astype(f32)
    return x * cosf + pltpu.roll(x, DK // 2, axis=1) * sinf


def _prep_kernel(kc_ref, vc_ref, ks_ref, vs_ref, kw_ref, vw_ref, cos_ref, sin_ref,
                 kco_ref, vco_ref, ksa_ref, vso_ref, kwo_ref, vwo_ref, *, tt):
    cosf = cos_ref[...]
    sinf = sin_ref[...]
    kco_ref[...] = _rope(kc_ref[...], cosf, sinf).astype(bf16)
    vco_ref[...] = vc_ref[...].astype(bf16)
    ksa_ref[:, 0:DK] = _rope(ks_ref[...], cosf, sinf).astype(bf16)
    t = pl.program_id(2) * tt + lax.broadcasted_iota(jnp.int32, (tt, NSB), 0)
    j = lax.broadcasted_iota(jnp.int32, (tt, NSB), 1)
    ksa_ref[:, DK:DK + NSB] = jnp.where(t // SLC == j, 1.0, 0.0).astype(bf16)
    vso_ref[...] = vs_ref[...].astype(f32).T.astype(bf16)
    kwo_ref[...] = _rope(kw_ref[...], cosf, sinf).astype(bf16)
    vwo_ref[...] = vw_ref[...].astype(f32).T.astype(bf16)


def _prep(proj, cosf, sinf, B, T, tt):
    nt = T // tt

    def col(name):
        base = _blk(name, DK)
        return pl.BlockSpec((tt, DK), lambda b, g, i: (b * nt + i, base + g))

    tab = pl.BlockSpec((tt, DK), lambda b, g, i: (b * nt + i, 0))

    def out(w):
        return pl.BlockSpec((None, None, tt, w), lambda b, g, i: (b, g, i, 0))

    out_t = pl.BlockSpec((None, None, None, DK, tt), lambda b, g, i: (b, g, i, 0, 0))
    return pl.pallas_call(
        functools.partial(_prep_kernel, tt=tt),
        grid=(B, G, nt),
        in_specs=[col('k_c'), col('v_c'), col('k_s'), col('v_s'), col('k_w'), col('v_w'), tab, tab],
        out_specs=[out(DK), out(DK), out(DK + NSB), out_t, out(DK), out_t],
        out_shape=[jax.ShapeDtypeStruct((B, G, T, DK), bf16),
                   jax.ShapeDtypeStruct((B, G, T, DK), bf16),
                   jax.ShapeDtypeStruct((B, G, T, DK + NSB), bf16),
                   jax.ShapeDtypeStruct((B, G, nt, DK, tt), bf16),
                   jax.ShapeDtypeStruct((B, G, T, DK), bf16),
                   jax.ShapeDtypeStruct((B, G, nt, DK, tt), bf16)],
        compiler_params=_cparams(("parallel", "parallel", "parallel")),
        name="kv_prep",
    )(proj, proj, proj, proj, proj, proj, cosf, sinf)


def _compress_one(x_ref, pe_ref, w1_ref, b1_ref, w2_ref, b2_ref, o_ref, transpose):
    nc = x_ref.shape[0]
    half = CMP_STRIDE * DK
    x = x_ref[...]
    first = _dot(x, w1_ref[0:half, :])
    second = _dot(x, w1_ref[half:2 * half, :])
    second = pltpu.roll(second, nc - 1, axis=0)
    pe = _dot(pe_ref[...], w1_ref[...])[0:1, :]
    h = first + second + pe + b1_ref[...]
    h = h * _sigmoid(h)
    y = _dot(h.astype(bf16), w2_ref[...]) + b2_ref[...]
    o_ref[...] = (y.T if transpose else y).astype(bf16)


def _compress_kernel(k_ref, v_ref, pek, w1k, b1k, w2k, b2k, pev, w1v, b1v, w2v, b2v, ko_ref, vo_ref):
    _compress_one(k_ref, pek, w1k, b1k, w2k, b2k, ko_ref, False)
    _compress_one(v_ref, pev, w1v, b1v, w2v, b2v, vo_ref, True)


def _compress(kc2, vc2, wk, wv):
    B, _, nc, width = kc2.shape
    xin = pl.BlockSpec((None, None, nc, width), lambda b, g: (b, g, 0, 0))

    def full(a):
        return pl.BlockSpec(a.shape, lambda b, g: (0,) * a.ndim)

    out = pl.BlockSpec((None, None, nc, DK), lambda b, g: (b, g, 0, 0))
    out_t = pl.BlockSpec((None, None, DK, nc), lambda b, g: (b, g, 0, 0))
    return pl.pallas_call(
        _compress_kernel,
        grid=(B, G),
        in_specs=[xin, xin] + [full(a) for a in wk] + [full(a) for a in wv],
        out_specs=[out, out_t],
        out_shape=[jax.ShapeDtypeStruct((B, G, nc, DK), bf16),
                   jax.ShapeDtypeStruct((B, G, DK, nc), bf16)],
        compiler_params=_cparams(("parallel", "parallel")),
        name="compress",
    )(kc2, vc2, *wk, *wv)


def _nsa_kernel(q_ref, cos_ref, sin_ref, gate_ref, ovl_ref, kcmp_ref, vcmp_ref,
                ksa_ref, vs_ref, kw_ref, vw_ref, o_ref,
                qa_ref, ms_ref, ls_ref, accs_ref, mw_ref, lw_ref, accw_ref, out_ref, sc_ref, mx_ref,
                *, tq, n_sel):
    qi = pl.program_id(2)
    q0 = qi * tq
    heads = [slice(r * tq, (r + 1) * tq) for r in range(R)]
    qscale = DK ** -0.5 * LOG2E
    cosf = cos_ref[...]
    sinf = sin_ref[...]
    for r in range(R):
        qr = _rope(q_ref[:, r * DK:(r + 1) * DK], cosf, sinf) * qscale
        qa_ref[0:DK, heads[r]] = qr.T.astype(bf16)
    t1 = q0 + lax.broadcasted_iota(jnp.int32, (1, tq), 1)
    gate_t = _sigmoid(gate_ref[...]).T

    def gate_row(c, r):
        return gate_t[3 * r + c:3 * r + c + 1, :]

    nc = kcmp_ref.shape[0]
    n_idx = lax.broadcasted_iota(jnp.int32, (nc, 1), 0)
    vis = (n_idx * CMP_STRIDE + (CMP_BLOCK - 1)) <= t1
    psum = jnp.zeros((nc, tq), f32)
    for r in range(R):
        s = jnp.where(vis, _dot(kcmp_ref[...], qa_ref[0:DK, heads[r]]), NEG)
        m = jnp.max(s, axis=0, keepdims=True)
        e = jnp.where(vis, jnp.exp2(s - m), 0.0)
        d = jnp.sum(e, axis=0, keepdims=True)
        p = e * (1.0 / jnp.where(d > 0, d, 1.0))
        out_ref[:, heads[r]] = gate_row(0, r) * _dot(vcmp_ref[...], p.astype(bf16))
        psum = psum + p
    p_hi, p_lo = _split2(psum)
    imp = _dot(ovl_ref[...], p_hi) + _dot(ovl_ref[...], p_lo)

    def reset(m_ref, l_ref, acc_ref):
        m_ref[...] = jnp.full(m_ref.shape, NEG, f32)
        l_ref[...] = jnp.zeros(l_ref.shape, f32)
        acc_ref[...] = jnp.zeros(acc_ref.shape, f32)

    def qk(k_rows, q_rows, mask):
        scores = []
        for r in range(R):
            s = _dot(k_rows, qa_ref[q_rows, heads[r]])
            scores.append(s if mask is None else jnp.where(mask, s, NEG))
        return scores, [jnp.max(s, axis=0, keepdims=True) for s in scores]

    def softmax_pv_head(state, r, s, smax, v_tiles):
        m_ref, l_ref, acc_ref = state
        m_prev = m_ref[:, heads[r]]
        m_new = jnp.maximum(m_prev, smax)
        alpha = jnp.exp2(m_prev - m_new)
        pt = jnp.exp2(s - m_new)
        ptb = pt.astype(bf16)
        pv = _dot(v_tiles[0], ptb[0:tq])
        for i in range(1, len(v_tiles)):
            pv = pv + _dot(v_tiles[i], ptb[i * tq:(i + 1) * tq])
        l_ref[:, heads[r]] = alpha * l_ref[:, heads[r]] + jnp.sum(pt, axis=0, keepdims=True)
        acc_ref[:, heads[r]] = alpha * acc_ref[:, heads[r]] + pv
        m_ref[:, heads[r]] = m_new

    def finish(state, c):
        m_ref, l_ref, acc_ref = state
        for r in range(R):
            out_ref[:, heads[r]] += (gate_row(c, r) * (1.0 / l_ref[:, heads[r]])) * acc_ref[:, heads[r]]

    plain = slice(0, DK)
    aug = slice(0, DK + NSB)

    def wide_tile(state, k_ref, v_ref, first_tile, n_tiles, q_rows, mask):
        k_rows = k_ref[pl.ds(pl.multiple_of(first_tile * tq, tq), n_tiles * tq), :]
        scores, maxes = qk(k_rows, q_rows, mask)
        v_tiles = [v_ref[first_tile + i] for i in range(n_tiles)]
        for r in range(R):
            softmax_pv_head(state, r, scores[r], maxes[r], v_tiles)

    nw = WINDOW // tq + 1
    wt = jnp.maximum(qi - WINDOW // tq, 0)
    kidx = wt * tq + lax.broadcasted_iota(jnp.int32, (nw * tq, 1), 0)
    win = (mw_ref, lw_ref, accw_ref)
    reset(*win)
    wide_tile(win, kw_ref, vw_ref, wt, nw, plain, (kidx <= t1) & (kidx > t1 - WINDOW))
    finish(win, 2)

    j_idx = lax.broadcasted_iota(jnp.int32, (NSB, 1), 0)
    j_f = j_idx.astype(f32)
    cur = t1 // SLC
    valid = (j_idx * SLC) <= t1
    forced = (j_idx == 0) | (j_idx == cur) | (j_idx == cur - 1)
    score = jnp.where(valid, imp + jnp.where(forced, FORCE_BONUS, 0.0), -jnp.inf)
    sel = jnp.zeros((NSB, tq), f32)
    for _ in range(n_sel):
        mx = jnp.max(score, axis=0, keepdims=True)
        first = jnp.min(jnp.where(score == mx, j_f, float(NSB)), axis=0, keepdims=True)
        hit = j_f == first
        sel = jnp.where(hit, 1.0, sel)
        score = jnp.where(hit, -jnp.inf, score)
    nsel = jnp.where(sel > 0.5, 0.0, NEG).astype(bf16)
    for r in range(R):
        qa_ref[DK:DK + NSB, heads[r]] = nsel

    selst = (ms_ref, ls_ref, accs_ref)
    reset(*selst)
    n2 = qi // 2

    def sel_scores(k2, r):
        s = _dot(ksa_ref[pl.ds(pl.multiple_of(k2 * (2 * tq), 2 * tq), 2 * tq), :], qa_ref[aug, heads[r]])
        sc_ref[:, heads[r]] = s
        mx_ref[:, heads[r]] = jnp.max(s, axis=0, keepdims=True)

    for r in range(R):
        sel_scores(0, r)
    kidx2 = n2 * (2 * tq) + lax.broadcasted_iota(jnp.int32, (2 * tq, 1), 0)
    wide_tile(selst, ksa_ref, vs_ref, 2 * n2, 2, aug, kidx2 <= t1)

    def sel_body(k2, carry):
        nxt = jnp.minimum(k2 + 1, jnp.maximum(n2 - 1, 0))
        for r in range(R):
            softmax_pv_head(selst, r, sc_ref[:, heads[r]], mx_ref[:, heads[r]],
                            [vs_ref[2 * k2], vs_ref[2 * k2 + 1]])
            sel_scores(nxt, r)
        return carry

    lax.fori_loop(0, n2, sel_body, 0)
    finish(selst, 1)

    for r in range(R):
        o_ref[:, r * DK:(r + 1) * DK] = out_ref[:, heads[r]].T.astype(bf16)


def _nsa(proj, small, cosf, sinf, ovl_t, kcmp, vcmp_t, ksa, vs_t, kw, vw_t, B, T, tq):
    nq = T // tq
    nc = kcmp.shape[2]
    rows = R * tq
    n_sel = min(N_SELECT, T // SLC)
    assert WINDOW % tq == 0 and nq % 2 == 0 and nq > WINDOW // tq

    def per_bg(*shape):
        return pl.BlockSpec((None, None) + shape, lambda b, g, i: (b, g) + (0,) * len(shape))

    tab = pl.BlockSpec((tq, DK), lambda b, g, i: (b * nq + i, 0))
    return pl.pallas_call(
        functools.partial(_nsa_kernel, tq=tq, n_sel=n_sel),
        grid=(B, G, nq),
        in_specs=[
            pl.BlockSpec((tq, R * DK), lambda b, g, i: (b * nq + i, g)),
            tab, tab,
            pl.BlockSpec((tq, 128), lambda b, g, i: (b * nq + i, g)),
            pl.BlockSpec((NSB, nc), lambda b, g, i: (0, 0)),
            per_bg(nc, DK), per_bg(DK, nc),
            per_bg(T, DK + NSB), per_bg(nq, DK, tq), per_bg(T, DK), per_bg(nq, DK, tq),
        ],
        out_specs=pl.BlockSpec((tq, R * DK), lambda b, g, i: (b * nq + i, g)),
        out_shape=jax.ShapeDtypeStruct((B * T, D), bf16),
        scratch_shapes=[pltpu.VMEM((DK + NSB, rows), bf16)]
        + [pltpu.VMEM((1, rows), f32), pltpu.VMEM((1, rows), f32), pltpu.VMEM((DK, rows), f32)] * 2
        + [pltpu.VMEM((DK, rows), f32), pltpu.VMEM((2 * tq, rows), f32), pltpu.VMEM((1, rows), f32)],
        compiler_params=_cparams(("parallel", "parallel", "arbitrary")),
        name="nsa_attention",
    )(proj, cosf, sinf, small, ovl_t, kcmp, vcmp_t, ksa, vs_t, kw, vw_t)


def _ssd_kernel(z_ref, xs_ref, bm_ref, cm_ref, dt_ref, xsh_ref, bmh_ref, cmh_ref,
                wx_ref, bx_ref, wb_ref, bb_ref, wc_ref, bc_ref,
                dtb_ref, alog_ref, dskip_ref, ng_ref, eh_ref,
                o_ref, st_ref, ext_ref, y_ref):
    c = pl.program_id(1)
    L = SSD_L
    HP = 2 * SSD_P
    GW = D // SSD_G

    @pl.when(c == 0)
    def _():
        st_ref[...] = jnp.zeros_like(st_ref)

    def conv_silu(u_ref, h_ref, w_ref, b_ref):
        width = u_ref.shape[1]
        ext_ref[0:HALO, 0:width] = jnp.where(c == 0, 0.0, h_ref[...].astype(f32))
        ext_ref[HALO:HALO + L, 0:width] = u_ref[...].astype(f32)
        y = b_ref[...]
        for j in range(SSD_CONV):
            lo = HALO - (SSD_CONV - 1) + j
            y = y + w_ref[j:j + 1, :] * ext_ref[lo:lo + L, 0:width]
        return y * _sigmoid(y)

    xs = conv_silu(xs_ref, xsh_ref, wx_ref, bx_ref)
    bm = conv_silu(bm_ref, bmh_ref, wb_ref, bb_ref)
    cm = conv_silu(cm_ref, cmh_ref, wc_ref, bc_ref)

    lane = lax.broadcasted_iota(jnp.int32, (1, 128), 1)
    xdt_in = dt_ref[...] + dtb_ref[...]
    dt = jnp.maximum(xdt_in, 0.0) + jnp.log1p(jnp.exp(-jnp.abs(xdt_in)))
    dt = jnp.where(lane < SSD_HEADS, dt, 0.0)
    da = dt * (-jnp.exp(alog_ref[...]))

    row = lax.broadcasted_iota(jnp.int32, (L, L), 0)
    colm = lax.broadcasted_iota(jnp.int32, (L, L), 1)
    causal = colm <= row
    tril = jnp.where(causal, 1.0, 0.0).astype(bf16)
    d_hi, d_mid, d_lo = _split3(da)
    acs = _dot(tril, d_hi) + _dot(tril, d_mid) + _dot(tril, d_lo)
    acs_t = acs.T
    last = acs[L - 1:L, :]
    eacs = jnp.exp(acs)
    dec = jnp.exp(last - acs)
    cdec = jnp.exp(jnp.broadcast_to(last, (8, 128)))

    eh = eh_ref[...]
    dt_x = _dot(dt.astype(bf16), eh)
    eacs_x = _dot(eacs.astype(bf16), eh)
    dec_x = _dot(dec.astype(bf16), eh)
    c_hi, c_lo = _split2(cdec)
    cdec_x = (_dot(c_hi, eh) + _dot(c_lo, eh))[0:1, :]

    xdt = xs * dt_x
    xdt_b = xdt.astype(bf16)
    xdec_b = (xdt * dec_x).astype(bf16)
    lane_hp = lax.broadcasted_iota(jnp.int32, (L, HP), 1)

    for g in range(SSD_G):
        cm_g = cm[:, g * SSD_N:(g + 1) * SSD_N].astype(bf16)
        bm_g = bm[:, g * SSD_N:(g + 1) * SSD_N]
        cb = _dot_nt(cm_g, bm_g.astype(bf16))
        s_in = st_ref[g]
        y_off = _dot(cm_g, s_in.astype(bf16))
        s_new = _dot(bm_g.T.astype(bf16), xdec_b[:, g * GW:(g + 1) * GW])
        st_ref[g] = s_in * cdec_x[:, g * GW:(g + 1) * GW] + s_new
        for pp in range(GW // HP):
            h0 = g * (GW // SSD_P) + 2 * pp
            c0 = g * GW + pp * HP
            ms = []
            for h in (h0, h0 + 1):
                diff = acs[:, h:h + 1] - acs_t[h:h + 1, :]
                ms.append(jnp.where(causal, cb * jnp.exp(jnp.where(causal, diff, NEG)), 0.0))
            lhs = jnp.concatenate(ms, axis=1).astype(bf16)
            xp = xdt_b[:, c0:c0 + HP]
            zero = jnp.zeros_like(xp)
            rhs = jnp.concatenate([jnp.where(lane_hp < SSD_P, xp, zero),
                                   jnp.where(lane_hp >= SSD_P, xp, zero)], axis=0)
            y = (_dot(lhs, rhs) + eacs_x[:, c0:c0 + HP] * y_off[:, pp * HP:(pp + 1) * HP]
                 + dskip_ref[:, c0:c0 + HP] * xs[:, c0:c0 + HP])
            y_ref[:, c0:c0 + HP] = y

    z = z_ref[...].astype(f32)
    y = y_ref[...] * (z * _sigmoid(z))
    for g in range(SSD_G):
        yg = y[:, g * GW:(g + 1) * GW]
        ms = jnp.mean(yg * yg, axis=-1, keepdims=True)
        o_ref[:, g * GW:(g + 1) * GW] = (yg * lax.rsqrt(ms + EPS) * ng_ref[:, g * GW:(g + 1) * GW]).astype(bf16)


def _ssd(proj, small, wts, B, T):
    nch = T // SSD_L
    L = SSD_L

    def main(name, w):
        base = _blk(name, w)
        return pl.BlockSpec((L, w), lambda b, c: (b * nch + c, base))

    def halo(name, w):
        base = _blk(name, w)
        return pl.BlockSpec((HALO, w), lambda b, c: (jnp.maximum((b * nch + c) * (L // HALO) - 1, 0), base))

    def full(a):
        return pl.BlockSpec(a.shape, lambda b, c: (0,) * a.ndim)

    gn = SSD_G * SSD_N
    return pl.pallas_call(
        _ssd_kernel,
        grid=(B, nch),
        in_specs=[main('s_z', D), main('xs', D), main('bm', gn), main('cm', gn),
                  pl.BlockSpec((L, 128), lambda b, c: (b * nch + c, G)),
                  halo('xs', D), halo('bm', gn), halo('cm', gn)] + [full(a) for a in wts],
        out_specs=pl.BlockSpec((L, D), lambda b, c: (b * nch + c, 0)),
        out_shape=jax.ShapeDtypeStruct((B * T, D), bf16),
        scratch_shapes=[pltpu.VMEM((SSD_G, SSD_N, D // SSD_G), f32),
                        pltpu.VMEM((HALO + L, D), f32),
                        pltpu.VMEM((L, D), f32)],
        compiler_params=_cparams(("parallel", "arbitrary")),
        name="ssd",
    )(proj, proj, proj, proj, small, proj, proj, proj, *wts)


def _merge_kernel(ya_ref, cb_ref, cc_ref, cu_ref, cch_ref, cuh_ref, yc_ref,
                  g0_ref, g1_ref, g2_ref, x_ref, cw_ref, wo_ref, o_ref, ext_ref, *, tm, tiles_per_seq):
    first = (pl.program_id(0) % tiles_per_seq) == 0
    up = lambda ref: ref[...].astype(f32)
    ext_ref[0:HALO, :] = jnp.where(first, 0.0, up(cch_ref) * up(cuh_ref))
    ext_ref[HALO:HALO + tm, :] = up(cc_ref) * up(cu_ref)
    lo = HALO - (CONV_W - 1)
    conv = cw_ref[0:1, :] * ext_ref[lo:lo + tm, :]
    for j in range(1, CONV_W):
        conv = conv + cw_ref[j:j + 1, :] * ext_ref[lo + j:lo + j + tm, :]
    yb = up(cb_ref) * conv
    merged = (_sigmoid(up(g0_ref)) * up(ya_ref) + _sigmoid(up(g1_ref)) * yb
              + _sigmoid(up(g2_ref)) * up(yc_ref))
    o_ref[...] = x_ref[...] + _dot(merged.astype(bf16), wo_ref[...])


def _merge(ya, yc, proj, x2, cw, wo, T, tm):
    M = x2.shape[0]

    def col(name):
        base = _blk(name, D)
        return pl.BlockSpec((tm, D), lambda i: (i, base))

    def halo(name):
        base = _blk(name, D)
        return pl.BlockSpec((HALO, D), lambda i: (jnp.maximum(i * (tm // HALO) - 1, 0), base))

    row = pl.BlockSpec((tm, D), lambda i: (i, 0))
    return pl.pallas_call(
        functools.partial(_merge_kernel, tm=tm, tiles_per_seq=T // tm),
        grid=(M // tm,),
        in_specs=[row, col('cb'), col('cc'), col('cu'), halo('cc'), halo('cu'), row,
                  col('gm0'), col('gm1'), col('gm2'), row,
                  pl.BlockSpec((CONV_W, D), lambda i: (0, 0)),
                  pl.BlockSpec((D, D), lambda i: (0, 0))],
        out_specs=row,
        out_shape=jax.ShapeDtypeStruct((M, D), f32),
        scratch_shapes=[pltpu.VMEM((HALO + tm, D), f32)],
        compiler_params=_cparams(("parallel",)),
        name="merge_oproj",
    )(ya, proj, proj, proj, proj, proj, yc, proj, proj, proj, x2, cw, wo)


def _ffn_kernel(x_ref, g_ref, wu_ref, wd_ref, o_ref, h_ref):
    @pl.when(pl.program_id(1) == 0)
    def _():
        x = x_ref[...]
        h_ref[...] = _rms(x, g_ref[...]).astype(bf16)
        o_ref[...] = x

    u = jnp.maximum(_dot(h_ref[...], wu_ref[...]), 0.0)
    o_ref[...] += _dot((u * u).astype(bf16), wd_ref[...])


def _ffn(x2, g, wu, wd, tm, tf):
    M = x2.shape[0]
    return pl.pallas_call(
        _ffn_kernel,
        grid=(M // tm, D_FF // tf),
        in_specs=[pl.BlockSpec((tm, D), lambda i, j: (i, 0)),
                  pl.BlockSpec((1, D), lambda i, j: (0, 0)),
                  pl.BlockSpec((D, tf), lambda i, j: (0, j)),
                  pl.BlockSpec((tf, D), lambda i, j: (j, 0))],
        out_specs=pl.BlockSpec((tm, D), lambda i, j: (i, 0)),
        out_shape=jax.ShapeDtypeStruct((M, D), f32),
        scratch_shapes=[pltpu.VMEM((tm, D), bf16)],
        compiler_params=_cparams(("parallel", "arbitrary")),
        name="ffn",
    )(x2, g, wu, wd)


def _ple_kernel(x_ref, p_ref, g_ref, wp_ref, wg_ref, gf_ref, o_ref, *, final):
    x = x_ref[...]
    gate = _sigmoid(_dot(_rms(x, g_ref[...]).astype(bf16), wg_ref[...]))
    y = x + _dot(p_ref[...].astype(bf16), wp_ref[...]) * gate
    if final:
        y = _rms(y, gf_ref[...])
    o_ref[...] = y


def _ple(x2, p2, g, wp, wg, gf, tm, final):
    M = x2.shape[0]
    row = pl.BlockSpec((tm, D), lambda i: (i, 0))
    vec = pl.BlockSpec((1, D), lambda i: (0, 0))
    return pl.pallas_call(
        functools.partial(_ple_kernel, final=final),
        grid=(M // tm,),
        in_specs=[row, pl.BlockSpec((tm, PLE), lambda i: (i, 0)), vec,
                  pl.BlockSpec((PLE, D), lambda i: (0, 0)),
                  pl.BlockSpec((D, D), lambda i: (0, 0)), vec],
        out_specs=row,
        out_shape=jax.ShapeDtypeStruct((M, D), f32),
        compiler_params=_cparams(("parallel",)),
        name="ple",
    )(x2, p2, g, wp, wg, gf)


def _prep_w_in(w):
    main = jnp.concatenate([w[:, _SRC[n][0]:_SRC[n][0] + _SRC[n][1]] for n in _ORDER], axis=1)
    a, wd = _SRC['g_nsa']
    gn = w[:, a:a + wd].reshape(D, G, R * 3)
    gn = jnp.pad(gn, ((0, 0), (0, 0), (0, 128 - R * 3))).reshape(D, G * 128)
    a, wd = _SRC['s_dt']
    dt = jnp.pad(w[:, a:a + wd], ((0, 0), (0, 128 - wd)))
    return main.astype(bf16), jnp.concatenate([gn, dt], axis=1).astype(bf16)


def _overlap_matrix(nc):
    i = np.arange(nc)[:, None]
    j = np.arange(NSB)[None, :]
    ovl = (i * CMP_STRIDE < j * SLC + SLC) & (i * CMP_STRIDE + CMP_BLOCK > j * SLC)
    return jnp.asarray(ovl.T.astype(np.float32), dtype=bf16)


def _head_expand():
    h = np.arange(128)[:, None]
    ch = np.arange(D)[None, :]
    return jnp.asarray((ch // SSD_P == h).astype(np.float32), dtype=bf16)


def kernel(x, p, positions, g_mix, w_in, nsa_pe_k, nsa_pe_v, phi_k_w1, phi_k_b1, phi_k_w2, phi_k_b2,
           phi_v_w1, phi_v_b1, phi_v_w2, phi_v_b2, sconv_w, ssd_conv_w, ssd_conv_b, ssd_dt_bias,
           ssd_a_log, ssd_d, ssd_norm_g, w_o, g_mlp, w_up, w_down, g_ple, w_ple, w_ple_gate, g_final):
    B, T, _ = x.shape
    depth = w_in.shape[0]
    M = B * T
    assert T % 256 == 0 and T // SLC <= NSB
    tq = 256
    tm_proj = 1024 if M % 1024 == 0 else 256
    tm = 512 if T % 512 == 0 else 256
    nc = T // CMP_STRIDE

    inv_freq = 1.0 / (10000.0 ** (jnp.arange(0, DK, 2, dtype=f32) / DK))
    ang = positions.astype(f32)[..., None] * inv_freq
    cosf = jnp.concatenate([jnp.cos(ang), jnp.cos(ang)], axis=-1).reshape(M, DK)
    sinf = jnp.concatenate([-jnp.sin(ang), jnp.sin(ang)], axis=-1).reshape(M, DK)
    ovl = _overlap_matrix(nc)
    eh = _head_expand()

    def vec(a, n=None):
        a = a.reshape(1, -1).astype(f32)
        return a if n is None else jnp.pad(a, ((0, 0), (0, n - a.shape[1])))

    x2 = x.reshape(M, D)
    for i in range(depth):
        w_main, w_small = _prep_w_in(w_in[i])
        proj, small = _proj(x2, vec(g_mix[i]), w_main, w_small, tm_proj, 2048)

        kc, vc, ksa, vs, kw, vw = _prep(proj, cosf, sinf, B, T, tq)

        def phi(pe, w1, b1, w2, b2):
            pe8 = jnp.broadcast_to(pe.reshape(1, -1), (8, CMP_BLOCK * DK)).astype(bf16)
            return (pe8, w1.astype(bf16), vec(b1), w2.astype(bf16), vec(b2))

        kcmp, vcmp = _compress(kc.reshape(B, G, nc, CMP_STRIDE * DK), vc.reshape(B, G, nc, CMP_STRIDE * DK),
                               phi(nsa_pe_k[i], phi_k_w1[i], phi_k_b1[i], phi_k_w2[i], phi_k_b2[i]),
                               phi(nsa_pe_v[i], phi_v_w1[i], phi_v_b1[i], phi_v_w2[i], phi_v_b2[i]))
        ya = _nsa(proj, small, cosf, sinf, ovl, kcmp, vcmp, ksa, vs, kw, vw, B, T, tq)

        cw, cbias = ssd_conv_w[i], ssd_conv_b[i]
        gn = SSD_G * SSD_N
        ssd_w = (cw[:, :D], vec(cbias[:D]), cw[:, D:D + gn], vec(cbias[D:D + gn]),
                 cw[:, D + gn:], vec(cbias[D + gn:]),
                 vec(ssd_dt_bias[i], 128), vec(ssd_a_log[i], 128),
                 vec(jnp.repeat(ssd_d[i], SSD_P)), vec(ssd_norm_g[i]), eh)
        yc = _ssd(proj, small, ssd_w, B, T)

        x2 = _merge(ya, yc, proj, x2, sconv_w[i], w_o[i].astype(bf16), T, 256)
        x2 = _ffn(x2, vec(g_mlp[i]), w_up[i].astype(bf16), w_down[i].astype(bf16), tm_proj, 512)
        x2 = _ple(x2, p[i].reshape(M, PLE), vec(g_ple[i]), w_ple[i].astype(bf16),
                  w_ple_gate[i].astype(bf16), vec(g_final), tm, final=(i == depth - 1))
    return x2.reshape(B, T, D)
```

```python
import functools

import numpy as np
import jax
import jax.numpy as jnp
from jax import lax
from jax.experimental import pallas as pl
from jax.experimental.pallas import tpu as pltpu

f32 = jnp.float32
bf16 = jnp.bfloat16

D = 2048
N_HEADS = 16
DK = 128
G = 4
R = N_HEADS // G
CMP_BLOCK = 32
CMP_STRIDE = 16
SLC = 64
N_SELECT = 16
WINDOW = 512
FORCE_BONUS = 1.0e4
CONV_W = 3
SSD_HEADS = 32
SSD_P = 64
SSD_G = 4
SSD_N = 128
SSD_CONV = 4
SSD_L = 128
D_FF = 4 * D
PLE = 256
EPS = 1e-6
NSB = 128
HALO = 16
NEG = -1e30
LOG2E = 1.4426950408889634

VMEM_LIMIT = 56 * 1024 * 1024

_SRC = {
    'q': (0, 2048), 'k_c': (2048, 512), 'v_c': (2560, 512), 'k_s': (3072, 512), 'v_s': (3584, 512),
    'k_w': (4096, 512), 'v_w': (4608, 512), 'g_nsa': (5120, 48),
    'cb': (5168, 2048), 'cc': (7216, 2048), 'cu': (9264, 2048),
    's_z': (11312, 2048), 'xs': (13360, 2048), 'bm': (15408, 512), 'cm': (15920, 512),
    's_dt': (16432, 32), 'gm0': (16464, 2048), 'gm1': (18512, 2048), 'gm2': (20560, 2048),
}
_ORDER = ['q', 'cb', 'cc', 'cu', 's_z', 'xs', 'gm0', 'gm1', 'gm2',
          'k_c', 'v_c', 'k_s', 'v_s', 'k_w', 'v_w', 'bm', 'cm']
_OFF = {}
_o = 0
for _n in _ORDER:
    _OFF[_n] = _o
    _o += _SRC[_n][1]
N_MAIN = _o
N_SMALL = G * 128 + 128


def _blk(name, width):
    off = _OFF[name]
    assert off % width == 0
    return off // width


def _cparams(sem):
    return pltpu.CompilerParams(dimension_semantics=sem, vmem_limit_bytes=VMEM_LIMIT)


def _dot(a, b):
    return jnp.dot(a, b, preferred_element_type=f32)


def _dot_nt(a, b):
    return lax.dot_general(a, b, (((1,), (1,)), ((), ())), preferred_element_type=f32)


def _sigmoid(x):
    return 1.0 / (1.0 + jnp.exp(-x))


def _split2(x):
    hi = x.astype(bf16)
    lo = (x - hi.astype(f32)).astype(bf16)
    return hi, lo


def _split3(x):
    hi = x.astype(bf16)
    r1 = x - hi.astype(f32)
    mid = r1.astype(bf16)
    lo = (r1 - mid.astype(f32)).astype(bf16)
    return hi, mid, lo


def _rms(x, g):
    ms = jnp.mean(x * x, axis=-1, keepdims=True)
    return x * lax.rsqrt(ms + EPS) * g


def _proj_kernel(x_ref, g_ref, w_ref, ws_ref, o_ref, os_ref, h_ref):
    @pl.when(pl.program_id(1) == 0)
    def _():
        h = _rms(x_ref[...], g_ref[...]).astype(bf16)
        h_ref[...] = h
        os_ref[...] = _dot(h, ws_ref[...])

    o_ref[...] = _dot(h_ref[...], w_ref[...]).astype(bf16)


def _proj(x2, g, w_main, w_small, tm, tn):
    M = x2.shape[0]
    return pl.pallas_call(
        _proj_kernel,
        grid=(M // tm, N_MAIN // tn),
        in_specs=[
            pl.BlockSpec((tm, D), lambda i, j: (i, 0), pipeline_mode=pl.Buffered(1)),
            pl.BlockSpec((1, D), lambda i, j: (0, 0)),
            pl.BlockSpec((D, tn), lambda i, j: (0, j)),
            pl.BlockSpec((D, N_SMALL), lambda i, j: (0, 0), pipeline_mode=pl.Buffered(1)),
        ],
        out_specs=[
            pl.BlockSpec((tm, tn), lambda i, j: (i, j)),
            pl.BlockSpec((tm, N_SMALL), lambda i, j: (i, 0)),
        ],
        out_shape=[jax.ShapeDtypeStruct((M, N_MAIN), bf16),
                   jax.ShapeDtypeStruct((M, N_SMALL), f32)],
        scratch_shapes=[pltpu.VMEM((tm, D), bf16)],
        compiler_params=_cparams(("parallel", "arbitrary")),
        name="proj",
    )(x2, g, w_main, w_small)


def _rope(x, cosf, sinf):
    x = x.astype(f32)
    return x * cosf + pltpu.roll(x, DK // 2, axis=1) * sinf


def _prep_kernel(kc_ref, vc_ref, ks_ref, vs_ref, kw_ref, vw_ref, cos_ref, sin_ref,
                 kco_ref, vco_ref, ksa_ref, vso_ref, kwo_ref, vwo_ref, *, tt):
    cosf = cos_ref[...]
    sinf = sin_ref[...]
    kco_ref[...] = _rope(kc_ref[...], cosf, sinf).astype(bf16)
    vco_ref[...] = vc_ref[...].astype(bf16)
    ksa_ref[:, 0:DK] = _rope(ks_ref[...], cosf, sinf).astype(bf16)
    t = pl.program_id(2) * tt + lax.broadcasted_iota(jnp.int32, (tt, NSB), 0)
    j = lax.broadcasted_iota(jnp.int32, (tt, NSB), 1)
    ksa_ref[:, DK:DK + NSB] = jnp.where(t // SLC == j, 1.0, 0.0).astype(bf16)
    vso_ref[...] = vs_ref[...].astype(f32).T.astype(bf16)
    kwo_ref[...] = _rope(kw_ref[...], cosf, sinf).astype(bf16)
    vwo_ref[...] = vw_ref[...].astype(f32).T.astype(bf16)


def _prep(proj, cosf, sinf, B, T, tt):
    nt = T // tt

    def col(name):
        base = _blk(name, DK)
        return pl.BlockSpec((tt, DK), lambda b, g, i: (b * nt + i, base + g))

    tab = pl.BlockSpec((tt, DK), lambda b, g, i: (b * nt + i, 0))

    def out(w):
        return pl.BlockSpec((None, None, tt, w), lambda b, g, i: (b, g, i, 0))

    out_t = pl.BlockSpec((None, None, None, DK, tt), lambda b, g, i: (b, g, i, 0, 0))
    return pl.pallas_call(
        functools.partial(_prep_kernel, tt=tt),
        grid=(B, G, nt),
        in_specs=[col('k_c'), col('v_c'), col('k_s'), col('v_s'), col('k_w'), col('v_w'), tab, tab],
        out_specs=[out(DK), out(DK), out(DK + NSB), out_t, out(DK), out_t],
        out_shape=[jax.ShapeDtypeStruct((B, G, T, DK), bf16),
                   jax.ShapeDtypeStruct((B, G, T, DK), bf16),
                   jax.ShapeDtypeStruct((B, G, T, DK + NSB), bf16),
                   jax.ShapeDtypeStruct((B, G, nt, DK, tt), bf16),
                   jax.ShapeDtypeStruct((B, G, T, DK), bf16),
                   jax.ShapeDtypeStruct((B, G, nt, DK, tt), bf16)],
        compiler_params=_cparams(("parallel", "parallel", "parallel")),
        name="kv_prep",
    )(proj, proj, proj, proj, proj, proj, cosf, sinf)


def _compress_one(x_ref, pe_ref, w1_ref, b1_ref, w2_ref, b2_ref, o_ref, transpose):
    nc = x_ref.shape[0]
    half = CMP_STRIDE * DK
    x = x_ref[...]
    first = _dot(x, w1_ref[0:half, :])
    second = _dot(x, w1_ref[half:2 * half, :])
    second = pltpu.roll(second, nc - 1, axis=0)
    pe = _dot(pe_ref[...], w1_ref[...])[0:1, :]
    h = first + second + pe + b1_ref[...]
    h = h * _sigmoid(h)
    y = _dot(h.astype(bf16), w2_ref[...]) + b2_ref[...]
    o_ref[...] = (y.T if transpose else y).astype(bf16)


def _compress_kernel(k_ref, v_ref, pek, w1k, b1k, w2k, b2k, pev, w1v, b1v, w2v, b2v, ko_ref, vo_ref):
    _compress_one(k_ref, pek, w1k, b1k, w2k, b2k, ko_ref, False)
    _compress_one(v_ref, pev, w1v, b1v, w2v, b2v, vo_ref, True)


def _compress(kc2, vc2, wk, wv):
    B, _, nc, width = kc2.shape
    xin = pl.BlockSpec((None, None, nc, width), lambda b, g: (b, g, 0, 0))

    def full(a):
        return pl.BlockSpec(a.shape, lambda b, g: (0,) * a.ndim)

    out = pl.BlockSpec((None, None, nc, DK), lambda b, g: (b, g, 0, 0))
    out_t = pl.BlockSpec((None, None, DK, nc), lambda b, g: (b, g, 0, 0))
    return pl.pallas_call(
        _compress_kernel,
        grid=(B, G),
        in_specs=[xin, xin] + [full(a) for a in wk] + [full(a) for a in wv],
        out_specs=[out, out_t],
        out_shape=[jax.ShapeDtypeStruct((B, G, nc, DK), bf16),
                   jax.ShapeDtypeStruct((B, G, DK, nc), bf16)],
        compiler_params=_cparams(("parallel", "parallel")),
        name="compress",
    )(kc2, vc2, *wk, *wv)


def _nsa_kernel(q_ref, cos_ref, sin_ref, gate_ref, ovl_ref, kcmp_ref, vcmp_ref,
                ksa_ref, vs_ref, kw_ref, vw_ref, o_ref,
                qa_ref, ms_ref, ls_ref, accs_ref, mw_ref, lw_ref, accw_ref, out_ref,
                sc_ref, mx_ref, sd_ref, md_ref,
                *, tq, n_sel):
    qi = pl.program_id(2)
    q0 = qi * tq
    heads = [slice(r * tq, (r + 1) * tq) for r in range(R)]
    qscale = DK ** -0.5 * LOG2E
    cosf = cos_ref[...]
    sinf = sin_ref[...]
    for r in range(R):
        qr = _rope(q_ref[:, r * DK:(r + 1) * DK], cosf, sinf) * qscale
        qa_ref[0:DK, heads[r]] = qr.T.astype(bf16)
    t1 = q0 + lax.broadcasted_iota(jnp.int32, (1, tq), 1)
    gate_t = _sigmoid(gate_ref[...]).T

    def gate_row(c, r):
        return gate_t[3 * r + c:3 * r + c + 1, :]

    nc = kcmp_ref.shape[0]
    n_idx = lax.broadcasted_iota(jnp.int32, (nc, 1), 0)
    vis = (n_idx * CMP_STRIDE + (CMP_BLOCK - 1)) <= t1
    has_any = t1 >= CMP_BLOCK - 1
    psum = jnp.zeros((nc, tq), f32)
    for r in range(R):
        s = jnp.where(vis, _dot(kcmp_ref[...], qa_ref[0:DK, heads[r]]), NEG)
        e = jnp.exp2(s - jnp.max(s, axis=0, keepdims=True))
        inv = jnp.where(has_any, 1.0 / jnp.sum(e, axis=0, keepdims=True), 0.0)
        out_ref[:, heads[r]] = (gate_row(0, r) * inv) * _dot(vcmp_ref[...], e.astype(bf16))
        psum = psum + e * inv
    p_hi, p_lo = _split2(psum)
    imp = _dot(ovl_ref[...], p_hi) + _dot(ovl_ref[...], p_lo)

    def reset(m_ref, l_ref, acc_ref):
        m_ref[...] = jnp.full(m_ref.shape, NEG, f32)
        l_ref[...] = jnp.zeros(l_ref.shape, f32)
        acc_ref[...] = jnp.zeros(acc_ref.shape, f32)

    def scores_to(s_ref, smax_ref, k_ref, kt, q_rows, mask, r):
        s = _dot(k_ref[pl.ds(pl.multiple_of(kt * tq, tq), tq), :], qa_ref[q_rows, heads[r]])
        if mask is not None:
            s = jnp.where(mask, s, NEG)
        s_ref[:, heads[r]] = s
        smax_ref[:, heads[r]] = jnp.max(s, axis=0, keepdims=True)

    def softmax_pv_head(state, r, s, smax, v_tiles):
        m_ref, l_ref, acc_ref = state
        m_prev = m_ref[:, heads[r]]
        m_new = jnp.maximum(m_prev, smax)
        alpha = jnp.exp2(m_prev - m_new)
        pt = jnp.exp2(s - m_new)
        ptb = pt.astype(bf16)
        pv = _dot(v_tiles[0], ptb[0:tq])
        for i in range(1, len(v_tiles)):
            pv = pv + _dot(v_tiles[i], ptb[i * tq:(i + 1) * tq])
        l_ref[:, heads[r]] = alpha * l_ref[:, heads[r]] + jnp.sum(pt, axis=0, keepdims=True)
        acc_ref[:, heads[r]] = alpha * acc_ref[:, heads[r]] + pv
        m_ref[:, heads[r]] = m_new

    def finish(state, c):
        m_ref, l_ref, acc_ref = state
        for r in range(R):
            out_ref[:, heads[r]] += (gate_row(c, r) * (1.0 / l_ref[:, heads[r]])) * acc_ref[:, heads[r]]

    plain = slice(0, DK)
    aug = slice(0, DK + NSB)

    def masked_tile(state, k_ref, v_ref, kt, q_rows, mask):
        for r in range(R):
            scores_to(sd_ref, md_ref, k_ref, kt, q_rows, mask, r)
        for r in range(R):
            softmax_pv_head(state, r, sd_ref[:, heads[r]], md_ref[:, heads[r]], [v_ref[kt]])

    kpos = lax.broadcasted_iota(jnp.int32, (tq, 1), 0)
    causal = q0 + kpos <= t1

    win = (mw_ref, lw_ref, accw_ref)
    reset(*win)
    masked_tile(win, kw_ref, vw_ref, qi, plain, causal)
    for w in range(1, -(-(WINDOW - 1) // tq) + 1):
        inside = ((q0 - w * tq) + kpos > t1 - WINDOW) & (qi >= w)
        masked_tile(win, kw_ref, vw_ref, jnp.maximum(qi - w, 0), plain, inside)
    finish(win, 2)

    j_idx = lax.broadcasted_iota(jnp.int32, (NSB, 1), 0)
    j_f = j_idx.astype(f32)
    cur = t1 // SLC
    valid = (j_idx * SLC) <= t1
    forced = (j_idx == 0) | (j_idx == cur) | (j_idx == cur - 1)
    score = jnp.where(valid, imp + jnp.where(forced, FORCE_BONUS, 0.0), -jnp.inf)
    sel = jnp.zeros((NSB, tq), f32)
    for _ in range(n_sel):
        mx = jnp.max(score, axis=0, keepdims=True)
        first = jnp.min(jnp.where(score == mx, j_f, float(NSB)), axis=0, keepdims=True)
        hit = j_f == first
        sel = jnp.where(hit, 1.0, sel)
        score = jnp.where(hit, -jnp.inf, score)
    nsel = jnp.where(sel > 0.5, 0.0, NEG).astype(bf16)
    for r in range(R):
        qa_ref[DK:DK + NSB, heads[r]] = nsel

    selst = (ms_ref, ls_ref, accs_ref)
    reset(*selst)
    for r in range(R):
        scores_to(sc_ref, mx_ref, ksa_ref, 0, aug, None, r)
    masked_tile(selst, ksa_ref, vs_ref, qi, aug, causal)

    def sel_body(kt, carry):
        nxt = jnp.minimum(kt + 1, jnp.maximum(qi - 1, 0))
        for r in range(R):
            softmax_pv_head(selst, r, sc_ref[:, heads[r]], mx_ref[:, heads[r]], [vs_ref[kt]])
            scores_to(sc_ref, mx_ref, ksa_ref, nxt, aug, None, r)
        return carry

    lax.fori_loop(0, qi, sel_body, 0)
    finish(selst, 1)

    for r in range(R):
        o_ref[:, r * DK:(r + 1) * DK] = out_ref[:, heads[r]].T.astype(bf16)


def _nsa(proj, small, cosf, sinf, ovl_t, kcmp, vcmp_t, ksa, vs_t, kw, vw_t, B, T, tq):
    nq = T // tq
    nc = kcmp.shape[2]
    rows = R * tq
    n_sel = min(N_SELECT, T // SLC)
    assert nq > -(-(WINDOW - 1) // tq)

    def per_bg(*shape):
        return pl.BlockSpec((None, None) + shape, lambda b, g, i: (b, g) + (0,) * len(shape))

    tab = pl.BlockSpec((tq, DK), lambda b, g, i: (b * nq + i, 0))
    return pl.pallas_call(
        functools.partial(_nsa_kernel, tq=tq, n_sel=n_sel),
        grid=(B, G, nq),
        in_specs=[
            pl.BlockSpec((tq, R * DK), lambda b, g, i: (b * nq + i, g)),
            tab, tab,
            pl.BlockSpec((tq, 128), lambda b, g, i: (b * nq + i, g)),
            pl.BlockSpec((NSB, nc), lambda b, g, i: (0, 0)),
            per_bg(nc, DK), per_bg(DK, nc),
            per_bg(T, DK + NSB), per_bg(nq, DK, tq), per_bg(T, DK), per_bg(nq, DK, tq),
        ],
        out_specs=pl.BlockSpec((tq, R * DK), lambda b, g, i: (b * nq + i, g)),
        out_shape=jax.ShapeDtypeStruct((B * T, D), bf16),
        scratch_shapes=[pltpu.VMEM((DK + NSB, rows), bf16)]
        + [pltpu.VMEM((1, rows), f32), pltpu.VMEM((1, rows), f32), pltpu.VMEM((DK, rows), f32)] * 2
        + [pltpu.VMEM((DK, rows), f32)]
        + [pltpu.VMEM((tq, rows), f32), pltpu.VMEM((1, rows), f32)] * 2,
        compiler_params=_cparams(("parallel", "parallel", "arbitrary")),
        name="nsa_attention",
    )(proj, cosf, sinf, small, ovl_t, kcmp, vcmp_t, ksa, vs_t, kw, vw_t)


def _ssd_kernel(z_ref, xs_ref, bm_ref, cm_ref, dt_ref, xsh_ref, bmh_ref, cmh_ref,
                wx_ref, bx_ref, wb_ref, bb_ref, wc_ref, bc_ref,
                dtb_ref, alog_ref, dskip_ref, ng_ref, eh_ref,
                o_ref, st_ref, ext_ref, y_ref):
    c = pl.program_id(1)
    L = SSD_L
    HP = 2 * SSD_P
    GW = D // SSD_G

    @pl.when(c == 0)
    def _():
        st_ref[...] = jnp.zeros_like(st_ref)

    def conv_silu(u_ref, h_ref, w_ref, b_ref):
        width = u_ref.shape[1]
        ext_ref[0:HALO, 0:width] = jnp.where(c == 0, 0.0, h_ref[...].astype(f32))
        ext_ref[HALO:HALO + L, 0:width] = u_ref[...].astype(f32)
        y = b_ref[...]
        for j in range(SSD_CONV):
            lo = HALO - (SSD_CONV - 1) + j
            y = y + w_ref[j:j + 1, :] * ext_ref[lo:lo + L, 0:width]
        return y * _sigmoid(y)

    xs = conv_silu(xs_ref, xsh_ref, wx_ref, bx_ref)
    bm = conv_silu(bm_ref, bmh_ref, wb_ref, bb_ref)
    cm = conv_silu(cm_ref, cmh_ref, wc_ref, bc_ref)

    lane = lax.broadcasted_iota(jnp.int32, (1, 128), 1)
    xdt_in = dt_ref[...] + dtb_ref[...]
    dt = jnp.maximum(xdt_in, 0.0) + jnp.log1p(jnp.exp(-jnp.abs(xdt_in)))
    dt = jnp.where(lane < SSD_HEADS, dt, 0.0)
    da = dt * (-jnp.exp(alog_ref[...]))

    row = lax.broadcasted_iota(jnp.int32, (L, L), 0)
    colm = lax.broadcasted_iota(jnp.int32, (L, L), 1)
    causal = colm <= row
    tril = jnp.where(causal, 1.0, 0.0).astype(bf16)
    d_hi, d_mid, d_lo = _split3(da)
    acs = _dot(tril, d_hi) + _dot(tril, d_mid) + _dot(tril, d_lo)
    acs_t = acs.T
    last = acs[L - 1:L, :]
    eacs = jnp.exp(acs)
    dec = jnp.exp(last - acs)
    cdec = jnp.exp(jnp.broadcast_to(last, (8, 128)))

    eh = eh_ref[...]
    dt_x = _dot(dt.astype(bf16), eh)
    eacs_x = _dot(eacs.astype(bf16), eh)
    dec_x = _dot(dec.astype(bf16), eh)
    c_hi, c_lo = _split2(cdec)
    cdec_x = (_dot(c_hi, eh) + _dot(c_lo, eh))[0:1, :]

    xdt = xs * dt_x
    xdt_b = xdt.astype(bf16)
    xdec_b = (xdt * dec_x).astype(bf16)
    lane_hp = lax.broadcasted_iota(jnp.int32, (L, HP), 1)

    for g in range(SSD_G):
        cm_g = cm[:, g * SSD_N:(g + 1) * SSD_N].astype(bf16)
        bm_g = bm[:, g * SSD_N:(g + 1) * SSD_N]
        cb = _dot_nt(cm_g, bm_g.astype(bf16))
        s_in = st_ref[g]
        y_off = _dot(cm_g, s_in.astype(bf16))
        s_new = _dot(bm_g.T.astype(bf16), xdec_b[:, g * GW:(g + 1) * GW])
        st_ref[g] = s_in * cdec_x[:, g * GW:(g + 1) * GW] + s_new
        for pp in range(GW // HP):
            h0 = g * (GW // SSD_P) + 2 * pp
            c0 = g * GW + pp * HP
            ms = []
            for h in (h0, h0 + 1):
                diff = acs[:, h:h + 1] - acs_t[h:h + 1, :]
                ms.append(jnp.where(causal, cb * jnp.exp(jnp.where(causal, diff, NEG)), 0.0))
            lhs = jnp.concatenate(ms, axis=1).astype(bf16)
            xp = xdt_b[:, c0:c0 + HP]
            zero = jnp.zeros_like(xp)
            rhs = jnp.concatenate([jnp.where(lane_hp < SSD_P, xp, zero),
                                   jnp.where(lane_hp >= SSD_P, xp, zero)], axis=0)
            y = (_dot(lhs, rhs) + eacs_x[:, c0:c0 + HP] * y_off[:, pp * HP:(pp + 1) * HP]
                 + dskip_ref[:, c0:c0 + HP] * xs[:, c0:c0 + HP])
            y_ref[:, c0:c0 + HP] = y

    z = z_ref[...].astype(f32)
    y = y_ref[...] * (z * _sigmoid(z))
    for g in range(SSD_G):
        yg = y[:, g * GW:(g + 1) * GW]
        ms = jnp.mean(yg * yg, axis=-1, keepdims=True)
        o_ref[:, g * GW:(g + 1) * GW] = (yg * lax.rsqrt(ms + EPS) * ng_ref[:, g * GW:(g + 1) * GW]).astype(bf16)


def _ssd(proj, small, wts, B, T):
    nch = T // SSD_L
    L = SSD_L

    def main(name, w):
        base = _blk(name, w)
        return pl.BlockSpec((L, w), lambda b, c: (b * nch + c, base))

    def halo(name, w):
        base = _blk(name, w)
        return pl.BlockSpec((HALO, w), lambda b, c: (jnp.maximum((b * nch + c) * (L // HALO) - 1, 0), base))

    def full(a):
        return pl.BlockSpec(a.shape, lambda b, c: (0,) * a.ndim)

    gn = SSD_G * SSD_N
    return pl.pallas_call(
        _ssd_kernel,
        grid=(B, nch),
        in_specs=[main('s_z', D), main('xs', D), main('bm', gn), main('cm', gn),
                  pl.BlockSpec((L, 128), lambda b, c: (b * nch + c, G)),
                  halo('xs', D), halo('bm', gn), halo('cm', gn)] + [full(a) for a in wts],
        out_specs=pl.BlockSpec((L, D), lambda b, c: (b * nch + c, 0)),
        out_shape=jax.ShapeDtypeStruct((B * T, D), bf16),
        scratch_shapes=[pltpu.VMEM((SSD_G, SSD_N, D // SSD_G), f32),
                        pltpu.VMEM((HALO + L, D), f32),
                        pltpu.VMEM((L, D), f32)],
        compiler_params=_cparams(("parallel", "arbitrary")),
        name="ssd",
    )(proj, proj, proj, proj, small, proj, proj, proj, *wts)


def _merge_kernel(ya_ref, cb_ref, cc_ref, cu_ref, cch_ref, cuh_ref, yc_ref,
                  g0_ref, g1_ref, g2_ref, x_ref, cw_ref, wo_ref, o_ref, ext_ref, *, tm, tiles_per_seq):
    first = (pl.program_id(0) % tiles_per_seq) == 0
    up = lambda ref: ref[...].astype(f32)
    ext_ref[0:HALO, :] = jnp.where(first, 0.0, up(cch_ref) * up(cuh_ref))
    ext_ref[HALO:HALO + tm, :] = up(cc_ref) * up(cu_ref)
    lo = HALO - (CONV_W - 1)
    conv = cw_ref[0:1, :] * ext_ref[lo:lo + tm, :]
    for j in range(1, CONV_W):
        conv = conv + cw_ref[j:j + 1, :] * ext_ref[lo + j:lo + j + tm, :]
    yb = up(cb_ref) * conv
    merged = (_sigmoid(up(g0_ref)) * up(ya_ref) + _sigmoid(up(g1_ref)) * yb
              + _sigmoid(up(g2_ref)) * up(yc_ref))
    o_ref[...] = x_ref[...] + _dot(merged.astype(bf16), wo_ref[...])


def _merge(ya, yc, proj, x2, cw, wo, T, tm):
    M = x2.shape[0]

    def col(name):
        base = _blk(name, D)
        return pl.BlockSpec((tm, D), lambda i: (i, base))

    def halo(name):
        base = _blk(name, D)
        return pl.BlockSpec((HALO, D), lambda i: (jnp.maximum(i * (tm // HALO) - 1, 0), base))

    row = pl.BlockSpec((tm, D), lambda i: (i, 0))
    return pl.pallas_call(
        functools.partial(_merge_kernel, tm=tm, tiles_per_seq=T // tm),
        grid=(M // tm,),
        in_specs=[row, col('cb'), col('cc'), col('cu'), halo('cc'), halo('cu'), row,
                  col('gm0'), col('gm1'), col('gm2'), row,
                  pl.BlockSpec((CONV_W, D), lambda i: (0, 0)),
                  pl.BlockSpec((D, D), lambda i: (0, 0))],
        out_specs=row,
        out_shape=jax.ShapeDtypeStruct((M, D), f32),
        scratch_shapes=[pltpu.VMEM((HALO + tm, D), f32)],
        compiler_params=_cparams(("parallel",)),
        name="merge_oproj",
    )(ya, proj, proj, proj, proj, proj, yc, proj, proj, proj, x2, cw, wo)


def _ffn_kernel(x_ref, g_ref, wu_ref, wd_ref, o_ref, h_ref):
    @pl.when(pl.program_id(1) == 0)
    def _():
        x = x_ref[...]
        h_ref[...] = _rms(x, g_ref[...]).astype(bf16)
        o_ref[...] = x

    u = jnp.maximum(_dot(h_ref[...], wu_ref[...]), 0.0)
    o_ref[...] += _dot((u * u).astype(bf16), wd_ref[...])


def _ffn(x2, g, wu, wd, tm, tf):
    M = x2.shape[0]
    return pl.pallas_call(
        _ffn_kernel,
        grid=(M // tm, D_FF // tf),
        in_specs=[pl.BlockSpec((tm, D), lambda i, j: (i, 0)),
                  pl.BlockSpec((1, D), lambda i, j: (0, 0)),
                  pl.BlockSpec((D, tf), lambda i, j: (0, j)),
                  pl.BlockSpec((tf, D), lambda i, j: (j, 0))],
        out_specs=pl.BlockSpec((tm, D), lambda i, j: (i, 0)),
        out_shape=jax.ShapeDtypeStruct((M, D), f32),
        scratch_shapes=[pltpu.VMEM((tm, D), bf16)],
        compiler_params=_cparams(("parallel", "arbitrary")),
        name="ffn",
    )(x2, g, wu, wd)


def _ple_kernel(x_ref, p_ref, g_ref, wp_ref, wg_ref, gf_ref, o_ref, *, final):
    x = x_ref[...]
    gate = _sigmoid(_dot(_rms(x, g_ref[...]).astype(bf16), wg_ref[...]))
    y = x + _dot(p_ref[...].astype(bf16), wp_ref[...]) * gate
    if final:
        y = _rms(y, gf_ref[...])
    o_ref[...] = y


def _ple(x2, p2, g, wp, wg, gf, tm, final):
    M = x2.shape[0]
    row = pl.BlockSpec((tm, D), lambda i: (i, 0))
    vec = pl.BlockSpec((1, D), lambda i: (0, 0))
    return pl.pallas_call(
        functools.partial(_ple_kernel, final=final),
        grid=(M // tm,),
        in_specs=[row, pl.BlockSpec((tm, PLE), lambda i: (i, 0)), vec,
                  pl.BlockSpec((PLE, D), lambda i: (0, 0)),
                  pl.BlockSpec((D, D), lambda i: (0, 0)), vec],
        out_specs=row,
        out_shape=jax.ShapeDtypeStruct((M, D), f32),
        compiler_params=_cparams(("parallel",)),
        name="ple",
    )(x2, p2, g, wp, wg, gf)


def _prep_w_in(w):
    main = jnp.concatenate([w[:, _SRC[n][0]:_SRC[n][0] + _SRC[n][1]] for n in _ORDER], axis=1)
    a, wd = _SRC['g_nsa']
    gn = w[:, a:a + wd].reshape(D, G, R * 3)
    gn = jnp.pad(gn, ((0, 0), (0, 0), (0, 128 - R * 3))).reshape(D, G * 128)
    a, wd = _SRC['s_dt']
    dt = jnp.pad(w[:, a:a + wd], ((0, 0), (0, 128 - wd)))
    return main.astype(bf16), jnp.concatenate([gn, dt], axis=1).astype(bf16)


def _overlap_matrix(nc):
    i = np.arange(nc)[:, None]
    j = np.arange(NSB)[None, :]
    ovl = (i * CMP_STRIDE < j * SLC + SLC) & (i * CMP_STRIDE + CMP_BLOCK > j * SLC)
    return jnp.asarray(ovl.T.astype(np.float32), dtype=bf16)


def _head_expand():
    h = np.arange(128)[:, None]
    ch = np.arange(D)[None, :]
    return jnp.asarray((ch // SSD_P == h).astype(np.float32), dtype=bf16)


def kernel(x, p, positions, g_mix, w_in, nsa_pe_k, nsa_pe_v, phi_k_w1, phi_k_b1, phi_k_w2, phi_k_b2,
           phi_v_w1, phi_v_b1, phi_v_w2, phi_v_b2, sconv_w, ssd_conv_w, ssd_conv_b, ssd_dt_bias,
           ssd_a_log, ssd_d, ssd_norm_g, w_o, g_mlp, w_up, w_down, g_ple, w_ple, w_ple_gate, g_final):
    B, T, _ = x.shape
    depth = w_in.shape[0]
    M = B * T
    assert T % 256 == 0 and T // SLC <= NSB
    tq = 512
    tm_proj = 1024 if M % 1024 == 0 else 256
    tm = 512 if T % 512 == 0 else 256
    nc = T // CMP_STRIDE

    inv_freq = 1.0 / (10000.0 ** (jnp.arange(0, DK, 2, dtype=f32) / DK))
    ang = positions.astype(f32)[..., None] * inv_freq
    cosf = jnp.concatenate([jnp.cos(ang), jnp.cos(ang)], axis=-1).reshape(M, DK)
    sinf = jnp.concatenate([-jnp.sin(ang), jnp.sin(ang)], axis=-1).reshape(M, DK)
    ovl = _overlap_matrix(nc)
    eh = _head_expand()

    def vec(a, n=None):
        a = a.reshape(1, -1).astype(f32)
        return a if n is None else jnp.pad(a, ((0, 0), (0, n - a.shape[1])))

    x2 = x.reshape(M, D)
    for i in range(depth):
        w_main, w_small = _prep_w_in(w_in[i])
        proj, small = _proj(x2, vec(g_mix[i]), w_main, w_small, tm_proj, 2048)

        kc, vc, ksa, vs, kw, vw = _prep(proj, cosf, sinf, B, T, tq)

        def phi(pe, w1, b1, w2, b2):
            pe8 = jnp.broadcast_to(pe.reshape(1, -1), (8, CMP_BLOCK * DK)).astype(bf16)
            return (pe8, w1.astype(bf16), vec(b1), w2.astype(bf16), vec(b2))

        kcmp, vcmp = _compress(kc.reshape(B, G, nc, CMP_STRIDE * DK), vc.reshape(B, G, nc, CMP_STRIDE * DK),
                               phi(nsa_pe_k[i], phi_k_w1[i], phi_k_b1[i], phi_k_w2[i], phi_k_b2[i]),
                               phi(nsa_pe_v[i], phi_v_w1[i], phi_v_b1[i], phi_v_w2[i], phi_v_b2[i]))
        ya = _nsa(proj, small, cosf, sinf, ovl, kcmp, vcmp, ksa, vs, kw, vw, B, T, tq)

        cw, cbias = ssd_conv_w[i], ssd_conv_b[i]
        gn = SSD_G * SSD_N
        ssd_w = (cw[:, :D], vec(cbias[:D]), cw[:, D:D + gn], vec(cbias[D:D + gn]),
                 cw[:, D + gn:], vec(cbias[D + gn:]),
                 vec(ssd_dt_bias[i], 128), vec(ssd_a_log[i], 128),
                 vec(jnp.repeat(ssd_d[i], SSD_P)), vec(ssd_norm_g[i]), eh)
        yc = _ssd(proj, small, ssd_w, B, T)

        x2 = _merge(ya, yc, proj, x2, sconv_w[i], w_o[i].astype(bf16), T, 256)
        x2 = _ffn(x2, vec(g_mlp[i]), w_up[i].astype(bf16), w_down[i].astype(bf16), tm_proj, 512)
        x2 = _ple(x2, p[i].reshape(M, PLE), vec(g_ple[i]), w_ple[i].astype(bf16),
                  w_ple_gate[i].astype(bf16), vec(g_final), tm, final=(i == depth - 1))
    return x2.reshape(B, T, D)
```

```python
import functools

import numpy as np
import jax
import jax.numpy as jnp
from jax import lax
from jax.experimental import pallas as pl
from jax.experimental.pallas import tpu as pltpu

f32 = jnp.float32
bf16 = jnp.bfloat16

D = 2048
N_HEADS = 16
DK = 128
G = 4
R = N_HEADS // G
CMP_BLOCK = 32
CMP_STRIDE = 16
SLC = 64
N_SELECT = 16
WINDOW = 512
FORCE_BONUS = 1.0e4
CONV_W = 3
SSD_HEADS = 32
SSD_P = 64
SSD_G = 4
SSD_N = 128
SSD_CONV = 4
SSD_L = 128
D_FF = 4 * D
PLE = 256
EPS = 1e-6
NSB = 128
HALO = 16
NEG = -1e30
LOG2E = 1.4426950408889634

VMEM_LIMIT = 56 * 1024 * 1024

_SRC = {
    'q': (0, 2048), 'k_c': (2048, 512), 'v_c': (2560, 512), 'k_s': (3072, 512), 'v_s': (3584, 512),
    'k_w': (4096, 512), 'v_w': (4608, 512), 'g_nsa': (5120, 48),
    'cb': (5168, 2048), 'cc': (7216, 2048), 'cu': (9264, 2048),
    's_z': (11312, 2048), 'xs': (13360, 2048), 'bm': (15408, 512), 'cm': (15920, 512),
    's_dt': (16432, 32), 'gm0': (16464, 2048), 'gm1': (18512, 2048), 'gm2': (20560, 2048),
}
_ORDER = ['q', 'cb', 'cc', 'cu', 's_z', 'xs', 'gm0', 'gm1', 'gm2',
          'k_c', 'v_c', 'k_s', 'v_s', 'k_w', 'v_w', 'bm', 'cm']
_OFF = {}
_o = 0
for _n in _ORDER:
    _OFF[_n] = _o
    _o += _SRC[_n][1]
N_MAIN = _o
N_SMALL = G * 128 + 128


def _blk(name, width):
    off = _OFF[name]
    assert off % width == 0
    return off // width


def _cparams(sem):
    return pltpu.CompilerParams(dimension_semantics=sem, vmem_limit_bytes=VMEM_LIMIT)


def _dot(a, b):
    return jnp.dot(a, b, preferred_element_type=f32)


def _dot_nt(a, b):
    return lax.dot_general(a, b, (((1,), (1,)), ((), ())), preferred_element_type=f32)


def _sigmoid(x):
    return 1.0 / (1.0 + jnp.exp(-x))


def _split2(x):
    hi = x.astype(bf16)
    lo = (x - hi.astype(f32)).astype(bf16)
    return hi, lo


def _split3(x):
    hi = x.astype(bf16)
    r1 = x - hi.astype(f32)
    mid = r1.astype(bf16)
    lo = (r1 - mid.astype(f32)).astype(bf16)
    return hi, mid, lo


def _rms(x, g):
    ms = jnp.mean(x * x, axis=-1, keepdims=True)
    return x * lax.rsqrt(ms + EPS) * g


def _proj_kernel(x_ref, g_ref, w_ref, ws_ref, o_ref, os_ref, h_ref):
    @pl.when(pl.program_id(1) == 0)
    def _():
        h = _rms(x_ref[...], g_ref[...]).astype(bf16)
        h_ref[...] = h
        os_ref[...] = _dot(h, ws_ref[...])

    o_ref[...] = _dot(h_ref[...], w_ref[...]).astype(bf16)


def _proj(x2, g, w_main, w_small, tm, tn):
    M = x2.shape[0]
    return pl.pallas_call(
        _proj_kernel,
        grid=(M // tm, N_MAIN // tn),
        in_specs=[
            pl.BlockSpec((tm, D), lambda i, j: (i, 0), pipeline_mode=pl.Buffered(1)),
            pl.BlockSpec((1, D), lambda i, j: (0, 0)),
            pl.BlockSpec((D, tn), lambda i, j: (0, j)),
            pl.BlockSpec((D, N_SMALL), lambda i, j: (0, 0), pipeline_mode=pl.Buffered(1)),
        ],
        out_specs=[
            pl.BlockSpec((tm, tn), lambda i, j: (i, j)),
            pl.BlockSpec((tm, N_SMALL), lambda i, j: (i, 0)),
        ],
        out_shape=[jax.ShapeDtypeStruct((M, N_MAIN), bf16),
                   jax.ShapeDtypeStruct((M, N_SMALL), f32)],
        scratch_shapes=[pltpu.VMEM((tm, D), bf16)],
        compiler_params=_cparams(("parallel", "arbitrary")),
        name="proj",
    )(x2, g, w_main, w_small)


def _rope(x, cosf, sinf):
    x = x.astype(f32)
    return x * cosf + pltpu.roll(x, DK // 2, axis=1) * sinf


def _prep_kernel(kc_ref, vc_ref, ks_ref, vs_ref, kw_ref, vw_ref, cos_ref, sin_ref,
                 kco_ref, vco_ref, ksa_ref, vso_ref, kwo_ref, vwo_ref, *, tt):
    cosf = cos_ref[...]
    sinf = sin_ref[...]
    kco_ref[...] = _rope(kc_ref[...], cosf, sinf).astype(bf16)
    vco_ref[...] = vc_ref[...].astype(bf16)
    ksa_ref[:, 0:DK] = _rope(ks_ref[...], cosf, sinf).astype(bf16)
    t = pl.program_id(2) * tt + lax.broadcasted_iota(jnp.int32, (tt, NSB), 0)
    j = lax.broadcasted_iota(jnp.int32, (tt, NSB), 1)
    ksa_ref[:, DK:DK + NSB] = jnp.where(t // SLC == j, 1.0, 0.0).astype(bf16)
    vso_ref[...] = vs_ref[...].astype(f32).T.astype(bf16)
    kwo_ref[...] = _rope(kw_ref[...], cosf, sinf).astype(bf16)
    vwo_ref[...] = vw_ref[...].astype(f32).T.astype(bf16)


def _prep(proj, cosf, sinf, B, T, tt):
    nt = T // tt

    def col(name):
        base = _blk(name, DK)
        return pl.BlockSpec((tt, DK), lambda b, g, i: (b * nt + i, base + g))

    tab = pl.BlockSpec((tt, DK), lambda b, g, i: (b * nt + i, 0))

    def out(w):
        return pl.BlockSpec((None, None, tt, w), lambda b, g, i: (b, g, i, 0))

    out_t = pl.BlockSpec((None, None, None, DK, tt), lambda b, g, i: (b, g, i, 0, 0))
    return pl.pallas_call(
        functools.partial(_prep_kernel, tt=tt),
        grid=(B, G, nt),
        in_specs=[col('k_c'), col('v_c'), col('k_s'), col('v_s'), col('k_w'), col('v_w'), tab, tab],
        out_specs=[out(DK), out(DK), out(DK + NSB), out_t, out(DK), out_t],
        out_shape=[jax.ShapeDtypeStruct((B, G, T, DK), bf16),
                   jax.ShapeDtypeStruct((B, G, T, DK), bf16),
                   jax.ShapeDtypeStruct((B, G, T, DK + NSB), bf16),
                   jax.ShapeDtypeStruct((B, G, nt, DK, tt), bf16),
                   jax.ShapeDtypeStruct((B, G, T, DK), bf16),
                   jax.ShapeDtypeStruct((B, G, nt, DK, tt), bf16)],
        compiler_params=_cparams(("parallel", "parallel", "parallel")),
        name="kv_prep",
    )(proj, proj, proj, proj, proj, proj, cosf, sinf)


def _compress_one(x_ref, pe_ref, w1_ref, b1_ref, w2_ref, b2_ref, o_ref, transpose):
    nc = x_ref.shape[0]
    half = CMP_STRIDE * DK
    x = x_ref[...]
    first = _dot(x, w1_ref[0:half, :])
    second = _dot(x, w1_ref[half:2 * half, :])
    second = pltpu.roll(second, nc - 1, axis=0)
    pe = _dot(pe_ref[...], w1_ref[...])[0:1, :]
    h = first + second + pe + b1_ref[...]
    h = h * _sigmoid(h)
    y = _dot(h.astype(bf16), w2_ref[...]) + b2_ref[...]
    o_ref[...] = (y.T if transpose else y).astype(bf16)


def _compress_kernel(k_ref, v_ref, pek, w1k, b1k, w2k, b2k, pev, w1v, b1v, w2v, b2v, ko_ref, vo_ref):
    _compress_one(k_ref, pek, w1k, b1k, w2k, b2k, ko_ref, False)
    _compress_one(v_ref, pev, w1v, b1v, w2v, b2v, vo_ref, True)


def _compress(kc2, vc2, wk, wv):
    B, _, nc, width = kc2.shape
    xin = pl.BlockSpec((None, None, nc, width), lambda b, g: (b, g, 0, 0))

    def full(a):
        return pl.BlockSpec(a.shape, lambda b, g: (0,) * a.ndim)

    out = pl.BlockSpec((None, None, nc, DK), lambda b, g: (b, g, 0, 0))
    out_t = pl.BlockSpec((None, None, DK, nc), lambda b, g: (b, g, 0, 0))
    return pl.pallas_call(
        _compress_kernel,
        grid=(B, G),
        in_specs=[xin, xin] + [full(a) for a in wk] + [full(a) for a in wv],
        out_specs=[out, out_t],
        out_shape=[jax.ShapeDtypeStruct((B, G, nc, DK), bf16),
                   jax.ShapeDtypeStruct((B, G, DK, nc), bf16)],
        compiler_params=_cparams(("parallel", "parallel")),
        name="compress",
    )(kc2, vc2, *wk, *wv)


def _nsa_kernel(q_ref, cos_ref, sin_ref, gate_ref, ovl_ref, kcmp_ref, vcmp_ref,
                ksa_ref, vs_ref, kw_ref, vw_ref, o_ref,
                qa_ref, ms_ref, ls_ref, accs_ref, mw_ref, lw_ref, accw_ref, out_ref,
                sc_ref, mx_ref, sd_ref, md_ref,
                *, tq, n_sel):
    qi = pl.program_id(2)
    q0 = qi * tq
    heads = [slice(r * tq, (r + 1) * tq) for r in range(R)]
    qscale = DK ** -0.5 * LOG2E
    cosf = cos_ref[...]
    sinf = sin_ref[...]
    for r in range(R):
        qr = _rope(q_ref[:, r * DK:(r + 1) * DK], cosf, sinf) * qscale
        qa_ref[0:DK, heads[r]] = qr.T.astype(bf16)
    t1 = q0 + lax.broadcasted_iota(jnp.int32, (1, tq), 1)
    gate_t = _sigmoid(gate_ref[...]).T

    def gate_row(c, r):
        return gate_t[3 * r + c:3 * r + c + 1, :]

    nc = kcmp_ref.shape[0]
    n_idx = lax.broadcasted_iota(jnp.int32, (nc, 1), 0)
    vis = (n_idx * CMP_STRIDE + (CMP_BLOCK - 1)) <= t1
    has_any = t1 >= CMP_BLOCK - 1
    psum = jnp.zeros((nc, tq), f32)
    for r in range(R):
        s = jnp.where(vis, _dot(kcmp_ref[...], qa_ref[0:DK, heads[r]]), NEG)
        e = jnp.exp2(s - jnp.max(s, axis=0, keepdims=True))
        inv = jnp.where(has_any, 1.0 / jnp.sum(e, axis=0, keepdims=True), 0.0)
        out_ref[:, heads[r]] = (gate_row(0, r) * inv) * _dot(vcmp_ref[...], e.astype(bf16))
        psum = psum + e * inv
    p_hi, p_lo = _split2(psum)
    imp = _dot(ovl_ref[...], p_hi) + _dot(ovl_ref[...], p_lo)

    def reset(m_ref, l_ref, acc_ref):
        m_ref[...] = jnp.full(m_ref.shape, NEG, f32)
        l_ref[...] = jnp.zeros(l_ref.shape, f32)
        acc_ref[...] = jnp.zeros(acc_ref.shape, f32)

    def scores_to(s_ref, smax_ref, k_ref, kt, q_rows, mask, r):
        s = _dot(k_ref[pl.ds(pl.multiple_of(kt * tq, tq), tq), :], qa_ref[q_rows, heads[r]])
        if mask is not None:
            s = jnp.where(mask, s, NEG)
        s_ref[:, heads[r]] = s
        smax_ref[:, heads[r]] = jnp.max(s, axis=0, keepdims=True)

    def softmax_pv_head(state, r, s, smax, v_tiles):
        m_ref, l_ref, acc_ref = state
        m_prev = m_ref[:, heads[r]]
        m_new = jnp.maximum(m_prev, smax)
        alpha = jnp.exp2(m_prev - m_new)
        pt = jnp.exp2(s - m_new)
        ptb = pt.astype(bf16)
        pv = _dot(v_tiles[0], ptb[0:tq])
        for i in range(1, len(v_tiles)):
            pv = pv + _dot(v_tiles[i], ptb[i * tq:(i + 1) * tq])
        l_ref[:, heads[r]] = alpha * l_ref[:, heads[r]] + jnp.sum(pt, axis=0, keepdims=True)
        acc_ref[:, heads[r]] = alpha * acc_ref[:, heads[r]] + pv
        m_ref[:, heads[r]] = m_new

    def finish(state, c):
        m_ref, l_ref, acc_ref = state
        for r in range(R):
            out_ref[:, heads[r]] += (gate_row(c, r) * (1.0 / l_ref[:, heads[r]])) * acc_ref[:, heads[r]]

    plain = slice(0, DK)
    aug = slice(0, DK + NSB)

    buf_a = (sd_ref, md_ref)
    buf_b = (sc_ref, mx_ref)

    def tile_scores(buf, k_ref, kt, q_rows, mask):
        for r in range(R):
            scores_to(buf[0], buf[1], k_ref, kt, q_rows, mask, r)

    def tile_softmax(buf, state, v_ref, kt):
        for r in range(R):
            softmax_pv_head(state, r, buf[0][:, heads[r]], buf[1][:, heads[r]], [v_ref[kt]])

    kpos = lax.broadcasted_iota(jnp.int32, (tq, 1), 0)
    causal = q0 + kpos <= t1

    win = (mw_ref, lw_ref, accw_ref)
    reset(*win)
    prev = jnp.maximum(qi - 1, 0)
    inside = ((q0 - tq) + kpos > t1 - WINDOW) & (qi >= 1)
    tile_scores(buf_a, kw_ref, qi, plain, causal)
    tile_scores(buf_b, kw_ref, prev, plain, inside)
    tile_softmax(buf_a, win, vw_ref, qi)
    tile_softmax(buf_b, win, vw_ref, prev)
    finish(win, 2)

    j_idx = lax.broadcasted_iota(jnp.int32, (NSB, 1), 0)
    j_f = j_idx.astype(f32)
    cur = t1 // SLC
    valid = (j_idx * SLC) <= t1
    forced = (j_idx == 0) | (j_idx == cur) | (j_idx == cur - 1)
    score = jnp.where(valid, imp + jnp.where(forced, FORCE_BONUS, 0.0), -jnp.inf)
    sel = jnp.zeros((NSB, tq), f32)
    for _ in range(n_sel):
        mx = jnp.max(score, axis=0, keepdims=True)
        first = jnp.min(jnp.where(score == mx, j_f, float(NSB)), axis=0, keepdims=True)
        hit = j_f == first
        sel = jnp.where(hit, 1.0, sel)
        score = jnp.where(hit, -jnp.inf, score)
    nsel = jnp.where(sel > 0.5, 0.0, NEG).astype(bf16)
    for r in range(R):
        qa_ref[DK:DK + NSB, heads[r]] = nsel

    selst = (ms_ref, ls_ref, accs_ref)
    reset(*selst)
    tile_scores(buf_a, ksa_ref, qi, aug, causal)
    tile_scores(buf_b, ksa_ref, 0, aug, None)
    tile_softmax(buf_a, selst, vs_ref, qi)

    def sel_body(kt, carry):
        nxt = jnp.minimum(kt + 1, jnp.maximum(qi - 1, 0))
        for r in range(R):
            softmax_pv_head(selst, r, sc_ref[:, heads[r]], mx_ref[:, heads[r]], [vs_ref[kt]])
            scores_to(sc_ref, mx_ref, ksa_ref, nxt, aug, None, r)
        return carry

    lax.fori_loop(0, qi, sel_body, 0)
    finish(selst, 1)

    for r in range(R):
        o_ref[:, r * DK:(r + 1) * DK] = out_ref[:, heads[r]].T.astype(bf16)


def _nsa(proj, small, cosf, sinf, ovl_t, kcmp, vcmp_t, ksa, vs_t, kw, vw_t, B, T, tq):
    nq = T // tq
    nc = kcmp.shape[2]
    rows = R * tq
    n_sel = min(N_SELECT, T // SLC)
    assert tq >= WINDOW and nq >= 2

    def per_bg(*shape):
        return pl.BlockSpec((None, None) + shape, lambda b, g, i: (b, g) + (0,) * len(shape))

    tab = pl.BlockSpec((tq, DK), lambda b, g, i: (b * nq + i, 0))
    return pl.pallas_call(
        functools.partial(_nsa_kernel, tq=tq, n_sel=n_sel),
        grid=(B, G, nq),
        in_specs=[
            pl.BlockSpec((tq, R * DK), lambda b, g, i: (b * nq + i, g)),
            tab, tab,
            pl.BlockSpec((tq, 128), lambda b, g, i: (b * nq + i, g)),
            pl.BlockSpec((NSB, nc), lambda b, g, i: (0, 0)),
            per_bg(nc, DK), per_bg(DK, nc),
            per_bg(T, DK + NSB), per_bg(nq, DK, tq), per_bg(T, DK), per_bg(nq, DK, tq),
        ],
        out_specs=pl.BlockSpec((tq, R * DK), lambda b, g, i: (b * nq + i, g)),
        out_shape=jax.ShapeDtypeStruct((B * T, D), bf16),
        scratch_shapes=[pltpu.VMEM((DK + NSB, rows), bf16)]
        + [pltpu.VMEM((1, rows), f32), pltpu.VMEM((1, rows), f32), pltpu.VMEM((DK, rows), f32)] * 2
        + [pltpu.VMEM((DK, rows), f32)]
        + [pltpu.VMEM((tq, rows), f32), pltpu.VMEM((1, rows), f32)] * 2,
        compiler_params=_cparams(("parallel", "parallel", "arbitrary")),
        name="nsa_attention",
    )(proj, cosf, sinf, small, ovl_t, kcmp, vcmp_t, ksa, vs_t, kw, vw_t)


def _ssd_kernel(z_ref, xs_ref, bm_ref, cm_ref, dt_ref, xsh_ref, bmh_ref, cmh_ref,
                wx_ref, bx_ref, wb_ref, bb_ref, wc_ref, bc_ref,
                dtb_ref, alog_ref, dskip_ref, ng_ref, eh_ref,
                o_ref, st_ref, ext_ref, y_ref):
    c = pl.program_id(1)
    L = SSD_L
    HP = 2 * SSD_P
    GW = D // SSD_G

    @pl.when(c == 0)
    def _():
        st_ref[...] = jnp.zeros_like(st_ref)

    def conv_silu(u_ref, h_ref, w_ref, b_ref):
        width = u_ref.shape[1]
        ext_ref[0:HALO, 0:width] = jnp.where(c == 0, 0.0, h_ref[...].astype(f32))
        ext_ref[HALO:HALO + L, 0:width] = u_ref[...].astype(f32)
        y = b_ref[...]
        for j in range(SSD_CONV):
            lo = HALO - (SSD_CONV - 1) + j
            y = y + w_ref[j:j + 1, :] * ext_ref[lo:lo + L, 0:width]
        return y * _sigmoid(y)

    xs = conv_silu(xs_ref, xsh_ref, wx_ref, bx_ref)
    bm = conv_silu(bm_ref, bmh_ref, wb_ref, bb_ref)
    cm = conv_silu(cm_ref, cmh_ref, wc_ref, bc_ref)

    lane = lax.broadcasted_iota(jnp.int32, (1, 128), 1)
    xdt_in = dt_ref[...] + dtb_ref[...]
    dt = jnp.maximum(xdt_in, 0.0) + jnp.log1p(jnp.exp(-jnp.abs(xdt_in)))
    dt = jnp.where(lane < SSD_HEADS, dt, 0.0)
    da = dt * (-jnp.exp(alog_ref[...]))

    row = lax.broadcasted_iota(jnp.int32, (L, L), 0)
    colm = lax.broadcasted_iota(jnp.int32, (L, L), 1)
    causal = colm <= row
    tril = jnp.where(causal, 1.0, 0.0).astype(bf16)
    d_hi, d_mid, d_lo = _split3(da)
    acs = _dot(tril, d_hi) + _dot(tril, d_mid) + _dot(tril, d_lo)
    acs_t = acs.T
    last = acs[L - 1:L, :]
    eacs = jnp.exp(acs)
    dec = jnp.exp(last - acs)
    cdec = jnp.exp(jnp.broadcast_to(last, (8, 128)))

    eh = eh_ref[...]
    dt_x = _dot(dt.astype(bf16), eh)
    eacs_x = _dot(eacs.astype(bf16), eh)
    dec_x = _dot(dec.astype(bf16), eh)
    c_hi, c_lo = _split2(cdec)
    cdec_x = (_dot(c_hi, eh) + _dot(c_lo, eh))[0:1, :]

    xdt = xs * dt_x
    xdt_b = xdt.astype(bf16)
    xdec_b = (xdt * dec_x).astype(bf16)
    lane_hp = lax.broadcasted_iota(jnp.int32, (L, HP), 1)

    for g in range(SSD_G):
        cm_g = cm[:, g * SSD_N:(g + 1) * SSD_N].astype(bf16)
        bm_g = bm[:, g * SSD_N:(g + 1) * SSD_N]
        cb = _dot_nt(cm_g, bm_g.astype(bf16))
        s_in = st_ref[g]
        y_off = _dot(cm_g, s_in.astype(bf16))
        s_new = _dot(bm_g.T.astype(bf16), xdec_b[:, g * GW:(g + 1) * GW])
        st_ref[g] = s_in * cdec_x[:, g * GW:(g + 1) * GW] + s_new
        for pp in range(GW // HP):
            h0 = g * (GW // SSD_P) + 2 * pp
            c0 = g * GW + pp * HP
            ms = []
            for h in (h0, h0 + 1):
                diff = acs[:, h:h + 1] - acs_t[h:h + 1, :]
                ms.append(jnp.where(causal, cb * jnp.exp(jnp.where(causal, diff, NEG)), 0.0))
            lhs = jnp.concatenate(ms, axis=1).astype(bf16)
            xp = xdt_b[:, c0:c0 + HP]
            zero = jnp.zeros_like(xp)
            rhs = jnp.concatenate([jnp.where(lane_hp < SSD_P, xp, zero),
                                   jnp.where(lane_hp >= SSD_P, xp, zero)], axis=0)
            y = (_dot(lhs, rhs) + eacs_x[:, c0:c0 + HP] * y_off[:, pp * HP:(pp + 1) * HP]
                 + dskip_ref[:, c0:c0 + HP] * xs[:, c0:c0 + HP])
            y_ref[:, c0:c0 + HP] = y

    z = z_ref[...].astype(f32)
    y = y_ref[...] * (z * _sigmoid(z))
    for g in range(SSD_G):
        yg = y[:, g * GW:(g + 1) * GW]
        ms = jnp.mean(yg * yg, axis=-1, keepdims=True)
        o_ref[:, g * GW:(g + 1) * GW] = (yg * lax.rsqrt(ms + EPS) * ng_ref[:, g * GW:(g + 1) * GW]).astype(bf16)


def _ssd(proj, small, wts, B, T):
    nch = T // SSD_L
    L = SSD_L

    def main(name, w):
        base = _blk(name, w)
        return pl.BlockSpec((L, w), lambda b, c: (b * nch + c, base))

    def halo(name, w):
        base = _blk(name, w)
        return pl.BlockSpec((HALO, w), lambda b, c: (jnp.maximum((b * nch + c) * (L // HALO) - 1, 0), base))

    def full(a):
        return pl.BlockSpec(a.shape, lambda b, c: (0,) * a.ndim)

    gn = SSD_G * SSD_N
    return pl.pallas_call(
        _ssd_kernel,
        grid=(B, nch),
        in_specs=[main('s_z', D), main('xs', D), main('bm', gn), main('cm', gn),
                  pl.BlockSpec((L, 128), lambda b, c: (b * nch + c, G)),
                  halo('xs', D), halo('bm', gn), halo('cm', gn)] + [full(a) for a in wts],
        out_specs=pl.BlockSpec((L, D), lambda b, c: (b * nch + c, 0)),
        out_shape=jax.ShapeDtypeStruct((B * T, D), bf16),
        scratch_shapes=[pltpu.VMEM((SSD_G, SSD_N, D // SSD_G), f32),
                        pltpu.VMEM((HALO + L, D), f32),
                        pltpu.VMEM((L, D), f32)],
        compiler_params=_cparams(("parallel", "arbitrary")),
        name="ssd",
    )(proj, proj, proj, proj, small, proj, proj, proj, *wts)


def _merge_kernel(ya_ref, cb_ref, cc_ref, cu_ref, cch_ref, cuh_ref, yc_ref,
                  g0_ref, g1_ref, g2_ref, x_ref, cw_ref, wo_ref, o_ref, ext_ref, *, tm, tiles_per_seq):
    first = (pl.program_id(0) % tiles_per_seq) == 0
    up = lambda ref: ref[...].astype(f32)
    ext_ref[0:HALO, :] = jnp.where(first, 0.0, up(cch_ref) * up(cuh_ref))
    ext_ref[HALO:HALO + tm, :] = up(cc_ref) * up(cu_ref)
    lo = HALO - (CONV_W - 1)
    conv = cw_ref[0:1, :] * ext_ref[lo:lo + tm, :]
    for j in range(1, CONV_W):
        conv = conv + cw_ref[j:j + 1, :] * ext_ref[lo + j:lo + j + tm, :]
    yb = up(cb_ref) * conv
    merged = (_sigmoid(up(g0_ref)) * up(ya_ref) + _sigmoid(up(g1_ref)) * yb
              + _sigmoid(up(g2_ref)) * up(yc_ref))
    o_ref[...] = x_ref[...] + _dot(merged.astype(bf16), wo_ref[...])


def _merge(ya, yc, proj, x2, cw, wo, T, tm):
    M = x2.shape[0]

    def col(name):
        base = _blk(name, D)
        return pl.BlockSpec((tm, D), lambda i: (i, base))

    def halo(name):
        base = _blk(name, D)
        return pl.BlockSpec((HALO, D), lambda i: (jnp.maximum(i * (tm // HALO) - 1, 0), base))

    row = pl.BlockSpec((tm, D), lambda i: (i, 0))
    return pl.pallas_call(
        functools.partial(_merge_kernel, tm=tm, tiles_per_seq=T // tm),
        grid=(M // tm,),
        in_specs=[row, col('cb'), col('cc'), col('cu'), halo('cc'), halo('cu'), row,
                  col('gm0'), col('gm1'), col('gm2'), row,
                  pl.BlockSpec((CONV_W, D), lambda i: (0, 0)),
                  pl.BlockSpec((D, D), lambda i: (0, 0))],
        out_specs=row,
        out_shape=jax.ShapeDtypeStruct((M, D), f32),
        scratch_shapes=[pltpu.VMEM((HALO + tm, D), f32)],
        compiler_params=_cparams(("parallel",)),
        name="merge_oproj",
    )(ya, proj, proj, proj, proj, proj, yc, proj, proj, proj, x2, cw, wo)


def _ffn_kernel(x_ref, g_ref, wu_ref, wd_ref, o_ref, h_ref):
    @pl.when(pl.program_id(1) == 0)
    def _():
        x = x_ref[...]
        h_ref[...] = _rms(x, g_ref[...]).astype(bf16)
        o_ref[...] = x

    u = jnp.maximum(_dot(h_ref[...], wu_ref[...]), 0.0)
    o_ref[...] += _dot((u * u).astype(bf16), wd_ref[...])


def _ffn(x2, g, wu, wd, tm, tf):
    M = x2.shape[0]
    return pl.pallas_call(
        _ffn_kernel,
        grid=(M // tm, D_FF // tf),
        in_specs=[pl.BlockSpec((tm, D), lambda i, j: (i, 0)),
                  pl.BlockSpec((1, D), lambda i, j: (0, 0)),
                  pl.BlockSpec((D, tf), lambda i, j: (0, j)),
                  pl.BlockSpec((tf, D), lambda i, j: (j, 0))],
        out_specs=pl.BlockSpec((tm, D), lambda i, j: (i, 0)),
        out_shape=jax.ShapeDtypeStruct((M, D), f32),
        scratch_shapes=[pltpu.VMEM((tm, D), bf16)],
        compiler_params=_cparams(("parallel", "arbitrary")),
        name="ffn",
    )(x2, g, wu, wd)


def _ple_kernel(x_ref, p_ref, g_ref, wp_ref, wg_ref, gf_ref, o_ref, *, final):
    x = x_ref[...]
    gate = _sigmoid(_dot(_rms(x, g_ref[...]).astype(bf16), wg_ref[...]))
    y = x + _dot(p_ref[...].astype(bf16), wp_ref[...]) * gate
    if final:
        y = _rms(y, gf_ref[...])
    o_ref[...] = y


def _ple(x2, p2, layer, g, wp, wg, gf, tm, final):
    M = x2.shape[0]
    base = layer * (M // tm)
    row = pl.BlockSpec((tm, D), lambda i: (i, 0))
    vec = pl.BlockSpec((1, D), lambda i: (0, 0))
    return pl.pallas_call(
        functools.partial(_ple_kernel, final=final),
        grid=(M // tm,),
        in_specs=[row, pl.BlockSpec((tm, PLE), lambda i: (base + i, 0)), vec,
                  pl.BlockSpec((PLE, D), lambda i: (0, 0)),
                  pl.BlockSpec((D, D), lambda i: (0, 0)), vec],
        out_specs=row,
        out_shape=jax.ShapeDtypeStruct((M, D), f32),
        compiler_params=_cparams(("parallel",)),
        name="ple",
    )(x2, p2, g, wp, wg, gf)


def _source_runs():
    runs = []
    for n in _ORDER:
        a, wd = _SRC[n]
        if runs and runs[-1][1] == a:
            runs[-1][1] = a + wd
        else:
            runs.append([a, a + wd])
    return runs


def _prep_w_in(w):
    depth = w.shape[0]
    main = jnp.concatenate([w[:, :, a:b] for a, b in _source_runs()], axis=2)
    a, wd = _SRC['g_nsa']
    gn = w[:, :, a:a + wd].reshape(depth, D, G, R * 3)
    gn = jnp.pad(gn, ((0, 0), (0, 0), (0, 0), (0, 128 - R * 3))).reshape(depth, D, G * 128)
    a, wd = _SRC['s_dt']
    dt = jnp.pad(w[:, :, a:a + wd], ((0, 0), (0, 0), (0, 128 - wd)))
    return main.astype(bf16), jnp.concatenate([gn, dt], axis=2).astype(bf16)


def _overlap_matrix(nc):
    i = np.arange(nc)[:, None]
    j = np.arange(NSB)[None, :]
    ovl = (i * CMP_STRIDE < j * SLC + SLC) & (i * CMP_STRIDE + CMP_BLOCK > j * SLC)
    return jnp.asarray(ovl.T.astype(np.float32), dtype=bf16)


def _head_expand():
    h = np.arange(128)[:, None]
    ch = np.arange(D)[None, :]
    return jnp.asarray((ch // SSD_P == h).astype(np.float32), dtype=bf16)


def kernel(x, p, positions, g_mix, w_in, nsa_pe_k, nsa_pe_v, phi_k_w1, phi_k_b1, phi_k_w2, phi_k_b2,
           phi_v_w1, phi_v_b1, phi_v_w2, phi_v_b2, sconv_w, ssd_conv_w, ssd_conv_b, ssd_dt_bias,
           ssd_a_log, ssd_d, ssd_norm_g, w_o, g_mlp, w_up, w_down, g_ple, w_ple, w_ple_gate, g_final):
    B, T, _ = x.shape
    depth = w_in.shape[0]
    M = B * T
    assert T % 256 == 0 and T // SLC <= NSB
    tq = 512
    tm_proj = 1024 if M % 1024 == 0 else 256
    tm = 512 if T % 512 == 0 else 256
    nc = T // CMP_STRIDE

    inv_freq = 1.0 / (10000.0 ** (jnp.arange(0, DK, 2, dtype=f32) / DK))
    ang = positions.astype(f32)[..., None] * inv_freq
    cosf = jnp.concatenate([jnp.cos(ang), jnp.cos(ang)], axis=-1).reshape(M, DK)
    sinf = jnp.concatenate([-jnp.sin(ang), jnp.sin(ang)], axis=-1).reshape(M, DK)
    ovl = _overlap_matrix(nc)
    eh = _head_expand()

    def vec(a, n=None):
        a = a.reshape(1, -1).astype(f32)
        return a if n is None else jnp.pad(a, ((0, 0), (0, n - a.shape[1])))

    x2 = x.reshape(M, D)
    p2 = p.reshape(depth * M, PLE)
    w_main, w_small = _prep_w_in(w_in)
    for i in range(depth):
        proj, small = _proj(x2, vec(g_mix[i]), w_main[i], w_small[i], tm_proj, 2048)

        kc, vc, ksa, vs, kw, vw = _prep(proj, cosf, sinf, B, T, tq)

        def phi(pe, w1, b1, w2, b2):
            pe8 = jnp.broadcast_to(pe.reshape(1, -1), (8, CMP_BLOCK * DK)).astype(bf16)
            return (pe8, w1.astype(bf16), vec(b1), w2.astype(bf16), vec(b2))

        kcmp, vcmp = _compress(kc.reshape(B, G, nc, CMP_STRIDE * DK), vc.reshape(B, G, nc, CMP_STRIDE * DK),
                               phi(nsa_pe_k[i], phi_k_w1[i], phi_k_b1[i], phi_k_w2[i], phi_k_b2[i]),
                               phi(nsa_pe_v[i], phi_v_w1[i], phi_v_b1[i], phi_v_w2[i], phi_v_b2[i]))
        ya = _nsa(proj, small, cosf, sinf, ovl, kcmp, vcmp, ksa, vs, kw, vw, B, T, tq)

        cw, cbias = ssd_conv_w[i], ssd_conv_b[i]
        gn = SSD_G * SSD_N
        ssd_w = (cw[:, :D], vec(cbias[:D]), cw[:, D:D + gn], vec(cbias[D:D + gn]),
                 cw[:, D + gn:], vec(cbias[D + gn:]),
                 vec(ssd_dt_bias[i], 128), vec(ssd_a_log[i], 128),
                 vec(jnp.repeat(ssd_d[i], SSD_P)), vec(ssd_norm_g[i]), eh)
        yc = _ssd(proj, small, ssd_w, B, T)

        x2 = _merge(ya, yc, proj, x2, sconv_w[i], w_o[i].astype(bf16), T, 256)
        x2 = _ffn(x2, vec(g_mlp[i]), w_up[i].astype(bf16), w_down[i].astype(bf16), tm_proj, 512)
        x2 = _ple(x2, p2, i, vec(g_ple[i]), w_ple[i].astype(bf16),
                  w_ple_gate[i].astype(bf16), vec(g_final), tm, final=(i == depth - 1))
    return x2.reshape(B, T, D)
```

```python
import functools

import numpy as np
import jax
import jax.numpy as jnp
from jax import lax
from jax.experimental import pallas as pl
from jax.experimental.pallas import tpu as pltpu

f32 = jnp.float32
bf16 = jnp.bfloat16

D = 2048
N_HEADS = 16
DK = 128
G = 4
R = N_HEADS // G
CMP_BLOCK = 32
CMP_STRIDE = 16
SLC = 64
N_SELECT = 16
WINDOW = 512
FORCE_BONUS = 1.0e4
CONV_W = 3
SSD_HEADS = 32
SSD_P = 64
SSD_G = 4
SSD_N = 128
SSD_CONV = 4
SSD_L = 128
D_FF = 4 * D
PLE = 256
EPS = 1e-6
NSB = 128
HALO = 16
NEG = -1e30
LOG2E = 1.4426950408889634

VMEM_LIMIT = 56 * 1024 * 1024

_SRC = {
    'q': (0, 2048), 'k_c': (2048, 512), 'v_c': (2560, 512), 'k_s': (3072, 512), 'v_s': (3584, 512),
    'k_w': (4096, 512), 'v_w': (4608, 512), 'g_nsa': (5120, 48),
    'cb': (5168, 2048), 'cc': (7216, 2048), 'cu': (9264, 2048),
    's_z': (11312, 2048), 'xs': (13360, 2048), 'bm': (15408, 512), 'cm': (15920, 512),
    's_dt': (16432, 32), 'gm0': (16464, 2048), 'gm1': (18512, 2048), 'gm2': (20560, 2048),
}
_ORDER = ['q', 'cb', 'cc', 'cu', 's_z', 'xs', 'gm0', 'gm1', 'gm2',
          'k_c', 'v_c', 'k_s', 'v_s', 'k_w', 'v_w', 'bm', 'cm']
_OFF = {}
_o = 0
for _n in _ORDER:
    _OFF[_n] = _o
    _o += _SRC[_n][1]
N_MAIN = _o
N_SMALL = G * 128 + 128


def _blk(name, width):
    off = _OFF[name]
    assert off % width == 0
    return off // width


def _cparams(sem):
    return pltpu.CompilerParams(dimension_semantics=sem, vmem_limit_bytes=VMEM_LIMIT)


def _dot(a, b):
    return jnp.dot(a, b, preferred_element_type=f32)


def _dot_nt(a, b):
    return lax.dot_general(a, b, (((1,), (1,)), ((), ())), preferred_element_type=f32)


def _sigmoid(x):
    return jax.nn.sigmoid(x)


def _split2(x):
    hi = x.astype(bf16)
    lo = (x - hi.astype(f32)).astype(bf16)
    return hi, lo


def _split3(x):
    hi = x.astype(bf16)
    r1 = x - hi.astype(f32)
    mid = r1.astype(bf16)
    lo = (r1 - mid.astype(f32)).astype(bf16)
    return hi, mid, lo


def _rms(x, g):
    ms = jnp.mean(x * x, axis=-1, keepdims=True)
    return x * lax.rsqrt(ms + EPS) * g


def _proj_kernel(x_ref, g_ref, w_ref, ws_ref, o_ref, os_ref, h_ref):
    @pl.when(pl.program_id(1) == 0)
    def _():
        h = _rms(x_ref[...], g_ref[...]).astype(bf16)
        h_ref[...] = h
        os_ref[...] = _dot(h, ws_ref[...])

    o_ref[...] = _dot(h_ref[...], w_ref[...]).astype(bf16)


def _proj(x2, g, w_main, w_small, tm, tn):
    M = x2.shape[0]
    return pl.pallas_call(
        _proj_kernel,
        grid=(M // tm, N_MAIN // tn),
        in_specs=[
            pl.BlockSpec((tm, D), lambda i, j: (i, 0), pipeline_mode=pl.Buffered(1)),
            pl.BlockSpec((1, D), lambda i, j: (0, 0)),
            pl.BlockSpec((D, tn), lambda i, j: (0, j)),
            pl.BlockSpec((D, N_SMALL), lambda i, j: (0, 0), pipeline_mode=pl.Buffered(1)),
        ],
        out_specs=[
            pl.BlockSpec((tm, tn), lambda i, j: (i, j)),
            pl.BlockSpec((tm, N_SMALL), lambda i, j: (i, 0)),
        ],
        out_shape=[jax.ShapeDtypeStruct((M, N_MAIN), bf16),
                   jax.ShapeDtypeStruct((M, N_SMALL), f32)],
        scratch_shapes=[pltpu.VMEM((tm, D), bf16)],
        compiler_params=_cparams(("parallel", "arbitrary")),
        name="proj",
    )(x2, g, w_main, w_small)


def _rope(x, cosf, sinf):
    x = x.astype(f32)
    return x * cosf + pltpu.roll(x, DK // 2, axis=1) * sinf


def _prep_kernel(kc_ref, vc_ref, ks_ref, vs_ref, kw_ref, vw_ref, cos_ref, sin_ref,
                 kco_ref, vco_ref, ksa_ref, vso_ref, kwo_ref, vwo_ref, *, tt):
    cosf = cos_ref[...]
    sinf = sin_ref[...]
    kco_ref[...] = _rope(kc_ref[...], cosf, sinf).astype(bf16)
    vco_ref[...] = vc_ref[...].astype(bf16)
    ksa_ref[:, 0:DK] = _rope(ks_ref[...], cosf, sinf).astype(bf16)
    t = pl.program_id(2) * tt + lax.broadcasted_iota(jnp.int32, (tt, NSB), 0)
    j = lax.broadcasted_iota(jnp.int32, (tt, NSB), 1)
    ksa_ref[:, DK:DK + NSB] = jnp.where(t // SLC == j, 1.0, 0.0).astype(bf16)
    vso_ref[...] = vs_ref[...].astype(f32).T.astype(bf16)
    kwo_ref[...] = _rope(kw_ref[...], cosf, sinf).astype(bf16)
    vwo_ref[...] = vw_ref[...].astype(f32).T.astype(bf16)


def _prep(proj, cosf, sinf, B, T, tt):
    nt = T // tt

    def col(name):
        base = _blk(name, DK)
        return pl.BlockSpec((tt, DK), lambda b, g, i: (b * nt + i, base + g))

    tab = pl.BlockSpec((tt, DK), lambda b, g, i: (b * nt + i, 0))

    def out(w):
        return pl.BlockSpec((None, None, tt, w), lambda b, g, i: (b, g, i, 0))

    out_t = pl.BlockSpec((None, None, None, DK, tt), lambda b, g, i: (b, g, i, 0, 0))
    return pl.pallas_call(
        functools.partial(_prep_kernel, tt=tt),
        grid=(B, G, nt),
        in_specs=[col('k_c'), col('v_c'), col('k_s'), col('v_s'), col('k_w'), col('v_w'), tab, tab],
        out_specs=[out(DK), out(DK), out(DK + NSB), out_t, out(DK), out_t],
        out_shape=[jax.ShapeDtypeStruct((B, G, T, DK), bf16),
                   jax.ShapeDtypeStruct((B, G, T, DK), bf16),
                   jax.ShapeDtypeStruct((B, G, T, DK + NSB), bf16),
                   jax.ShapeDtypeStruct((B, G, nt, DK, tt), bf16),
                   jax.ShapeDtypeStruct((B, G, T, DK), bf16),
                   jax.ShapeDtypeStruct((B, G, nt, DK, tt), bf16)],
        compiler_params=_cparams(("parallel", "parallel", "parallel")),
        name="kv_prep",
    )(proj, proj, proj, proj, proj, proj, cosf, sinf)


def _compress_one(x_ref, pe_ref, w1_ref, b1_ref, w2_ref, b2_ref, o_ref, transpose):
    nc = x_ref.shape[0]
    half = CMP_STRIDE * DK
    x = x_ref[...]
    first = _dot(x, w1_ref[0:half, :])
    second = _dot(x, w1_ref[half:2 * half, :])
    second = pltpu.roll(second, nc - 1, axis=0)
    pe = _dot(pe_ref[...], w1_ref[...])[0:1, :]
    h = first + second + pe + b1_ref[...]
    h = h * _sigmoid(h)
    y = _dot(h.astype(bf16), w2_ref[...]) + b2_ref[...]
    o_ref[...] = (y.T if transpose else y).astype(bf16)


def _compress_kernel(k_ref, v_ref, pek, w1k, b1k, w2k, b2k, pev, w1v, b1v, w2v, b2v, ko_ref, vo_ref):
    _compress_one(k_ref, pek, w1k, b1k, w2k, b2k, ko_ref, False)
    _compress_one(v_ref, pev, w1v, b1v, w2v, b2v, vo_ref, True)


def _compress(kc2, vc2, wk, wv):
    B, _, nc, width = kc2.shape
    xin = pl.BlockSpec((None, None, nc, width), lambda b, g: (b, g, 0, 0))

    def full(a):
        return pl.BlockSpec(a.shape, lambda b, g: (0,) * a.ndim)

    out = pl.BlockSpec((None, None, nc, DK), lambda b, g: (b, g, 0, 0))
    out_t = pl.BlockSpec((None, None, DK, nc), lambda b, g: (b, g, 0, 0))
    return pl.pallas_call(
        _compress_kernel,
        grid=(B, G),
        in_specs=[xin, xin] + [full(a) for a in wk] + [full(a) for a in wv],
        out_specs=[out, out_t],
        out_shape=[jax.ShapeDtypeStruct((B, G, nc, DK), bf16),
                   jax.ShapeDtypeStruct((B, G, DK, nc), bf16)],
        compiler_params=_cparams(("parallel", "parallel")),
        name="compress",
    )(kc2, vc2, *wk, *wv)


def _nsa_kernel(q_ref, cos_ref, sin_ref, gate_ref, ovl_ref, kcmp_ref, vcmp_ref,
                ksa_ref, vs_ref, kw_ref, vw_ref, o_ref,
                qa_ref, ms_ref, ls_ref, accs_ref, mw_ref, lw_ref, accw_ref, out_ref,
                sc_ref, mx_ref, sd_ref, md_ref,
                *, tq, n_sel):
    qi = pl.program_id(2)
    q0 = qi * tq
    heads = [slice(r * tq, (r + 1) * tq) for r in range(R)]
    qscale = DK ** -0.5 * LOG2E
    cosf = cos_ref[...]
    sinf = sin_ref[...]
    for r in range(R):
        qr = _rope(q_ref[:, r * DK:(r + 1) * DK], cosf, sinf) * qscale
        qa_ref[0:DK, heads[r]] = qr.T.astype(bf16)
    t1 = q0 + lax.broadcasted_iota(jnp.int32, (1, tq), 1)
    gate_t = _sigmoid(gate_ref[...]).T

    def gate_row(c, r):
        return gate_t[3 * r + c:3 * r + c + 1, :]

    nc = kcmp_ref.shape[0]
    n_idx = lax.broadcasted_iota(jnp.int32, (nc, 1), 0)
    vis = (n_idx * CMP_STRIDE + (CMP_BLOCK - 1)) <= t1
    has_any = t1 >= CMP_BLOCK - 1
    psum = jnp.zeros((nc, tq), f32)
    for r in range(R):
        s = jnp.where(vis, _dot(kcmp_ref[...], qa_ref[0:DK, heads[r]]), NEG)
        e = jnp.exp2(s - jnp.max(s, axis=0, keepdims=True))
        inv = jnp.where(has_any, 1.0 / jnp.sum(e, axis=0, keepdims=True), 0.0)
        out_ref[:, heads[r]] = (gate_row(0, r) * inv) * _dot(vcmp_ref[...], e.astype(bf16))
        psum = psum + e * inv
    p_hi, p_lo = _split2(psum)
    imp = _dot(ovl_ref[...], p_hi) + _dot(ovl_ref[...], p_lo)

    def reset(m_ref, l_ref, acc_ref):
        m_ref[...] = jnp.full(m_ref.shape, NEG, f32)
        l_ref[...] = jnp.zeros(l_ref.shape, f32)
        acc_ref[...] = jnp.zeros(acc_ref.shape, f32)

    def scores_to(s_ref, smax_ref, k_ref, kt, q_rows, mask, r):
        s = _dot(k_ref[pl.ds(pl.multiple_of(kt * tq, tq), tq), :], qa_ref[q_rows, heads[r]])
        if mask is not None:
            s = jnp.where(mask, s, NEG)
        s_ref[:, heads[r]] = s
        smax_ref[:, heads[r]] = jnp.max(s, axis=0, keepdims=True)

    def softmax_pv_head(state, r, s, smax, v_tiles):
        m_ref, l_ref, acc_ref = state
        m_prev = m_ref[:, heads[r]]
        m_new = jnp.maximum(m_prev, smax)
        alpha = jnp.exp2(m_prev - m_new)
        pt = jnp.exp2(s - m_new)
        ptb = pt.astype(bf16)
        pv = _dot(v_tiles[0], ptb[0:tq])
        for i in range(1, len(v_tiles)):
            pv = pv + _dot(v_tiles[i], ptb[i * tq:(i + 1) * tq])
        l_ref[:, heads[r]] = alpha * l_ref[:, heads[r]] + jnp.sum(pt, axis=0, keepdims=True)
        acc_ref[:, heads[r]] = alpha * acc_ref[:, heads[r]] + pv
        m_ref[:, heads[r]] = m_new

    def finish(state, c):
        m_ref, l_ref, acc_ref = state
        for r in range(R):
            out_ref[:, heads[r]] += (gate_row(c, r) * (1.0 / l_ref[:, heads[r]])) * acc_ref[:, heads[r]]

    plain = slice(0, DK)
    aug = slice(0, DK + NSB)

    buf_a = (sd_ref, md_ref)
    buf_b = (sc_ref, mx_ref)

    def tile_scores(buf, k_ref, kt, q_rows, mask):
        for r in range(R):
            scores_to(buf[0], buf[1], k_ref, kt, q_rows, mask, r)

    def tile_softmax(buf, state, v_ref, kt):
        for r in range(R):
            softmax_pv_head(state, r, buf[0][:, heads[r]], buf[1][:, heads[r]], [v_ref[kt]])

    kpos = lax.broadcasted_iota(jnp.int32, (tq, 1), 0)
    causal = q0 + kpos <= t1

    win = (mw_ref, lw_ref, accw_ref)
    reset(*win)
    prev = jnp.maximum(qi - 1, 0)
    inside = ((q0 - tq) + kpos > t1 - WINDOW) & (qi >= 1)
    tile_scores(buf_a, kw_ref, qi, plain, causal)
    tile_scores(buf_b, kw_ref, prev, plain, inside)
    tile_softmax(buf_a, win, vw_ref, qi)
    tile_softmax(buf_b, win, vw_ref, prev)
    finish(win, 2)

    j_idx = lax.broadcasted_iota(jnp.int32, (NSB, 1), 0)
    j_f = j_idx.astype(f32)
    cur = t1 // SLC
    valid = (j_idx * SLC) <= t1
    forced = (j_idx == 0) | (j_idx == cur) | (j_idx == cur - 1)
    score = jnp.where(valid, imp + jnp.where(forced, FORCE_BONUS, 0.0), -jnp.inf)
    sel = jnp.zeros((NSB, tq), f32)
    for _ in range(n_sel):
        mx = jnp.max(score, axis=0, keepdims=True)
        first = jnp.min(jnp.where(score == mx, j_f, float(NSB)), axis=0, keepdims=True)
        hit = j_f == first
        sel = jnp.where(hit, 1.0, sel)
        score = jnp.where(hit, -jnp.inf, score)
    nsel = jnp.where(sel > 0.5, 0.0, NEG).astype(bf16)
    for r in range(R):
        qa_ref[DK:DK + NSB, heads[r]] = nsel

    selst = (ms_ref, ls_ref, accs_ref)
    reset(*selst)
    tile_scores(buf_a, ksa_ref, qi, aug, causal)
    tile_scores(buf_b, ksa_ref, 0, aug, None)
    tile_softmax(buf_a, selst, vs_ref, qi)

    def sel_body(kt, carry):
        nxt = jnp.minimum(kt + 1, jnp.maximum(qi - 1, 0))
        for r in range(R):
            softmax_pv_head(selst, r, sc_ref[:, heads[r]], mx_ref[:, heads[r]], [vs_ref[kt]])
            scores_to(sc_ref, mx_ref, ksa_ref, nxt, aug, None, r)
        return carry

    lax.fori_loop(0, qi, sel_body, 0)
    finish(selst, 1)

    for r in range(R):
        o_ref[:, r * DK:(r + 1) * DK] = out_ref[:, heads[r]].T.astype(bf16)


def _nsa(proj, small, cosf, sinf, ovl_t, kcmp, vcmp_t, ksa, vs_t, kw, vw_t, B, T, tq):
    nq = T // tq
    nc = kcmp.shape[2]
    rows = R * tq
    n_sel = min(N_SELECT, T // SLC)
    assert tq >= WINDOW and nq >= 2

    def per_bg(*shape):
        return pl.BlockSpec((None, None) + shape, lambda b, g, i: (b, g) + (0,) * len(shape))

    tab = pl.BlockSpec((tq, DK), lambda b, g, i: (b * nq + i, 0))
    return pl.pallas_call(
        functools.partial(_nsa_kernel, tq=tq, n_sel=n_sel),
        grid=(B, G, nq),
        in_specs=[
            pl.BlockSpec((tq, R * DK), lambda b, g, i: (b * nq + i, g)),
            tab, tab,
            pl.BlockSpec((tq, 128), lambda b, g, i: (b * nq + i, g)),
            pl.BlockSpec((NSB, nc), lambda b, g, i: (0, 0)),
            per_bg(nc, DK), per_bg(DK, nc),
            per_bg(T, DK + NSB), per_bg(nq, DK, tq), per_bg(T, DK), per_bg(nq, DK, tq),
        ],
        out_specs=pl.BlockSpec((tq, R * DK), lambda b, g, i: (b * nq + i, g)),
        out_shape=jax.ShapeDtypeStruct((B * T, D), bf16),
        scratch_shapes=[pltpu.VMEM((DK + NSB, rows), bf16)]
        + [pltpu.VMEM((1, rows), f32), pltpu.VMEM((1, rows), f32), pltpu.VMEM((DK, rows), f32)] * 2
        + [pltpu.VMEM((DK, rows), f32)]
        + [pltpu.VMEM((tq, rows), f32), pltpu.VMEM((1, rows), f32)] * 2,
        compiler_params=_cparams(("parallel", "parallel", "arbitrary")),
        name="nsa_attention",
    )(proj, cosf, sinf, small, ovl_t, kcmp, vcmp_t, ksa, vs_t, kw, vw_t)


def _ssd_kernel(z_ref, xs_ref, bm_ref, cm_ref, dt_ref,
                wx_ref, bx_ref, wb_ref, bb_ref, wc_ref, bc_ref,
                dtb_ref, alog_ref, dskip_ref, ng_ref, eh_ref, shift_ref,
                o_ref, st_ref, prev_ref, y_ref):
    c = pl.program_id(1)
    L = SSD_L
    HP = 2 * SSD_P
    GW = D // SSD_G

    @pl.when(c == 0)
    def _():
        st_ref[...] = jnp.zeros_like(st_ref)
        prev_ref[...] = jnp.zeros_like(prev_ref)

    def conv_silu(u_ref, col, w_ref, b_ref):
        width = u_ref.shape[1]
        cur = u_ref[...]
        ext = jnp.concatenate([prev_ref[:, col:col + width], cur], axis=0)
        sh = _dot(shift_ref[...], ext)
        y = b_ref[...] + w_ref[SSD_CONV - 1:SSD_CONV, :] * cur.astype(f32)
        for j in range(SSD_CONV - 1):
            y = y + w_ref[j:j + 1, :] * sh[j * L:(j + 1) * L]
        prev_ref[:, col:col + width] = cur
        return y * _sigmoid(y)

    xs = conv_silu(xs_ref, 0, wx_ref, bx_ref)
    bm = conv_silu(bm_ref, D, wb_ref, bb_ref)
    cm = conv_silu(cm_ref, D + SSD_G * SSD_N, wc_ref, bc_ref)

    lane = lax.broadcasted_iota(jnp.int32, (1, 128), 1)
    xdt_in = dt_ref[...] + dtb_ref[...]
    dt = jnp.maximum(xdt_in, 0.0) + jnp.log1p(jnp.exp(-jnp.abs(xdt_in)))
    dt = jnp.where(lane < SSD_HEADS, dt, 0.0)
    da = dt * (-jnp.exp(alog_ref[...]))

    row = lax.broadcasted_iota(jnp.int32, (L, L), 0)
    colm = lax.broadcasted_iota(jnp.int32, (L, L), 1)
    causal = colm <= row
    tril = jnp.where(causal, 1.0, 0.0).astype(bf16)
    d_hi, d_mid, d_lo = _split3(da)
    acs = _dot(tril, d_hi) + _dot(tril, d_mid) + _dot(tril, d_lo)
    acs_t = acs.T
    last = acs[L - 1:L, :]
    eacs = jnp.exp(acs)
    dec = jnp.exp(last - acs)
    cdec = jnp.exp(jnp.broadcast_to(last, (8, 128)))

    eh = eh_ref[...]
    dt_x = _dot(dt.astype(bf16), eh)
    eacs_x = _dot(eacs.astype(bf16), eh)
    dec_x = _dot(dec.astype(bf16), eh)
    c_hi, c_lo = _split2(cdec)
    cdec_x = (_dot(c_hi, eh) + _dot(c_lo, eh))[0:1, :]

    xdt = xs * dt_x
    xdt_b = xdt.astype(bf16)
    xdec_b = (xdt * dec_x).astype(bf16)
    lane_hp = lax.broadcasted_iota(jnp.int32, (L, HP), 1)

    for g in range(SSD_G):
        cm_g = cm[:, g * SSD_N:(g + 1) * SSD_N].astype(bf16)
        bm_g = bm[:, g * SSD_N:(g + 1) * SSD_N]
        cb = _dot_nt(cm_g, bm_g.astype(bf16))
        s_in = st_ref[g]
        y_off = _dot(cm_g, s_in.astype(bf16))
        s_new = _dot(bm_g.T.astype(bf16), xdec_b[:, g * GW:(g + 1) * GW])
        st_ref[g] = s_in * cdec_x[:, g * GW:(g + 1) * GW] + s_new
        for pp in range(GW // HP):
            h0 = g * (GW // SSD_P) + 2 * pp
            c0 = g * GW + pp * HP
            ms = []
            for h in (h0, h0 + 1):
                diff = acs[:, h:h + 1] - acs_t[h:h + 1, :]
                ms.append(jnp.where(causal, cb * jnp.exp(jnp.where(causal, diff, NEG)), 0.0))
            lhs = jnp.concatenate(ms, axis=1).astype(bf16)
            xp = xdt_b[:, c0:c0 + HP]
            zero = jnp.zeros_like(xp)
            rhs = jnp.concatenate([jnp.where(lane_hp < SSD_P, xp, zero),
                                   jnp.where(lane_hp >= SSD_P, xp, zero)], axis=0)
            y = (_dot(lhs, rhs) + eacs_x[:, c0:c0 + HP] * y_off[:, pp * HP:(pp + 1) * HP]
                 + dskip_ref[:, c0:c0 + HP] * xs[:, c0:c0 + HP])
            y_ref[:, c0:c0 + HP] = y

    z = z_ref[...].astype(f32)
    y = y_ref[...] * (z * _sigmoid(z))
    for g in range(SSD_G):
        yg = y[:, g * GW:(g + 1) * GW]
        ms = jnp.mean(yg * yg, axis=-1, keepdims=True)
        o_ref[:, g * GW:(g + 1) * GW] = (yg * lax.rsqrt(ms + EPS) * ng_ref[:, g * GW:(g + 1) * GW]).astype(bf16)


def _ssd(proj, small, wts, B, T):
    nch = T // SSD_L
    L = SSD_L

    def main(name, w):
        base = _blk(name, w)
        return pl.BlockSpec((L, w), lambda b, c: (b * nch + c, base))

    def full(a):
        return pl.BlockSpec(a.shape, lambda b, c: (0,) * a.ndim)

    gn = SSD_G * SSD_N
    return pl.pallas_call(
        _ssd_kernel,
        grid=(B, nch),
        in_specs=[main('s_z', D), main('xs', D), main('bm', gn), main('cm', gn),
                  pl.BlockSpec((L, 128), lambda b, c: (b * nch + c, G))] + [full(a) for a in wts],
        out_specs=pl.BlockSpec((L, D), lambda b, c: (b * nch + c, 0)),
        out_shape=jax.ShapeDtypeStruct((B * T, D), bf16),
        scratch_shapes=[pltpu.VMEM((SSD_G, SSD_N, D // SSD_G), f32),
                        pltpu.VMEM((L, D + 2 * gn), bf16),
                        pltpu.VMEM((L, D), f32)],
        compiler_params=_cparams(("parallel", "arbitrary")),
        name="ssd",
    )(proj, proj, proj, proj, small, *wts)


def _merge_kernel(ya_ref, cb_ref, cc_ref, cu_ref, cch_ref, cuh_ref, yc_ref,
                  g0_ref, g1_ref, g2_ref, x_ref, cw_ref, wo_ref, o_ref, ext_ref, *, tm, tiles_per_seq):
    first = (pl.program_id(0) % tiles_per_seq) == 0
    up = lambda ref: ref[...].astype(f32)
    ext_ref[0:HALO, :] = jnp.where(first, 0.0, up(cch_ref) * up(cuh_ref))
    ext_ref[HALO:HALO + tm, :] = up(cc_ref) * up(cu_ref)
    lo = HALO - (CONV_W - 1)
    conv = cw_ref[0:1, :] * ext_ref[lo:lo + tm, :]
    for j in range(1, CONV_W):
        conv = conv + cw_ref[j:j + 1, :] * ext_ref[lo + j:lo + j + tm, :]
    yb = up(cb_ref) * conv
    merged = (_sigmoid(up(g0_ref)) * up(ya_ref) + _sigmoid(up(g1_ref)) * yb
              + _sigmoid(up(g2_ref)) * up(yc_ref))
    o_ref[...] = x_ref[...] + _dot(merged.astype(bf16), wo_ref[...])


def _merge(ya, yc, proj, x2, cw, wo, T, tm):
    M = x2.shape[0]

    def col(name):
        base = _blk(name, D)
        return pl.BlockSpec((tm, D), lambda i: (i, base))

    def halo(name):
        base = _blk(name, D)
        return pl.BlockSpec((HALO, D), lambda i: (jnp.maximum(i * (tm // HALO) - 1, 0), base))

    row = pl.BlockSpec((tm, D), lambda i: (i, 0))
    return pl.pallas_call(
        functools.partial(_merge_kernel, tm=tm, tiles_per_seq=T // tm),
        grid=(M // tm,),
        in_specs=[row, col('cb'), col('cc'), col('cu'), halo('cc'), halo('cu'), row,
                  col('gm0'), col('gm1'), col('gm2'), row,
                  pl.BlockSpec((CONV_W, D), lambda i: (0, 0)),
                  pl.BlockSpec((D, D), lambda i: (0, 0))],
        out_specs=row,
        out_shape=jax.ShapeDtypeStruct((M, D), f32),
        scratch_shapes=[pltpu.VMEM((HALO + tm, D), f32)],
        compiler_params=_cparams(("parallel",)),
        name="merge_oproj",
    )(ya, proj, proj, proj, proj, proj, yc, proj, proj, proj, x2, cw, wo)


def _ffn_kernel(x_ref, g_ref, wu_ref, wd_ref, o_ref, h_ref):
    @pl.when(pl.program_id(1) == 0)
    def _():
        x = x_ref[...]
        h_ref[...] = _rms(x, g_ref[...]).astype(bf16)
        o_ref[...] = x

    u = jnp.maximum(_dot(h_ref[...], wu_ref[...]), 0.0)
    o_ref[...] += _dot((u * u).astype(bf16), wd_ref[...])


def _ffn(x2, g, wu, wd, tm, tf):
    M = x2.shape[0]
    return pl.pallas_call(
        _ffn_kernel,
        grid=(M // tm, D_FF // tf),
        in_specs=[pl.BlockSpec((tm, D), lambda i, j: (i, 0)),
                  pl.BlockSpec((1, D), lambda i, j: (0, 0)),
                  pl.BlockSpec((D, tf), lambda i, j: (0, j)),
                  pl.BlockSpec((tf, D), lambda i, j: (j, 0))],
        out_specs=pl.BlockSpec((tm, D), lambda i, j: (i, 0)),
        out_shape=jax.ShapeDtypeStruct((M, D), f32),
        scratch_shapes=[pltpu.VMEM((tm, D), bf16)],
        compiler_params=_cparams(("parallel", "arbitrary")),
        name="ffn",
    )(x2, g, wu, wd)


def _ple_kernel(x_ref, p_ref, g_ref, wp_ref, wg_ref, gf_ref, o_ref, *, final):
    x = x_ref[...]
    gate = _sigmoid(_dot(_rms(x, g_ref[...]).astype(bf16), wg_ref[...]))
    y = x + _dot(p_ref[...].astype(bf16), wp_ref[...]) * gate
    if final:
        y = _rms(y, gf_ref[...])
    o_ref[...] = y


def _ple(x2, p2, layer, g, wp, wg, gf, tm, final):
    M = x2.shape[0]
    base = layer * (M // tm)
    row = pl.BlockSpec((tm, D), lambda i: (i, 0))
    vec = pl.BlockSpec((1, D), lambda i: (0, 0))
    return pl.pallas_call(
        functools.partial(_ple_kernel, final=final),
        grid=(M // tm,),
        in_specs=[row, pl.BlockSpec((tm, PLE), lambda i: (base + i, 0)), vec,
                  pl.BlockSpec((PLE, D), lambda i: (0, 0)),
                  pl.BlockSpec((D, D), lambda i: (0, 0)), vec],
        out_specs=row,
        out_shape=jax.ShapeDtypeStruct((M, D), f32),
        compiler_params=_cparams(("parallel",)),
        name="ple",
    )(x2, p2, g, wp, wg, gf)


def _source_runs():
    runs = []
    for n in _ORDER:
        a, wd = _SRC[n]
        if runs and runs[-1][1] == a:
            runs[-1][1] = a + wd
        else:
            runs.append([a, a + wd])
    return runs


def _prep_w_in(w):
    depth = w.shape[0]
    main = jnp.concatenate([w[:, :, a:b] for a, b in _source_runs()], axis=2)
    a, wd = _SRC['g_nsa']
    gn = w[:, :, a:a + wd].reshape(depth, D, G, R * 3)
    gn = jnp.pad(gn, ((0, 0), (0, 0), (0, 0), (0, 128 - R * 3))).reshape(depth, D, G * 128)
    a, wd = _SRC['s_dt']
    dt = jnp.pad(w[:, :, a:a + wd], ((0, 0), (0, 0), (0, 128 - wd)))
    return main.astype(bf16), jnp.concatenate([gn, dt], axis=2).astype(bf16)


def _overlap_matrix(nc):
    i = np.arange(nc)[:, None]
    j = np.arange(NSB)[None, :]
    ovl = (i * CMP_STRIDE < j * SLC + SLC) & (i * CMP_STRIDE + CMP_BLOCK > j * SLC)
    return jnp.asarray(ovl.T.astype(np.float32), dtype=bf16)


def _head_expand():
    h = np.arange(128)[:, None]
    ch = np.arange(D)[None, :]
    return jnp.asarray((ch // SSD_P == h).astype(np.float32), dtype=bf16)


def _conv_shift():
    i = np.arange((SSD_CONV - 1) * SSD_L)[:, None]
    r = np.arange(2 * SSD_L)[None, :]
    pick = r == SSD_L + i % SSD_L - (SSD_CONV - 1) + i // SSD_L
    return jnp.asarray(pick.astype(np.float32), dtype=bf16)


def kernel(x, p, positions, g_mix, w_in, nsa_pe_k, nsa_pe_v, phi_k_w1, phi_k_b1, phi_k_w2, phi_k_b2,
           phi_v_w1, phi_v_b1, phi_v_w2, phi_v_b2, sconv_w, ssd_conv_w, ssd_conv_b, ssd_dt_bias,
           ssd_a_log, ssd_d, ssd_norm_g, w_o, g_mlp, w_up, w_down, g_ple, w_ple, w_ple_gate, g_final):
    B, T, _ = x.shape
    depth = w_in.shape[0]
    M = B * T
    assert T % 256 == 0 and T // SLC <= NSB
    tq = 512
    tm_proj = 1024 if M % 1024 == 0 else 256
    tm = 512 if T % 512 == 0 else 256
    nc = T // CMP_STRIDE

    inv_freq = 1.0 / (10000.0 ** (jnp.arange(0, DK, 2, dtype=f32) / DK))
    ang = positions.astype(f32)[..., None] * inv_freq
    cosf = jnp.concatenate([jnp.cos(ang), jnp.cos(ang)], axis=-1).reshape(M, DK)
    sinf = jnp.concatenate([-jnp.sin(ang), jnp.sin(ang)], axis=-1).reshape(M, DK)
    ovl = _overlap_matrix(nc)
    eh = _head_expand()
    shift = _conv_shift()

    def vec(a, n=None):
        a = a.reshape(1, -1).astype(f32)
        return a if n is None else jnp.pad(a, ((0, 0), (0, n - a.shape[1])))

    x2 = x.reshape(M, D)
    p2 = p.reshape(depth * M, PLE)
    w_main, w_small = _prep_w_in(w_in)
    for i in range(depth):
        proj, small = _proj(x2, vec(g_mix[i]), w_main[i], w_small[i], tm_proj, 2048)

        kc, vc, ksa, vs, kw, vw = _prep(proj, cosf, sinf, B, T, tq)

        def phi(pe, w1, b1, w2, b2):
            pe8 = jnp.broadcast_to(pe.reshape(1, -1), (8, CMP_BLOCK * DK)).astype(bf16)
            return (pe8, w1.astype(bf16), vec(b1), w2.astype(bf16), vec(b2))

        kcmp, vcmp = _compress(kc.reshape(B, G, nc, CMP_STRIDE * DK), vc.reshape(B, G, nc, CMP_STRIDE * DK),
                               phi(nsa_pe_k[i], phi_k_w1[i], phi_k_b1[i], phi_k_w2[i], phi_k_b2[i]),
                               phi(nsa_pe_v[i], phi_v_w1[i], phi_v_b1[i], phi_v_w2[i], phi_v_b2[i]))
        ya = _nsa(proj, small, cosf, sinf, ovl, kcmp, vcmp, ksa, vs, kw, vw, B, T, tq)

        cw, cbias = ssd_conv_w[i], ssd_conv_b[i]
        gn = SSD_G * SSD_N
        ssd_w = (cw[:, :D], vec(cbias[:D]), cw[:, D:D + gn], vec(cbias[D:D + gn]),
                 cw[:, D + gn:], vec(cbias[D + gn:]),
                 vec(ssd_dt_bias[i], 128), vec(ssd_a_log[i], 128),
                 vec(jnp.repeat(ssd_d[i], SSD_P)), vec(ssd_norm_g[i]), eh, shift)
        yc = _ssd(proj, small, ssd_w, B, T)

        x2 = _merge(ya, yc, proj, x2, sconv_w[i], w_o[i].astype(bf16), T, 256)
        x2 = _ffn(x2, vec(g_mlp[i]), w_up[i].astype(bf16), w_down[i].astype(bf16), tm_proj, 512)
        x2 = _ple(x2, p2, i, vec(g_ple[i]), w_ple[i].astype(bf16),
                  w_ple_gate[i].astype(bf16), vec(g_final), tm, final=(i == depth - 1))
    return x2.reshape(B, T, D)
```

```python
import functools

import numpy as np
import jax
import jax.numpy as jnp
from jax import lax
from jax.experimental import pallas as pl
from jax.experimental.pallas import tpu as pltpu

f32 = jnp.float32
bf16 = jnp.bfloat16

D = 2048
N_HEADS = 16
DK = 128
G = 4
R = N_HEADS // G
CMP_BLOCK = 32
CMP_STRIDE = 16
SLC = 64
N_SELECT = 16
WINDOW = 512
FORCE_BONUS = 1.0e4
CONV_W = 3
SSD_HEADS = 32
SSD_P = 64
SSD_G = 4
SSD_N = 128
SSD_CONV = 4
SSD_L = 128
D_FF = 4 * D
PLE = 256
EPS = 1e-6
NSB = 128
HALO = 16
NEG = -1e30
LOG2E = 1.4426950408889634

VMEM_LIMIT = 56 * 1024 * 1024

_SRC = {
    'q': (0, 2048), 'k_c': (2048, 512), 'v_c': (2560, 512), 'k_s': (3072, 512), 'v_s': (3584, 512),
    'k_w': (4096, 512), 'v_w': (4608, 512), 'g_nsa': (5120, 48),
    'cb': (5168, 2048), 'cc': (7216, 2048), 'cu': (9264, 2048),
    's_z': (11312, 2048), 'xs': (13360, 2048), 'bm': (15408, 512), 'cm': (15920, 512),
    's_dt': (16432, 32), 'gm0': (16464, 2048), 'gm1': (18512, 2048), 'gm2': (20560, 2048),
}
_ORDER = ['q', 'cb', 'cc', 'cu', 's_z', 'xs', 'gm0', 'gm1', 'gm2',
          'k_c', 'v_c', 'k_s', 'v_s', 'k_w', 'v_w', 'bm', 'cm']
_OFF = {}
_o = 0
for _n in _ORDER:
    _OFF[_n] = _o
    _o += _SRC[_n][1]
N_MAIN = _o
N_SMALL = G * 128 + 128


def _blk(name, width):
    off = _OFF[name]
    assert off % width == 0
    return off // width


def _cparams(sem):
    return pltpu.CompilerParams(dimension_semantics=sem, vmem_limit_bytes=VMEM_LIMIT)


def _dot(a, b):
    return jnp.dot(a, b, preferred_element_type=f32)


def _dot_nt(a, b):
    return lax.dot_general(a, b, (((1,), (1,)), ((), ())), preferred_element_type=f32)


def _sigmoid(x):
    return jax.nn.sigmoid(x)


def _split2(x):
    hi = x.astype(bf16)
    lo = (x - hi.astype(f32)).astype(bf16)
    return hi, lo


def _split3(x):
    hi = x.astype(bf16)
    r1 = x - hi.astype(f32)
    mid = r1.astype(bf16)
    lo = (r1 - mid.astype(f32)).astype(bf16)
    return hi, mid, lo


def _rms(x, g):
    ms = jnp.mean(x * x, axis=-1, keepdims=True)
    return x * lax.rsqrt(ms + EPS) * g


def _proj_kernel(x_ref, g_ref, w_ref, ws_ref, o_ref, os_ref, h_ref):
    @pl.when(pl.program_id(1) == 0)
    def _():
        h = _rms(x_ref[...], g_ref[...]).astype(bf16)
        h_ref[...] = h
        os_ref[...] = _dot(h, ws_ref[...])

    o_ref[...] = _dot(h_ref[...], w_ref[...]).astype(bf16)


def _proj(x2, g, w_main, w_small, tm, tn):
    M = x2.shape[0]
    return pl.pallas_call(
        _proj_kernel,
        grid=(M // tm, N_MAIN // tn),
        in_specs=[
            pl.BlockSpec((tm, D), lambda i, j: (i, 0), pipeline_mode=pl.Buffered(1)),
            pl.BlockSpec((1, D), lambda i, j: (0, 0)),
            pl.BlockSpec((D, tn), lambda i, j: (0, j)),
            pl.BlockSpec((D, N_SMALL), lambda i, j: (0, 0), pipeline_mode=pl.Buffered(1)),
        ],
        out_specs=[
            pl.BlockSpec((tm, tn), lambda i, j: (i, j)),
            pl.BlockSpec((tm, N_SMALL), lambda i, j: (i, 0)),
        ],
        out_shape=[jax.ShapeDtypeStruct((M, N_MAIN), bf16),
                   jax.ShapeDtypeStruct((M, N_SMALL), f32)],
        scratch_shapes=[pltpu.VMEM((tm, D), bf16)],
        compiler_params=_cparams(("parallel", "arbitrary")),
        name="proj",
    )(x2, g, w_main, w_small)


def _rope(x, cosf, sinf):
    x = x.astype(f32)
    return x * cosf + pltpu.roll(x, DK // 2, axis=1) * sinf


def _prep_kernel(kc_ref, vc_ref, ks_ref, vs_ref, kw_ref, vw_ref, cos_ref, sin_ref,
                 kco_ref, vco_ref, ksa_ref, vso_ref, kwo_ref, vwo_ref, stage_ref, *, tt):
    cosf = cos_ref[...]
    sinf = sin_ref[...]
    t = pl.program_id(1) * tt + lax.broadcasted_iota(jnp.int32, (tt, NSB), 0)
    j = lax.broadcasted_iota(jnp.int32, (tt, NSB), 1)
    onehot = jnp.where(t // SLC == j, 1.0, 0.0).astype(bf16)
    rows = tt // CMP_STRIDE

    def to_block_rows(x, o_ref, g):
        stage_ref[...] = x
        for l in range(CMP_STRIDE):
            o_ref[g, :, l * DK:(l + 1) * DK] = stage_ref[pl.ds(l, rows, stride=CMP_STRIDE), :].astype(bf16)

    for g in range(G):
        cols = slice(g * DK, (g + 1) * DK)
        to_block_rows(_rope(kc_ref[:, cols], cosf, sinf), kco_ref, g)
        to_block_rows(vc_ref[:, cols].astype(f32), vco_ref, g)
        ksa_ref[g, :, 0:DK] = _rope(ks_ref[:, cols], cosf, sinf).astype(bf16)
        ksa_ref[g, :, DK:DK + NSB] = onehot
        vso_ref[g] = vs_ref[:, cols].astype(f32).T.astype(bf16)
        kwo_ref[g] = _rope(kw_ref[:, cols], cosf, sinf).astype(bf16)
        vwo_ref[g] = vw_ref[:, cols].astype(f32).T.astype(bf16)


def _prep(proj, cosf, sinf, B, T, tt):
    nt = T // tt
    gw = G * DK
    rows = tt // CMP_STRIDE

    def col(name):
        base = _blk(name, gw)
        return pl.BlockSpec((tt, gw), lambda b, i: (b * nt + i, base))

    tab = pl.BlockSpec((tt, DK), lambda b, i: (b * nt + i, 0))

    def out(n, w):
        return pl.BlockSpec((None, G, n, w), lambda b, i: (b, 0, i, 0))

    out_t = pl.BlockSpec((None, G, None, DK, tt), lambda b, i: (b, 0, i, 0, 0))
    return pl.pallas_call(
        functools.partial(_prep_kernel, tt=tt),
        grid=(B, nt),
        in_specs=[col('k_c'), col('v_c'), col('k_s'), col('v_s'), col('k_w'), col('v_w'), tab, tab],
        out_specs=[out(rows, CMP_STRIDE * DK), out(rows, CMP_STRIDE * DK), out(tt, DK + NSB), out_t,
                   out(tt, DK), out_t],
        out_shape=[jax.ShapeDtypeStruct((B, G, T // CMP_STRIDE, CMP_STRIDE * DK), bf16),
                   jax.ShapeDtypeStruct((B, G, T // CMP_STRIDE, CMP_STRIDE * DK), bf16),
                   jax.ShapeDtypeStruct((B, G, T, DK + NSB), bf16),
                   jax.ShapeDtypeStruct((B, G, nt, DK, tt), bf16),
                   jax.ShapeDtypeStruct((B, G, T, DK), bf16),
                   jax.ShapeDtypeStruct((B, G, nt, DK, tt), bf16)],
        scratch_shapes=[pltpu.VMEM((tt, DK), f32)],
        compiler_params=_cparams(("parallel", "parallel")),
        name="kv_prep",
    )(proj, proj, proj, proj, proj, proj, cosf, sinf)


def _compress_one(x_ref, pe_ref, w1_ref, b1_ref, w2_ref, b2_ref, o_ref, transpose):
    nc = x_ref.shape[0]
    half = CMP_STRIDE * DK
    x = x_ref[...]
    first = _dot(x, w1_ref[0:half, :])
    second = _dot(x, w1_ref[half:2 * half, :])
    second = pltpu.roll(second, nc - 1, axis=0)
    pe = _dot(pe_ref[...], w1_ref[...])[0:1, :]
    h = first + second + pe + b1_ref[...]
    h = h * _sigmoid(h)
    y = _dot(h.astype(bf16), w2_ref[...]) + b2_ref[...]
    o_ref[...] = (y.T if transpose else y).astype(bf16)


def _compress_kernel(k_ref, v_ref, pek, w1k, b1k, w2k, b2k, pev, w1v, b1v, w2v, b2v, ko_ref, vo_ref):
    _compress_one(k_ref, pek, w1k, b1k, w2k, b2k, ko_ref, False)
    _compress_one(v_ref, pev, w1v, b1v, w2v, b2v, vo_ref, True)


def _compress(kc2, vc2, wk, wv):
    B, _, nc, width = kc2.shape
    xin = pl.BlockSpec((None, None, nc, width), lambda b, g: (b, g, 0, 0))

    def full(a):
        return pl.BlockSpec(a.shape, lambda b, g: (0,) * a.ndim)

    out = pl.BlockSpec((None, None, nc, DK), lambda b, g: (b, g, 0, 0))
    out_t = pl.BlockSpec((None, None, DK, nc), lambda b, g: (b, g, 0, 0))
    return pl.pallas_call(
        _compress_kernel,
        grid=(B, G),
        in_specs=[xin, xin] + [full(a) for a in wk] + [full(a) for a in wv],
        out_specs=[out, out_t],
        out_shape=[jax.ShapeDtypeStruct((B, G, nc, DK), bf16),
                   jax.ShapeDtypeStruct((B, G, DK, nc), bf16)],
        compiler_params=_cparams(("parallel", "parallel")),
        name="compress",
    )(kc2, vc2, *wk, *wv)


def _nsa_kernel(q_ref, cos_ref, sin_ref, gate_ref, ovl_ref, kcmp_ref, vcmp_ref,
                ksa_ref, vs_ref, kw_ref, vw_ref, o_ref,
                qa_ref, ms_ref, ls_ref, accs_ref, mw_ref, lw_ref, accw_ref, out_ref,
                sc_ref, mx_ref, sd_ref, md_ref,
                *, tq, n_sel):
    qi = pl.program_id(2)
    q0 = qi * tq
    heads = [slice(r * tq, (r + 1) * tq) for r in range(R)]
    qscale = DK ** -0.5 * LOG2E
    cosf = cos_ref[...]
    sinf = sin_ref[...]
    for r in range(R):
        qr = _rope(q_ref[:, r * DK:(r + 1) * DK], cosf, sinf) * qscale
        qa_ref[0:DK, heads[r]] = qr.T.astype(bf16)
    t1 = q0 + lax.broadcasted_iota(jnp.int32, (1, tq), 1)
    gate_t = _sigmoid(gate_ref[...]).T

    def gate_row(c, r):
        return gate_t[3 * r + c:3 * r + c + 1, :]

    nc = kcmp_ref.shape[0]
    n_idx = lax.broadcasted_iota(jnp.int32, (nc, 1), 0)
    vis = (n_idx * CMP_STRIDE + (CMP_BLOCK - 1)) <= t1
    has_any = t1 >= CMP_BLOCK - 1
    psum = jnp.zeros((nc, tq), f32)
    for r in range(R):
        s = jnp.where(vis, _dot(kcmp_ref[...], qa_ref[0:DK, heads[r]]), NEG)
        e = jnp.exp2(s - jnp.max(s, axis=0, keepdims=True))
        inv = jnp.where(has_any, 1.0 / jnp.sum(e, axis=0, keepdims=True), 0.0)
        out_ref[:, heads[r]] = (gate_row(0, r) * inv) * _dot(vcmp_ref[...], e.astype(bf16))
        psum = psum + e * inv
    p_hi, p_lo = _split2(psum)
    imp = _dot(ovl_ref[...], p_hi) + _dot(ovl_ref[...], p_lo)

    def reset(m_ref, l_ref, acc_ref):
        m_ref[...] = jnp.full(m_ref.shape, NEG, f32)
        l_ref[...] = jnp.zeros(l_ref.shape, f32)
        acc_ref[...] = jnp.zeros(acc_ref.shape, f32)

    def scores_to(s_ref, smax_ref, k_ref, kt, q_rows, mask, r):
        s = _dot(k_ref[pl.ds(pl.multiple_of(kt * tq, tq), tq), :], qa_ref[q_rows, heads[r]])
        if mask is not None:
            s = jnp.where(mask, s, NEG)
        s_ref[:, heads[r]] = s
        smax_ref[:, heads[r]] = jnp.max(s, axis=0, keepdims=True)

    def softmax_pv_head(state, r, s, smax, v_tiles):
        m_ref, l_ref, acc_ref = state
        m_prev = m_ref[:, heads[r]]
        m_new = jnp.maximum(m_prev, smax)
        alpha = jnp.exp2(m_prev - m_new)
        pt = jnp.exp2(s - m_new)
        ptb = pt.astype(bf16)
        pv = _dot(v_tiles[0], ptb[0:tq])
        for i in range(1, len(v_tiles)):
            pv = pv + _dot(v_tiles[i], ptb[i * tq:(i + 1) * tq])
        l_ref[:, heads[r]] = alpha * l_ref[:, heads[r]] + jnp.sum(pt, axis=0, keepdims=True)
        acc_ref[:, heads[r]] = alpha * acc_ref[:, heads[r]] + pv
        m_ref[:, heads[r]] = m_new

    def finish(state, c):
        m_ref, l_ref, acc_ref = state
        for r in range(R):
            out_ref[:, heads[r]] += (gate_row(c, r) * (1.0 / l_ref[:, heads[r]])) * acc_ref[:, heads[r]]

    plain = slice(0, DK)
    aug = slice(0, DK + NSB)

    buf_a = (sd_ref, md_ref)
    buf_b = (sc_ref, mx_ref)

    def tile_scores(buf, k_ref, kt, q_rows, mask):
        for r in range(R):
            scores_to(buf[0], buf[1], k_ref, kt, q_rows, mask, r)

    def tile_softmax(buf, state, v_ref, kt):
        for r in range(R):
            softmax_pv_head(state, r, buf[0][:, heads[r]], buf[1][:, heads[r]], [v_ref[kt]])

    kpos = lax.broadcasted_iota(jnp.int32, (tq, 1), 0)
    causal = q0 + kpos <= t1

    win = (mw_ref, lw_ref, accw_ref)
    reset(*win)
    prev = jnp.maximum(qi - 1, 0)
    inside = ((q0 - tq) + kpos > t1 - WINDOW) & (qi >= 1)
    tile_scores(buf_a, kw_ref, qi, plain, causal)
    tile_scores(buf_b, kw_ref, prev, plain, inside)
    tile_softmax(buf_a, win, vw_ref, qi)
    tile_softmax(buf_b, win, vw_ref, prev)
    finish(win, 2)

    j_idx = lax.broadcasted_iota(jnp.int32, (NSB, 1), 0)
    j_f = j_idx.astype(f32)
    cur = t1 // SLC
    valid = (j_idx * SLC) <= t1
    forced = (j_idx == 0) | (j_idx == cur) | (j_idx == cur - 1)
    score = jnp.where(valid, imp + jnp.where(forced, FORCE_BONUS, 0.0), -jnp.inf)
    sel = jnp.zeros((NSB, tq), f32)
    for _ in range(n_sel):
        mx = jnp.max(score, axis=0, keepdims=True)
        first = jnp.min(jnp.where(score == mx, j_f, float(NSB)), axis=0, keepdims=True)
        hit = j_f == first
        sel = jnp.where(hit, 1.0, sel)
        score = jnp.where(hit, -jnp.inf, score)
    nsel = jnp.where(sel > 0.5, 0.0, NEG).astype(bf16)
    for r in range(R):
        qa_ref[DK:DK + NSB, heads[r]] = nsel

    selst = (ms_ref, ls_ref, accs_ref)
    reset(*selst)
    tile_scores(buf_a, ksa_ref, qi, aug, causal)
    tile_scores(buf_b, ksa_ref, 0, aug, None)
    tile_softmax(buf_a, selst, vs_ref, qi)

    def sel_body(kt, carry):
        nxt = jnp.minimum(kt + 1, jnp.maximum(qi - 1, 0))
        for r in range(R):
            softmax_pv_head(selst, r, sc_ref[:, heads[r]], mx_ref[:, heads[r]], [vs_ref[kt]])
            scores_to(sc_ref, mx_ref, ksa_ref, nxt, aug, None, r)
        return carry

    lax.fori_loop(0, qi, sel_body, 0)
    finish(selst, 1)

    for r in range(R):
        o_ref[:, r * DK:(r + 1) * DK] = out_ref[:, heads[r]].T.astype(bf16)


def _nsa(proj, small, cosf, sinf, ovl_t, kcmp, vcmp_t, ksa, vs_t, kw, vw_t, B, T, tq):
    nq = T // tq
    nc = kcmp.shape[2]
    rows = R * tq
    n_sel = min(N_SELECT, T // SLC)
    assert tq >= WINDOW and nq >= 2

    def per_bg(*shape):
        return pl.BlockSpec((None, None) + shape, lambda b, g, i: (b, g) + (0,) * len(shape))

    tab = pl.BlockSpec((tq, DK), lambda b, g, i: (b * nq + i, 0))
    return pl.pallas_call(
        functools.partial(_nsa_kernel, tq=tq, n_sel=n_sel),
        grid=(B, G, nq),
        in_specs=[
            pl.BlockSpec((tq, R * DK), lambda b, g, i: (b * nq + i, g)),
            tab, tab,
            pl.BlockSpec((tq, 128), lambda b, g, i: (b * nq + i, g)),
            pl.BlockSpec((NSB, nc), lambda b, g, i: (0, 0)),
            per_bg(nc, DK), per_bg(DK, nc),
            per_bg(T, DK + NSB), per_bg(nq, DK, tq), per_bg(T, DK), per_bg(nq, DK, tq),
        ],
        out_specs=pl.BlockSpec((tq, R * DK), lambda b, g, i: (b * nq + i, g)),
        out_shape=jax.ShapeDtypeStruct((B * T, D), bf16),
        scratch_shapes=[pltpu.VMEM((DK + NSB, rows), bf16)]
        + [pltpu.VMEM((1, rows), f32), pltpu.VMEM((1, rows), f32), pltpu.VMEM((DK, rows), f32)] * 2
        + [pltpu.VMEM((DK, rows), f32)]
        + [pltpu.VMEM((tq, rows), f32), pltpu.VMEM((1, rows), f32)] * 2,
        compiler_params=_cparams(("parallel", "parallel", "arbitrary")),
        name="nsa_attention",
    )(proj, cosf, sinf, small, ovl_t, kcmp, vcmp_t, ksa, vs_t, kw, vw_t)


def _ssd_kernel(z_ref, xs_ref, bm_ref, cm_ref, dt_ref,
                wx_ref, bx_ref, wb_ref, bb_ref, wc_ref, bc_ref,
                dtb_ref, alog_ref, dskip_ref, ng_ref, eh_ref, shift_ref,
                o_ref, st_ref, prev_ref, y_ref):
    c = pl.program_id(1)
    L = SSD_L
    HP = 2 * SSD_P
    GW = D // SSD_G

    @pl.when(c == 0)
    def _():
        st_ref[...] = jnp.zeros_like(st_ref)
        prev_ref[...] = jnp.zeros_like(prev_ref)

    def conv_silu(u_ref, col, w_ref, b_ref):
        width = u_ref.shape[1]
        cur = u_ref[...]
        ext = jnp.concatenate([prev_ref[:, col:col + width], cur], axis=0)
        sh = _dot(shift_ref[...], ext)
        y = b_ref[...] + w_ref[SSD_CONV - 1:SSD_CONV, :] * cur.astype(f32)
        for j in range(SSD_CONV - 1):
            y = y + w_ref[j:j + 1, :] * sh[j * L:(j + 1) * L]
        prev_ref[:, col:col + width] = cur
        return y * _sigmoid(y)

    xs = conv_silu(xs_ref, 0, wx_ref, bx_ref)
    bm = conv_silu(bm_ref, D, wb_ref, bb_ref)
    cm = conv_silu(cm_ref, D + SSD_G * SSD_N, wc_ref, bc_ref)

    lane = lax.broadcasted_iota(jnp.int32, (1, 128), 1)
    xdt_in = dt_ref[...] + dtb_ref[...]
    dt = jnp.maximum(xdt_in, 0.0) + jnp.log1p(jnp.exp(-jnp.abs(xdt_in)))
    dt = jnp.where(lane < SSD_HEADS, dt, 0.0)
    da = dt * (-jnp.exp(alog_ref[...]))

    row = lax.broadcasted_iota(jnp.int32, (L, L), 0)
    colm = lax.broadcasted_iota(jnp.int32, (L, L), 1)
    causal = colm <= row
    tril = jnp.where(causal, 1.0, 0.0).astype(bf16)
    d_hi, d_mid, d_lo = _split3(da)
    acs = _dot(tril, d_hi) + _dot(tril, d_mid) + _dot(tril, d_lo)
    acs_t = acs.T
    last = acs[L - 1:L, :]
    eacs = jnp.exp(acs)
    dec = jnp.exp(last - acs)
    cdec = jnp.exp(jnp.broadcast_to(last, (8, 128)))

    eh = eh_ref[...]
    dt_x = _dot(dt.astype(bf16), eh)
    eacs_x = _dot(eacs.astype(bf16), eh)
    dec_x = _dot(dec.astype(bf16), eh)
    c_hi, c_lo = _split2(cdec)
    cdec_x = (_dot(c_hi, eh) + _dot(c_lo, eh))[0:1, :]

    xdt = xs * dt_x
    xdt_b = xdt.astype(bf16)
    xdec_b = (xdt * dec_x).astype(bf16)
    lane_hp = lax.broadcasted_iota(jnp.int32, (L, HP), 1)

    for g in range(SSD_G):
        cm_g = cm[:, g * SSD_N:(g + 1) * SSD_N].astype(bf16)
        bm_g = bm[:, g * SSD_N:(g + 1) * SSD_N]
        cb = _dot_nt(cm_g, bm_g.astype(bf16))
        s_in = st_ref[g]
        y_off = _dot(cm_g, s_in.astype(bf16))
        s_new = _dot(bm_g.T.astype(bf16), xdec_b[:, g * GW:(g + 1) * GW])
        st_ref[g] = s_in * cdec_x[:, g * GW:(g + 1) * GW] + s_new
        for pp in range(GW // HP):
            h0 = g * (GW // SSD_P) + 2 * pp
            c0 = g * GW + pp * HP
            ms = []
            for h in (h0, h0 + 1):
                diff = acs[:, h:h + 1] - acs_t[h:h + 1, :]
                ms.append(jnp.where(causal, cb * jnp.exp(jnp.where(causal, diff, NEG)), 0.0))
            lhs = jnp.concatenate(ms, axis=1).astype(bf16)
            xp = xdt_b[:, c0:c0 + HP]
            zero = jnp.zeros_like(xp)
            rhs = jnp.concatenate([jnp.where(lane_hp < SSD_P, xp, zero),
                                   jnp.where(lane_hp >= SSD_P, xp, zero)], axis=0)
            y = (_dot(lhs, rhs) + eacs_x[:, c0:c0 + HP] * y_off[:, pp * HP:(pp + 1) * HP]
                 + dskip_ref[:, c0:c0 + HP] * xs[:, c0:c0 + HP])
            y_ref[:, c0:c0 + HP] = y

    z = z_ref[...].astype(f32)
    y = y_ref[...] * (z * _sigmoid(z))
    for g in range(SSD_G):
        yg = y[:, g * GW:(g + 1) * GW]
        ms = jnp.mean(yg * yg, axis=-1, keepdims=True)
        o_ref[:, g * GW:(g + 1) * GW] = (yg * lax.rsqrt(ms + EPS) * ng_ref[:, g * GW:(g + 1) * GW]).astype(bf16)


def _ssd(proj, small, wts, B, T):
    nch = T // SSD_L
    L = SSD_L

    def main(name, w):
        base = _blk(name, w)
        return pl.BlockSpec((L, w), lambda b, c: (b * nch + c, base))

    def full(a):
        return pl.BlockSpec(a.shape, lambda b, c: (0,) * a.ndim)

    gn = SSD_G * SSD_N
    return pl.pallas_call(
        _ssd_kernel,
        grid=(B, nch),
        in_specs=[main('s_z', D), main('xs', D), main('bm', gn), main('cm', gn),
                  pl.BlockSpec((L, 128), lambda b, c: (b * nch + c, G))] + [full(a) for a in wts],
        out_specs=pl.BlockSpec((L, D), lambda b, c: (b * nch + c, 0)),
        out_shape=jax.ShapeDtypeStruct((B * T, D), bf16),
        scratch_shapes=[pltpu.VMEM((SSD_G, SSD_N, D // SSD_G), f32),
                        pltpu.VMEM((L, D + 2 * gn), bf16),
                        pltpu.VMEM((L, D), f32)],
        compiler_params=_cparams(("parallel", "arbitrary")),
        name="ssd",
    )(proj, proj, proj, proj, small, *wts)


def _merge_kernel(ya_ref, cb_ref, cc_ref, cu_ref, cch_ref, cuh_ref, yc_ref,
                  g0_ref, g1_ref, g2_ref, x_ref, cw_ref, wo_ref, o_ref, ext_ref, *, tm, tiles_per_seq):
    first = (pl.program_id(0) % tiles_per_seq) == 0
    up = lambda ref: ref[...].astype(f32)
    ext_ref[0:HALO, :] = jnp.where(first, 0.0, up(cch_ref) * up(cuh_ref))
    ext_ref[HALO:HALO + tm, :] = up(cc_ref) * up(cu_ref)
    lo = HALO - (CONV_W - 1)
    conv = cw_ref[0:1, :] * ext_ref[lo:lo + tm, :]
    for j in range(1, CONV_W):
        conv = conv + cw_ref[j:j + 1, :] * ext_ref[lo + j:lo + j + tm, :]
    yb = up(cb_ref) * conv
    merged = (_sigmoid(up(g0_ref)) * up(ya_ref) + _sigmoid(up(g1_ref)) * yb
              + _sigmoid(up(g2_ref)) * up(yc_ref))
    o_ref[...] = x_ref[...] + _dot(merged.astype(bf16), wo_ref[...])


def _merge(ya, yc, proj, x2, cw, wo, T, tm):
    M = x2.shape[0]

    def col(name):
        base = _blk(name, D)
        return pl.BlockSpec((tm, D), lambda i: (i, base))

    def halo(name):
        base = _blk(name, D)
        return pl.BlockSpec((HALO, D), lambda i: (jnp.maximum(i * (tm // HALO) - 1, 0), base))

    row = pl.BlockSpec((tm, D), lambda i: (i, 0))
    return pl.pallas_call(
        functools.partial(_merge_kernel, tm=tm, tiles_per_seq=T // tm),
        grid=(M // tm,),
        in_specs=[row, col('cb'), col('cc'), col('cu'), halo('cc'), halo('cu'), row,
                  col('gm0'), col('gm1'), col('gm2'), row,
                  pl.BlockSpec((CONV_W, D), lambda i: (0, 0)),
                  pl.BlockSpec((D, D), lambda i: (0, 0))],
        out_specs=row,
        out_shape=jax.ShapeDtypeStruct((M, D), f32),
        scratch_shapes=[pltpu.VMEM((HALO + tm, D), f32)],
        compiler_params=_cparams(("parallel",)),
        name="merge_oproj",
    )(ya, proj, proj, proj, proj, proj, yc, proj, proj, proj, x2, cw, wo)


def _ffn_kernel(x_ref, g_ref, wu_ref, wd_ref, o_ref, h_ref):
    @pl.when(pl.program_id(1) == 0)
    def _():
        x = x_ref[...]
        h_ref[...] = _rms(x, g_ref[...]).astype(bf16)
        o_ref[...] = x

    u = jnp.maximum(_dot(h_ref[...], wu_ref[...]), 0.0)
    o_ref[...] += _dot((u * u).astype(bf16), wd_ref[...])


def _ffn(x2, g, wu, wd, tm, tf):
    M = x2.shape[0]
    return pl.pallas_call(
        _ffn_kernel,
        grid=(M // tm, D_FF // tf),
        in_specs=[pl.BlockSpec((tm, D), lambda i, j: (i, 0), pipeline_mode=pl.Buffered(1)),
                  pl.BlockSpec((1, D), lambda i, j: (0, 0)),
                  pl.BlockSpec((D, tf), lambda i, j: (0, j)),
                  pl.BlockSpec((tf, D), lambda i, j: (j, 0))],
        out_specs=pl.BlockSpec((tm, D), lambda i, j: (i, 0)),
        out_shape=jax.ShapeDtypeStruct((M, D), f32),
        scratch_shapes=[pltpu.VMEM((tm, D), bf16)],
        compiler_params=_cparams(("parallel", "arbitrary")),
        name="ffn",
    )(x2, g, wu, wd)


def _ple_kernel(x_ref, p_ref, g_ref, wp_ref, wg_ref, gf_ref, o_ref, *, final):
    x = x_ref[...]
    gate = _sigmoid(_dot(_rms(x, g_ref[...]).astype(bf16), wg_ref[...]))
    y = x + _dot(p_ref[...].astype(bf16), wp_ref[...]) * gate
    if final:
        y = _rms(y, gf_ref[...])
    o_ref[...] = y


def _ple(x2, p2, layer, g, wp, wg, gf, tm, final):
    M = x2.shape[0]
    base = layer * (M // tm)
    row = pl.BlockSpec((tm, D), lambda i: (i, 0))
    vec = pl.BlockSpec((1, D), lambda i: (0, 0))
    return pl.pallas_call(
        functools.partial(_ple_kernel, final=final),
        grid=(M // tm,),
        in_specs=[row, pl.BlockSpec((tm, PLE), lambda i: (base + i, 0)), vec,
                  pl.BlockSpec((PLE, D), lambda i: (0, 0)),
                  pl.BlockSpec((D, D), lambda i: (0, 0)), vec],
        out_specs=row,
        out_shape=jax.ShapeDtypeStruct((M, D), f32),
        compiler_params=_cparams(("parallel",)),
        name="ple",
    )(x2, p2, g, wp, wg, gf)


def _source_runs():
    runs = []
    for n in _ORDER:
        a, wd = _SRC[n]
        if runs and runs[-1][1] == a:
            runs[-1][1] = a + wd
        else:
            runs.append([a, a + wd])
    return runs


def _prep_w_in(w):
    depth = w.shape[0]
    main = jnp.concatenate([w[:, :, a:b] for a, b in _source_runs()], axis=2)
    a, wd = _SRC['g_nsa']
    gn = w[:, :, a:a + wd].reshape(depth, D, G, R * 3)
    gn = jnp.pad(gn, ((0, 0), (0, 0), (0, 0), (0, 128 - R * 3))).reshape(depth, D, G * 128)
    a, wd = _SRC['s_dt']
    dt = jnp.pad(w[:, :, a:a + wd], ((0, 0), (0, 0), (0, 128 - wd)))
    return main.astype(bf16), jnp.concatenate([gn, dt], axis=2).astype(bf16)


def _overlap_matrix(nc):
    i = np.arange(nc)[:, None]
    j = np.arange(NSB)[None, :]
    ovl = (i * CMP_STRIDE < j * SLC + SLC) & (i * CMP_STRIDE + CMP_BLOCK > j * SLC)
    return jnp.asarray(ovl.T.astype(np.float32), dtype=bf16)


def _head_expand():
    h = np.arange(128)[:, None]
    ch = np.arange(D)[None, :]
    return jnp.asarray((ch // SSD_P == h).astype(np.float32), dtype=bf16)


def _conv_shift():
    i = np.arange((SSD_CONV - 1) * SSD_L)[:, None]
    r = np.arange(2 * SSD_L)[None, :]
    pick = r == SSD_L + i % SSD_L - (SSD_CONV - 1) + i // SSD_L
    return jnp.asarray(pick.astype(np.float32), dtype=bf16)


def kernel(x, p, positions, g_mix, w_in, nsa_pe_k, nsa_pe_v, phi_k_w1, phi_k_b1, phi_k_w2, phi_k_b2,
           phi_v_w1, phi_v_b1, phi_v_w2, phi_v_b2, sconv_w, ssd_conv_w, ssd_conv_b, ssd_dt_bias,
           ssd_a_log, ssd_d, ssd_norm_g, w_o, g_mlp, w_up, w_down, g_ple, w_ple, w_ple_gate, g_final):
    B, T, _ = x.shape
    depth = w_in.shape[0]
    M = B * T
    assert T % 256 == 0 and T // SLC <= NSB
    tq = 512
    tm_proj = 1024 if M % 1024 == 0 else 256
    tm = 512 if T % 512 == 0 else 256
    nc = T // CMP_STRIDE

    inv_freq = 1.0 / (10000.0 ** (jnp.arange(0, DK, 2, dtype=f32) / DK))
    ang = positions.astype(f32)[..., None] * inv_freq
    cosf = jnp.concatenate([jnp.cos(ang), jnp.cos(ang)], axis=-1).reshape(M, DK)
    sinf = jnp.concatenate([-jnp.sin(ang), jnp.sin(ang)], axis=-1).reshape(M, DK)
    ovl = _overlap_matrix(nc)
    eh = _head_expand()
    shift = _conv_shift()

    def vec(a, n=None):
        a = a.reshape(1, -1).astype(f32)
        return a if n is None else jnp.pad(a, ((0, 0), (0, n - a.shape[1])))

    x2 = x.reshape(M, D)
    p2 = p.reshape(depth * M, PLE)
    w_main, w_small = _prep_w_in(w_in)
    for i in range(depth):
        proj, small = _proj(x2, vec(g_mix[i]), w_main[i], w_small[i], tm_proj, 2048)

        kc, vc, ksa, vs, kw, vw = _prep(proj, cosf, sinf, B, T, tq)

        def phi(pe, w1, b1, w2, b2):
            pe8 = jnp.broadcast_to(pe.reshape(1, -1), (8, CMP_BLOCK * DK)).astype(bf16)
            return (pe8, w1.astype(bf16), vec(b1), w2.astype(bf16), vec(b2))

        kcmp, vcmp = _compress(kc, vc,
                               phi(nsa_pe_k[i], phi_k_w1[i], phi_k_b1[i], phi_k_w2[i], phi_k_b2[i]),
                               phi(nsa_pe_v[i], phi_v_w1[i], phi_v_b1[i], phi_v_w2[i], phi_v_b2[i]))
        ya = _nsa(proj, small, cosf, sinf, ovl, kcmp, vcmp, ksa, vs, kw, vw, B, T, tq)

        cw, cbias = ssd_conv_w[i], ssd_conv_b[i]
        gn = SSD_G * SSD_N
        ssd_w = (cw[:, :D], vec(cbias[:D]), cw[:, D:D + gn], vec(cbias[D:D + gn]),
                 cw[:, D + gn:], vec(cbias[D + gn:]),
                 vec(ssd_dt_bias[i], 128), vec(ssd_a_log[i], 128),
                 vec(jnp.repeat(ssd_d[i], SSD_P)), vec(ssd_norm_g[i]), eh, shift)
        yc = _ssd(proj, small, ssd_w, B, T)

        x2 = _merge(ya, yc, proj, x2, sconv_w[i], w_o[i].astype(bf16), T, 256)
        x2 = _ffn(x2, vec(g_mlp[i]), w_up[i].astype(bf16), w_down[i].astype(bf16), tm_proj, 1024)
        x2 = _ple(x2, p2, i, vec(g_ple[i]), w_ple[i].astype(bf16),
                  w_ple_gate[i].astype(bf16), vec(g_final), tm, final=(i == depth - 1))
    return x2.reshape(B, T, D)
```

```python
import functools

import numpy as np
import jax
import jax.numpy as jnp
from jax import lax
from jax.experimental import pallas as pl
from jax.experimental.pallas import tpu as pltpu

f32 = jnp.float32
bf16 = jnp.bfloat16

D = 2048
N_HEADS = 16
DK = 128
G = 4
R = N_HEADS // G
CMP_BLOCK = 32
CMP_STRIDE = 16
SLC = 64
N_SELECT = 16
WINDOW = 512
FORCE_BONUS = 1.0e4
CONV_W = 3
SSD_HEADS = 32
SSD_P = 64
SSD_G = 4
SSD_N = 128
SSD_CONV = 4
SSD_L = 128
D_FF = 4 * D
PLE = 256
EPS = 1e-6
NSB = 128
HALO = 16
NEG = -1e30
LOG2E = 1.4426950408889634

VMEM_LIMIT = 56 * 1024 * 1024

_SRC = {
    'q': (0, 2048), 'k_c': (2048, 512), 'v_c': (2560, 512), 'k_s': (3072, 512), 'v_s': (3584, 512),
    'k_w': (4096, 512), 'v_w': (4608, 512), 'g_nsa': (5120, 48),
    'cb': (5168, 2048), 'cc': (7216, 2048), 'cu': (9264, 2048),
    's_z': (11312, 2048), 'xs': (13360, 2048), 'bm': (15408, 512), 'cm': (15920, 512),
    's_dt': (16432, 32), 'gm0': (16464, 2048), 'gm1': (18512, 2048), 'gm2': (20560, 2048),
}
_ORDER = ['q', 'cb', 'cc', 'cu', 's_z', 'xs', 'gm0', 'gm1', 'gm2',
          'k_c', 'v_c', 'k_s', 'v_s', 'k_w', 'v_w', 'bm', 'cm']
_OFF = {}
_o = 0
for _n in _ORDER:
    _OFF[_n] = _o
    _o += _SRC[_n][1]
N_MAIN = _o
N_SMALL = G * 128 + 128


def _blk(name, width):
    off = _OFF[name]
    assert off % width == 0
    return off // width


def _cparams(sem):
    return pltpu.CompilerParams(dimension_semantics=sem, vmem_limit_bytes=VMEM_LIMIT)


def _dot(a, b):
    return jnp.dot(a, b, preferred_element_type=f32)


def _dot_nt(a, b):
    return lax.dot_general(a, b, (((1,), (1,)), ((), ())), preferred_element_type=f32)


def _sigmoid(x):
    return jax.nn.sigmoid(x)


def _split2(x):
    hi = x.astype(bf16)
    lo = (x - hi.astype(f32)).astype(bf16)
    return hi, lo


def _split3(x):
    hi = x.astype(bf16)
    r1 = x - hi.astype(f32)
    mid = r1.astype(bf16)
    lo = (r1 - mid.astype(f32)).astype(bf16)
    return hi, mid, lo


def _rms(x, g):
    ms = jnp.mean(x * x, axis=-1, keepdims=True)
    return x * lax.rsqrt(ms + EPS) * g


def _proj_kernel(x_ref, g_ref, w_ref, ws_ref, o_ref, os_ref, h_ref):
    @pl.when(pl.program_id(1) == 0)
    def _():
        h = _rms(x_ref[...], g_ref[...]).astype(bf16)
        h_ref[...] = h
        os_ref[...] = _dot(h, ws_ref[...])

    o_ref[...] = _dot(h_ref[...], w_ref[...]).astype(bf16)


def _proj(x2, g, w_main, w_small, tm, tn):
    M = x2.shape[0]
    return pl.pallas_call(
        _proj_kernel,
        grid=(M // tm, N_MAIN // tn),
        in_specs=[
            pl.BlockSpec((tm, D), lambda i, j: (i, 0), pipeline_mode=pl.Buffered(1)),
            pl.BlockSpec((1, D), lambda i, j: (0, 0)),
            pl.BlockSpec((D, tn), lambda i, j: (0, j)),
            pl.BlockSpec((D, N_SMALL), lambda i, j: (0, 0), pipeline_mode=pl.Buffered(1)),
        ],
        out_specs=[
            pl.BlockSpec((tm, tn), lambda i, j: (i, j)),
            pl.BlockSpec((tm, N_SMALL), lambda i, j: (i, 0)),
        ],
        out_shape=[jax.ShapeDtypeStruct((M, N_MAIN), bf16),
                   jax.ShapeDtypeStruct((M, N_SMALL), f32)],
        scratch_shapes=[pltpu.VMEM((tm, D), bf16)],
        compiler_params=_cparams(("parallel", "arbitrary")),
        name="proj",
    )(x2, g, w_main, w_small)


def _rope(x, cosf, sinf):
    x = x.astype(f32)
    return x * cosf + pltpu.roll(x, DK // 2, axis=1) * sinf


def _prep_kernel(kc_ref, vc_ref, ks_ref, vs_ref, kw_ref, vw_ref, cos_ref, sin_ref,
                 kco_ref, vco_ref, ksa_ref, vso_ref, kwo_ref, vwo_ref, stage_ref, *, tt):
    cosf = cos_ref[...]
    sinf = sin_ref[...]
    t = pl.program_id(1) * tt + lax.broadcasted_iota(jnp.int32, (tt, NSB), 0)
    j = lax.broadcasted_iota(jnp.int32, (tt, NSB), 1)
    onehot = jnp.where(t // SLC == j, 1.0, 0.0).astype(bf16)
    rows = tt // CMP_STRIDE

    def to_block_rows(x, o_ref, g):
        stage_ref[...] = x
        for l in range(CMP_STRIDE):
            o_ref[g, :, l * DK:(l + 1) * DK] = stage_ref[pl.ds(l, rows, stride=CMP_STRIDE), :].astype(bf16)

    for g in range(G):
        cols = slice(g * DK, (g + 1) * DK)
        to_block_rows(_rope(kc_ref[:, cols], cosf, sinf), kco_ref, g)
        to_block_rows(vc_ref[:, cols].astype(f32), vco_ref, g)
        ksa_ref[g, :, 0:DK] = _rope(ks_ref[:, cols], cosf, sinf).astype(bf16)
        ksa_ref[g, :, DK:DK + NSB] = onehot
        vso_ref[g] = vs_ref[:, cols].astype(f32).T.astype(bf16)
        kwo_ref[g] = _rope(kw_ref[:, cols], cosf, sinf).astype(bf16)
        vwo_ref[g] = vw_ref[:, cols].astype(f32).T.astype(bf16)


def _prep(proj, cosf, sinf, B, T, tt):
    nt = T // tt
    gw = G * DK
    rows = tt // CMP_STRIDE

    def col(name):
        base = _blk(name, gw)
        return pl.BlockSpec((tt, gw), lambda b, i: (b * nt + i, base))

    tab = pl.BlockSpec((tt, DK), lambda b, i: (b * nt + i, 0))

    def out(n, w):
        return pl.BlockSpec((None, G, n, w), lambda b, i: (b, 0, i, 0))

    out_t = pl.BlockSpec((None, G, None, DK, tt), lambda b, i: (b, 0, i, 0, 0))
    return pl.pallas_call(
        functools.partial(_prep_kernel, tt=tt),
        grid=(B, nt),
        in_specs=[col('k_c'), col('v_c'), col('k_s'), col('v_s'), col('k_w'), col('v_w'), tab, tab],
        out_specs=[out(rows, CMP_STRIDE * DK), out(rows, CMP_STRIDE * DK), out(tt, DK + NSB), out_t,
                   out(tt, DK), out_t],
        out_shape=[jax.ShapeDtypeStruct((B, G, T // CMP_STRIDE, CMP_STRIDE * DK), bf16),
                   jax.ShapeDtypeStruct((B, G, T // CMP_STRIDE, CMP_STRIDE * DK), bf16),
                   jax.ShapeDtypeStruct((B, G, T, DK + NSB), bf16),
                   jax.ShapeDtypeStruct((B, G, nt, DK, tt), bf16),
                   jax.ShapeDtypeStruct((B, G, T, DK), bf16),
                   jax.ShapeDtypeStruct((B, G, nt, DK, tt), bf16)],
        scratch_shapes=[pltpu.VMEM((tt, DK), f32)],
        compiler_params=_cparams(("parallel", "parallel")),
        name="kv_prep",
    )(proj, proj, proj, proj, proj, proj, cosf, sinf)


def _compress_one(x_ref, pe_ref, w1_ref, b1_ref, w2_ref, b2_ref, o_ref, transpose):
    nc = x_ref.shape[0]
    half = CMP_STRIDE * DK
    x = x_ref[...]
    first = _dot(x, w1_ref[0:half, :])
    second = _dot(x, w1_ref[half:2 * half, :])
    second = pltpu.roll(second, nc - 1, axis=0)
    pe = _dot(pe_ref[...], w1_ref[...])[0:1, :]
    h = first + second + pe + b1_ref[...]
    h = h * _sigmoid(h)
    y = _dot(h.astype(bf16), w2_ref[...]) + b2_ref[...]
    o_ref[...] = (y.T if transpose else y).astype(bf16)


def _compress_kernel(k_ref, v_ref, pek, w1k, b1k, w2k, b2k, pev, w1v, b1v, w2v, b2v, ko_ref, vo_ref):
    _compress_one(k_ref, pek, w1k, b1k, w2k, b2k, ko_ref, False)
    _compress_one(v_ref, pev, w1v, b1v, w2v, b2v, vo_ref, True)


def _compress(kc2, vc2, wk, wv):
    B, _, nc, width = kc2.shape
    xin = pl.BlockSpec((None, None, nc, width), lambda b, g: (b, g, 0, 0))

    def full(a):
        return pl.BlockSpec(a.shape, lambda b, g: (0,) * a.ndim)

    out = pl.BlockSpec((None, None, nc, DK), lambda b, g: (b, g, 0, 0))
    out_t = pl.BlockSpec((None, None, DK, nc), lambda b, g: (b, g, 0, 0))
    return pl.pallas_call(
        _compress_kernel,
        grid=(B, G),
        in_specs=[xin, xin] + [full(a) for a in wk] + [full(a) for a in wv],
        out_specs=[out, out_t],
        out_shape=[jax.ShapeDtypeStruct((B, G, nc, DK), bf16),
                   jax.ShapeDtypeStruct((B, G, DK, nc), bf16)],
        compiler_params=_cparams(("parallel", "parallel")),
        name="compress",
    )(kc2, vc2, *wk, *wv)


def _nsa_kernel(q_ref, cos_ref, sin_ref, gate_ref, ovl_ref, kcmp_ref, vcmp_ref,
                ksa_ref, vs_ref, kw_ref, vw_ref, o_ref,
                qa_ref, ms_ref, ls_ref, accs_ref, mw_ref, lw_ref, accw_ref, out_ref,
                sc_ref, mx_ref, sd_ref, md_ref,
                *, tq, n_sel):
    qi = pl.program_id(2)
    q0 = qi * tq
    heads = [slice(r * tq, (r + 1) * tq) for r in range(R)]
    qscale = DK ** -0.5 * LOG2E
    cosf = cos_ref[...]
    sinf = sin_ref[...]
    for r in range(R):
        qr = _rope(q_ref[:, r * DK:(r + 1) * DK], cosf, sinf) * qscale
        qa_ref[0:DK, heads[r]] = qr.T.astype(bf16)
    t1 = q0 + lax.broadcasted_iota(jnp.int32, (1, tq), 1)
    gate_t = _sigmoid(gate_ref[...]).T

    def gate_row(c, r):
        return gate_t[3 * r + c:3 * r + c + 1, :]

    nc = kcmp_ref.shape[0]
    n_idx = lax.broadcasted_iota(jnp.int32, (nc, 1), 0)
    vis = (n_idx * CMP_STRIDE + (CMP_BLOCK - 1)) <= t1
    has_any = t1 >= CMP_BLOCK - 1
    psum = jnp.zeros((nc, tq), f32)
    for r in range(R):
        s = jnp.where(vis, _dot(kcmp_ref[...], qa_ref[0:DK, heads[r]]), NEG)
        e = jnp.exp2(s - jnp.max(s, axis=0, keepdims=True))
        inv = jnp.where(has_any, 1.0 / jnp.sum(e, axis=0, keepdims=True), 0.0)
        out_ref[:, heads[r]] = (gate_row(0, r) * inv) * _dot(vcmp_ref[...], e.astype(bf16))
        psum = psum + e * inv
    p_hi, p_lo = _split2(psum)
    imp = _dot(ovl_ref[...], p_hi) + _dot(ovl_ref[...], p_lo)

    def reset(m_ref, l_ref, acc_ref):
        m_ref[...] = jnp.full(m_ref.shape, NEG, f32)
        l_ref[...] = jnp.zeros(l_ref.shape, f32)
        acc_ref[...] = jnp.zeros(acc_ref.shape, f32)

    def scores_to(buf, k_ref, kt, q_rows, mask_fn, lanes, krows):
        n = krows.stop - krows.start
        start = pl.multiple_of(kt * tq + krows.start, n)
        s = _dot(k_ref[pl.ds(start, n), :], qa_ref[q_rows, lanes])
        if mask_fn is not None:
            s = jnp.where(mask_fn(krows, lanes), s, NEG)
        buf[0][krows, lanes] = s
        buf[1][:, lanes] = jnp.max(s, axis=0, keepdims=True)

    def softmax_pv(buf, state, v_ref, kt, lanes, krows):
        m_ref, l_ref, acc_ref = state
        m_prev = m_ref[:, lanes]
        m_new = jnp.maximum(m_prev, buf[1][:, lanes])
        alpha = jnp.exp2(m_prev - m_new)
        pt = jnp.exp2(buf[0][krows, lanes] - m_new)
        l_ref[:, lanes] = alpha * l_ref[:, lanes] + jnp.sum(pt, axis=0, keepdims=True)
        acc_ref[:, lanes] = alpha * acc_ref[:, lanes] + _dot(v_ref[kt, :, krows], pt.astype(bf16))
        m_ref[:, lanes] = m_new

    def finish(state, c):
        m_ref, l_ref, acc_ref = state
        for r in range(R):
            out_ref[:, heads[r]] += (gate_row(c, r) * (1.0 / l_ref[:, heads[r]])) * acc_ref[:, heads[r]]

    plain = slice(0, DK)
    aug = slice(0, DK + NSB)

    buf_a = (sd_ref, md_ref)
    buf_b = (sc_ref, mx_ref)

    half = tq // 2
    every = slice(0, tq)
    full_units = [(heads[r], every) for r in range(R)]

    def half_units(rows_lo, rows_hi):
        return [(slice(r * tq + h * half, r * tq + (h + 1) * half), rows)
                for r in range(R) for h, rows in enumerate((rows_lo, rows_hi))]

    diag_units = half_units(slice(0, half), every)
    prev_units = half_units(every, slice(half, tq))

    def tile_scores(buf, k_ref, kt, q_rows, mask_fn, units):
        for lanes, krows in units:
            scores_to(buf, k_ref, kt, q_rows, mask_fn, lanes, krows)

    def tile_softmax(buf, state, v_ref, kt, units):
        for lanes, krows in units:
            softmax_pv(buf, state, v_ref, kt, lanes, krows)

    kpos = lax.broadcasted_iota(jnp.int32, (tq, 1), 0)
    t_all = q0 + (lax.broadcasted_iota(jnp.int32, (1, R * tq), 1) & (tq - 1))

    def causal(krows, lanes):
        return q0 + kpos[krows] <= t_all[:, lanes]

    def inside(krows, lanes):
        return ((q0 - tq) + kpos[krows] > t_all[:, lanes] - WINDOW) & (qi >= 1)

    win = (mw_ref, lw_ref, accw_ref)
    reset(*win)
    prev = jnp.maximum(qi - 1, 0)
    tile_scores(buf_a, kw_ref, qi, plain, causal, diag_units)
    tile_scores(buf_b, kw_ref, prev, plain, inside, prev_units)
    tile_softmax(buf_a, win, vw_ref, qi, diag_units)
    tile_softmax(buf_b, win, vw_ref, prev, prev_units)
    finish(win, 2)

    j_idx = lax.broadcasted_iota(jnp.int32, (NSB, 1), 0)
    j_f = j_idx.astype(f32)
    cur = t1 // SLC
    valid = (j_idx * SLC) <= t1
    forced = (j_idx == 0) | (j_idx == cur) | (j_idx == cur - 1)
    score = jnp.where(valid, imp + jnp.where(forced, FORCE_BONUS, 0.0), -jnp.inf)
    sel = jnp.zeros((NSB, tq), f32)
    for _ in range(n_sel):
        mx = jnp.max(score, axis=0, keepdims=True)
        first = jnp.min(jnp.where(score == mx, j_f, float(NSB)), axis=0, keepdims=True)
        hit = j_f == first
        sel = jnp.where(hit, 1.0, sel)
        score = jnp.where(hit, -jnp.inf, score)
    nsel = jnp.where(sel > 0.5, 0.0, NEG).astype(bf16)
    for r in range(R):
        qa_ref[DK:DK + NSB, heads[r]] = nsel

    selst = (ms_ref, ls_ref, accs_ref)
    reset(*selst)
    tile_scores(buf_a, ksa_ref, qi, aug, causal, diag_units)
    tile_scores(buf_b, ksa_ref, 0, aug, None, full_units)
    tile_softmax(buf_a, selst, vs_ref, qi, diag_units)

    def sel_body(kt, carry):
        nxt = jnp.minimum(kt + 1, jnp.maximum(qi - 1, 0))
        for lanes, krows in full_units:
            softmax_pv(buf_b, selst, vs_ref, kt, lanes, krows)
            scores_to(buf_b, ksa_ref, nxt, aug, None, lanes, krows)
        return carry

    lax.fori_loop(0, qi, sel_body, 0)
    finish(selst, 1)

    for r in range(R):
        o_ref[:, r * DK:(r + 1) * DK] = out_ref[:, heads[r]].T.astype(bf16)


def _nsa(proj, small, cosf, sinf, ovl_t, kcmp, vcmp_t, ksa, vs_t, kw, vw_t, B, T, tq):
    nq = T // tq
    nc = kcmp.shape[2]
    rows = R * tq
    n_sel = min(N_SELECT, T // SLC)
    assert tq >= WINDOW and nq >= 2

    def per_bg(*shape):
        return pl.BlockSpec((None, None) + shape, lambda b, g, i: (b, g) + (0,) * len(shape))

    tab = pl.BlockSpec((tq, DK), lambda b, g, i: (b * nq + i, 0))
    return pl.pallas_call(
        functools.partial(_nsa_kernel, tq=tq, n_sel=n_sel),
        grid=(B, G, nq),
        in_specs=[
            pl.BlockSpec((tq, R * DK), lambda b, g, i: (b * nq + i, g)),
            tab, tab,
            pl.BlockSpec((tq, 128), lambda b, g, i: (b * nq + i, g)),
            pl.BlockSpec((NSB, nc), lambda b, g, i: (0, 0)),
            per_bg(nc, DK), per_bg(DK, nc),
            per_bg(T, DK + NSB), per_bg(nq, DK, tq), per_bg(T, DK), per_bg(nq, DK, tq),
        ],
        out_specs=pl.BlockSpec((tq, R * DK), lambda b, g, i: (b * nq + i, g)),
        out_shape=jax.ShapeDtypeStruct((B * T, D), bf16),
        scratch_shapes=[pltpu.VMEM((DK + NSB, rows), bf16)]
        + [pltpu.VMEM((1, rows), f32), pltpu.VMEM((1, rows), f32), pltpu.VMEM((DK, rows), f32)] * 2
        + [pltpu.VMEM((DK, rows), f32)]
        + [pltpu.VMEM((tq, rows), f32), pltpu.VMEM((1, rows), f32)] * 2,
        compiler_params=_cparams(("parallel", "parallel", "arbitrary")),
        name="nsa_attention",
    )(proj, cosf, sinf, small, ovl_t, kcmp, vcmp_t, ksa, vs_t, kw, vw_t)


def _ssd_kernel(z_ref, xs_ref, bm_ref, cm_ref, dt_ref,
                wx_ref, bx_ref, wb_ref, bb_ref, wc_ref, bc_ref,
                dtb_ref, alog_ref, dskip_ref, ng_ref, eh_ref, shift_ref,
                o_ref, st_ref, prev_ref, y_ref):
    c = pl.program_id(1)
    L = SSD_L
    HP = 2 * SSD_P
    GW = D // SSD_G

    @pl.when(c == 0)
    def _():
        st_ref[...] = jnp.zeros_like(st_ref)
        prev_ref[...] = jnp.zeros_like(prev_ref)

    def conv_silu(u_ref, col, w_ref, b_ref):
        width = u_ref.shape[1]
        cur = u_ref[...]
        ext = jnp.concatenate([prev_ref[:, col:col + width], cur], axis=0)
        sh = _dot(shift_ref[...], ext)
        y = b_ref[...] + w_ref[SSD_CONV - 1:SSD_CONV, :] * cur.astype(f32)
        for j in range(SSD_CONV - 1):
            y = y + w_ref[j:j + 1, :] * sh[j * L:(j + 1) * L]
        prev_ref[:, col:col + width] = cur
        return y * _sigmoid(y)

    xs = conv_silu(xs_ref, 0, wx_ref, bx_ref)
    bm = conv_silu(bm_ref, D, wb_ref, bb_ref)
    cm = conv_silu(cm_ref, D + SSD_G * SSD_N, wc_ref, bc_ref)

    lane = lax.broadcasted_iota(jnp.int32, (1, 128), 1)
    xdt_in = dt_ref[...] + dtb_ref[...]
    dt = jnp.maximum(xdt_in, 0.0) + jnp.log1p(jnp.exp(-jnp.abs(xdt_in)))
    dt = jnp.where(lane < SSD_HEADS, dt, 0.0)
    da = dt * (-jnp.exp(alog_ref[...]))

    row = lax.broadcasted_iota(jnp.int32, (L, L), 0)
    colm = lax.broadcasted_iota(jnp.int32, (L, L), 1)
    causal = colm <= row
    tril = jnp.where(causal, 1.0, 0.0).astype(bf16)
    d_hi, d_mid, d_lo = _split3(da)
    acs = _dot(tril, d_hi) + _dot(tril, d_mid) + _dot(tril, d_lo)
    acs_t = acs.T
    last = acs[L - 1:L, :]
    eacs = jnp.exp(acs)
    dec = jnp.exp(last - acs)
    cdec = jnp.exp(jnp.broadcast_to(last, (8, 128)))

    eh = eh_ref[...]
    dt_x = _dot(dt.astype(bf16), eh)
    eacs_x = _dot(eacs.astype(bf16), eh)
    dec_x = _dot(dec.astype(bf16), eh)
    c_hi, c_lo = _split2(cdec)
    cdec_x = (_dot(c_hi, eh) + _dot(c_lo, eh))[0:1, :]

    xdt = xs * dt_x
    xdt_b = xdt.astype(bf16)
    xdec_b = (xdt * dec_x).astype(bf16)
    lane_hp = lax.broadcasted_iota(jnp.int32, (L, HP), 1)

    for g in range(SSD_G):
        cm_g = cm[:, g * SSD_N:(g + 1) * SSD_N].astype(bf16)
        bm_g = bm[:, g * SSD_N:(g + 1) * SSD_N]
        cb = _dot_nt(cm_g, bm_g.astype(bf16))
        s_in = st_ref[g]
        y_off = _dot(cm_g, s_in.astype(bf16))
        s_new = _dot(bm_g.T.astype(bf16), xdec_b[:, g * GW:(g + 1) * GW])
        st_ref[g] = s_in * cdec_x[:, g * GW:(g + 1) * GW] + s_new
        for pp in range(GW // HP):
            h0 = g * (GW // SSD_P) + 2 * pp
            c0 = g * GW + pp * HP
            ms = []
            for h in (h0, h0 + 1):
                diff = acs[:, h:h + 1] - acs_t[h:h + 1, :]
                ms.append(jnp.where(causal, cb * jnp.exp(jnp.where(causal, diff, NEG)), 0.0))
            lhs = jnp.concatenate(ms, axis=1).astype(bf16)
            xp = xdt_b[:, c0:c0 + HP]
            zero = jnp.zeros_like(xp)
            rhs = jnp.concatenate([jnp.where(lane_hp < SSD_P, xp, zero),
                                   jnp.where(lane_hp >= SSD_P, xp, zero)], axis=0)
            y = (_dot(lhs, rhs) + eacs_x[:, c0:c0 + HP] * y_off[:, pp * HP:(pp + 1) * HP]
                 + dskip_ref[:, c0:c0 + HP] * xs[:, c0:c0 + HP])
            y_ref[:, c0:c0 + HP] = y

    z = z_ref[...].astype(f32)
    y = y_ref[...] * (z * _sigmoid(z))
    for g in range(SSD_G):
        yg = y[:, g * GW:(g + 1) * GW]
        ms = jnp.mean(yg * yg, axis=-1, keepdims=True)
        o_ref[:, g * GW:(g + 1) * GW] = (yg * lax.rsqrt(ms + EPS) * ng_ref[:, g * GW:(g + 1) * GW]).astype(bf16)


def _ssd(proj, small, wts, B, T):
    nch = T // SSD_L
    L = SSD_L

    def main(name, w):
        base = _blk(name, w)
        return pl.BlockSpec((L, w), lambda b, c: (b * nch + c, base))

    def full(a):
        return pl.BlockSpec(a.shape, lambda b, c: (0,) * a.ndim)

    gn = SSD_G * SSD_N
    return pl.pallas_call(
        _ssd_kernel,
        grid=(B, nch),
        in_specs=[main('s_z', D), main('xs', D), main('bm', gn), main('cm', gn),
                  pl.BlockSpec((L, 128), lambda b, c: (b * nch + c, G))] + [full(a) for a in wts],
        out_specs=pl.BlockSpec((L, D), lambda b, c: (b * nch + c, 0)),
        out_shape=jax.ShapeDtypeStruct((B * T, D), bf16),
        scratch_shapes=[pltpu.VMEM((SSD_G, SSD_N, D // SSD_G), f32),
                        pltpu.VMEM((L, D + 2 * gn), bf16),
                        pltpu.VMEM((L, D), f32)],
        compiler_params=_cparams(("parallel", "arbitrary")),
        name="ssd",
    )(proj, proj, proj, proj, small, *wts)


def _merge_kernel(ya_ref, cb_ref, cc_ref, cu_ref, cch_ref, cuh_ref, yc_ref,
                  g0_ref, g1_ref, g2_ref, x_ref, cw_ref, wo_ref, o_ref, ext_ref, *, tm, tiles_per_seq):
    first = (pl.program_id(0) % tiles_per_seq) == 0
    up = lambda ref: ref[...].astype(f32)
    ext_ref[0:HALO, :] = jnp.where(first, 0.0, up(cch_ref) * up(cuh_ref))
    ext_ref[HALO:HALO + tm, :] = up(cc_ref) * up(cu_ref)
    lo = HALO - (CONV_W - 1)
    conv = cw_ref[0:1, :] * ext_ref[lo:lo + tm, :]
    for j in range(1, CONV_W):
        conv = conv + cw_ref[j:j + 1, :] * ext_ref[lo + j:lo + j + tm, :]
    yb = up(cb_ref) * conv
    merged = (_sigmoid(up(g0_ref)) * up(ya_ref) + _sigmoid(up(g1_ref)) * yb
              + _sigmoid(up(g2_ref)) * up(yc_ref))
    o_ref[...] = x_ref[...] + _dot(merged.astype(bf16), wo_ref[...])


def _merge(ya, yc, proj, x2, cw, wo, T, tm):
    M = x2.shape[0]

    def col(name):
        base = _blk(name, D)
        return pl.BlockSpec((tm, D), lambda i: (i, base))

    def halo(name):
        base = _blk(name, D)
        return pl.BlockSpec((HALO, D), lambda i: (jnp.maximum(i * (tm // HALO) - 1, 0), base))

    row = pl.BlockSpec((tm, D), lambda i: (i, 0))
    return pl.pallas_call(
        functools.partial(_merge_kernel, tm=tm, tiles_per_seq=T // tm),
        grid=(M // tm,),
        in_specs=[row, col('cb'), col('cc'), col('cu'), halo('cc'), halo('cu'), row,
                  col('gm0'), col('gm1'), col('gm2'), row,
                  pl.BlockSpec((CONV_W, D), lambda i: (0, 0)),
                  pl.BlockSpec((D, D), lambda i: (0, 0))],
        out_specs=row,
        out_shape=jax.ShapeDtypeStruct((M, D), f32),
        scratch_shapes=[pltpu.VMEM((HALO + tm, D), f32)],
        compiler_params=_cparams(("parallel",)),
        name="merge_oproj",
    )(ya, proj, proj, proj, proj, proj, yc, proj, proj, proj, x2, cw, wo)


def _ffn_kernel(x_ref, g_ref, wu_ref, wd_ref, o_ref, h_ref):
    @pl.when(pl.program_id(1) == 0)
    def _():
        x = x_ref[...]
        h_ref[...] = _rms(x, g_ref[...]).astype(bf16)
        o_ref[...] = x

    u = jnp.maximum(_dot(h_ref[...], wu_ref[...]), 0.0)
    o_ref[...] += _dot((u * u).astype(bf16), wd_ref[...])


def _ffn(x2, g, wu, wd, tm, tf):
    M = x2.shape[0]
    return pl.pallas_call(
        _ffn_kernel,
        grid=(M // tm, D_FF // tf),
        in_specs=[pl.BlockSpec((tm, D), lambda i, j: (i, 0)),
                  pl.BlockSpec((1, D), lambda i, j: (0, 0)),
                  pl.BlockSpec((D, tf), lambda i, j: (0, j)),
                  pl.BlockSpec((tf, D), lambda i, j: (j, 0))],
        out_specs=pl.BlockSpec((tm, D), lambda i, j: (i, 0)),
        out_shape=jax.ShapeDtypeStruct((M, D), f32),
        scratch_shapes=[pltpu.VMEM((tm, D), bf16)],
        compiler_params=_cparams(("parallel", "arbitrary")),
        name="ffn",
    )(x2, g, wu, wd)


def _ple_kernel(x_ref, p_ref, g_ref, wp_ref, wg_ref, gf_ref, o_ref, *, final):
    x = x_ref[...]
    gate = _sigmoid(_dot(_rms(x, g_ref[...]).astype(bf16), wg_ref[...]))
    y = x + _dot(p_ref[...].astype(bf16), wp_ref[...]) * gate
    if final:
        y = _rms(y, gf_ref[...])
    o_ref[...] = y


def _ple(x2, p2, layer, g, wp, wg, gf, tm, final):
    M = x2.shape[0]
    base = layer * (M // tm)
    row = pl.BlockSpec((tm, D), lambda i: (i, 0))
    vec = pl.BlockSpec((1, D), lambda i: (0, 0))
    return pl.pallas_call(
        functools.partial(_ple_kernel, final=final),
        grid=(M // tm,),
        in_specs=[row, pl.BlockSpec((tm, PLE), lambda i: (base + i, 0)), vec,
                  pl.BlockSpec((PLE, D), lambda i: (0, 0)),
                  pl.BlockSpec((D, D), lambda i: (0, 0)), vec],
        out_specs=row,
        out_shape=jax.ShapeDtypeStruct((M, D), f32),
        compiler_params=_cparams(("parallel",)),
        name="ple",
    )(x2, p2, g, wp, wg, gf)


def _source_runs():
    runs = []
    for n in _ORDER:
        a, wd = _SRC[n]
        if runs and runs[-1][1] == a:
            runs[-1][1] = a + wd
        else:
            runs.append([a, a + wd])
    return runs


def _prep_w_in(w):
    depth = w.shape[0]
    main = jnp.concatenate([w[:, :, a:b] for a, b in _source_runs()], axis=2)
    a, wd = _SRC['g_nsa']
    gn = w[:, :, a:a + wd].reshape(depth, D, G, R * 3)
    gn = jnp.pad(gn, ((0, 0), (0, 0), (0, 0), (0, 128 - R * 3))).reshape(depth, D, G * 128)
    a, wd = _SRC['s_dt']
    dt = jnp.pad(w[:, :, a:a + wd], ((0, 0), (0, 0), (0, 128 - wd)))
    return main.astype(bf16), jnp.concatenate([gn, dt], axis=2).astype(bf16)


def _overlap_matrix(nc):
    i = np.arange(nc)[:, None]
    j = np.arange(NSB)[None, :]
    ovl = (i * CMP_STRIDE < j * SLC + SLC) & (i * CMP_STRIDE + CMP_BLOCK > j * SLC)
    return jnp.asarray(ovl.T.astype(np.float32), dtype=bf16)


def _head_expand():
    h = np.arange(128)[:, None]
    ch = np.arange(D)[None, :]
    return jnp.asarray((ch // SSD_P == h).astype(np.float32), dtype=bf16)


def _conv_shift():
    i = np.arange((SSD_CONV - 1) * SSD_L)[:, None]
    r = np.arange(2 * SSD_L)[None, :]
    pick = r == SSD_L + i % SSD_L - (SSD_CONV - 1) + i // SSD_L
    return jnp.asarray(pick.astype(np.float32), dtype=bf16)


def kernel(x, p, positions, g_mix, w_in, nsa_pe_k, nsa_pe_v, phi_k_w1, phi_k_b1, phi_k_w2, phi_k_b2,
           phi_v_w1, phi_v_b1, phi_v_w2, phi_v_b2, sconv_w, ssd_conv_w, ssd_conv_b, ssd_dt_bias,
           ssd_a_log, ssd_d, ssd_norm_g, w_o, g_mlp, w_up, w_down, g_ple, w_ple, w_ple_gate, g_final):
    B, T, _ = x.shape
    depth = w_in.shape[0]
    M = B * T
    assert T % 256 == 0 and T // SLC <= NSB
    tq = 512
    tm_proj = 1024 if M % 1024 == 0 else 256
    tm = 512 if T % 512 == 0 else 256
    nc = T // CMP_STRIDE

    inv_freq = 1.0 / (10000.0 ** (jnp.arange(0, DK, 2, dtype=f32) / DK))
    ang = positions.astype(f32)[..., None] * inv_freq
    cosf = jnp.concatenate([jnp.cos(ang), jnp.cos(ang)], axis=-1).reshape(M, DK)
    sinf = jnp.concatenate([-jnp.sin(ang), jnp.sin(ang)], axis=-1).reshape(M, DK)
    ovl = _overlap_matrix(nc)
    eh = _head_expand()
    shift = _conv_shift()

    def vec(a, n=None):
        a = a.reshape(1, -1).astype(f32)
        return a if n is None else jnp.pad(a, ((0, 0), (0, n - a.shape[1])))

    x2 = x.reshape(M, D)
    p2 = p.reshape(depth * M, PLE)
    w_main, w_small = _prep_w_in(w_in)
    for i in range(depth):
        proj, small = _proj(x2, vec(g_mix[i]), w_main[i], w_small[i], tm_proj, 2048)

        kc, vc, ksa, vs, kw, vw = _prep(proj, cosf, sinf, B, T, tq)

        def phi(pe, w1, b1, w2, b2):
            pe8 = jnp.broadcast_to(pe.reshape(1, -1), (8, CMP_BLOCK * DK)).astype(bf16)
            return (pe8, w1.astype(bf16), vec(b1), w2.astype(bf16), vec(b2))

        kcmp, vcmp = _compress(kc, vc,
                               phi(nsa_pe_k[i], phi_k_w1[i], phi_k_b1[i], phi_k_w2[i], phi_k_b2[i]),
                               phi(nsa_pe_v[i], phi_v_w1[i], phi_v_b1[i], phi_v_w2[i], phi_v_b2[i]))
        ya = _nsa(proj, small, cosf, sinf, ovl, kcmp, vcmp, ksa, vs, kw, vw, B, T, tq)

        cw, cbias = ssd_conv_w[i], ssd_conv_b[i]
        gn = SSD_G * SSD_N
        ssd_w = (cw[:, :D], vec(cbias[:D]), cw[:, D:D + gn], vec(cbias[D:D + gn]),
                 cw[:, D + gn:], vec(cbias[D + gn:]),
                 vec(ssd_dt_bias[i], 128), vec(ssd_a_log[i], 128),
                 vec(jnp.repeat(ssd_d[i], SSD_P)), vec(ssd_norm_g[i]), eh, shift)
        yc = _ssd(proj, small, ssd_w, B, T)

        x2 = _merge(ya, yc, proj, x2, sconv_w[i], w_o[i].astype(bf16), T, 256)
        x2 = _ffn(x2, vec(g_mlp[i]), w_up[i].astype(bf16), w_down[i].astype(bf16), tm_proj, 512)
        x2 = _ple(x2, p2, i, vec(g_ple[i]), w_ple[i].astype(bf16),
                  w_ple_gate[i].astype(bf16), vec(g_final), tm, final=(i == depth - 1))
    return x2.reshape(B, T, D)
```

```python
import functools

import numpy as np
import jax
import jax.numpy as jnp
from jax import lax
from jax.experimental import pallas as pl
from jax.experimental.pallas import tpu as pltpu

f32 = jnp.float32
bf16 = jnp.bfloat16

D = 2048
N_HEADS = 16
DK = 128
G = 4
R = N_HEADS // G
CMP_BLOCK = 32
CMP_STRIDE = 16
SLC = 64
N_SELECT = 16
WINDOW = 512
FORCE_BONUS = 1.0e4
CONV_W = 3
SSD_HEADS = 32
SSD_P = 64
SSD_G = 4
SSD_N = 128
SSD_CONV = 4
SSD_L = 128
D_FF = 4 * D
PLE = 256
EPS = 1e-6
NSB = 128
HALO = 16
NEG = -1e30
LOG2E = 1.4426950408889634

VMEM_LIMIT = 56 * 1024 * 1024

_SRC = {
    'q': (0, 2048), 'k_c': (2048, 512), 'v_c': (2560, 512), 'k_s': (3072, 512), 'v_s': (3584, 512),
    'k_w': (4096, 512), 'v_w': (4608, 512), 'g_nsa': (5120, 48),
    'cb': (5168, 2048), 'cc': (7216, 2048), 'cu': (9264, 2048),
    's_z': (11312, 2048), 'xs': (13360, 2048), 'bm': (15408, 512), 'cm': (15920, 512),
    's_dt': (16432, 32), 'gm0': (16464, 2048), 'gm1': (18512, 2048), 'gm2': (20560, 2048),
}
_ORDER = ['q', 'cb', 'cc', 'cu', 's_z', 'xs', 'gm0', 'gm1', 'gm2',
          'k_c', 'v_c', 'k_s', 'v_s', 'k_w', 'v_w', 'bm', 'cm']
_OFF = {}
_o = 0
for _n in _ORDER:
    _OFF[_n] = _o
    _o += _SRC[_n][1]
N_MAIN = _o
N_SMALL = G * 128 + 128


def _blk(name, width):
    off = _OFF[name]
    assert off % width == 0
    return off // width


def _cparams(sem):
    return pltpu.CompilerParams(dimension_semantics=sem, vmem_limit_bytes=VMEM_LIMIT)


def _dot(a, b):
    return jnp.dot(a, b, preferred_element_type=f32)


def _dot_nt(a, b):
    return lax.dot_general(a, b, (((1,), (1,)), ((), ())), preferred_element_type=f32)


def _sigmoid(x):
    return jax.nn.sigmoid(x)


def _split2(x):
    hi = x.astype(bf16)
    lo = (x - hi.astype(f32)).astype(bf16)
    return hi, lo


def _split3(x):
    hi = x.astype(bf16)
    r1 = x - hi.astype(f32)
    mid = r1.astype(bf16)
    lo = (r1 - mid.astype(f32)).astype(bf16)
    return hi, mid, lo


def _rms(x, g):
    ms = jnp.mean(x * x, axis=-1, keepdims=True)
    return x * lax.rsqrt(ms + EPS) * g


def _proj_kernel(x_ref, g_ref, w_ref, ws_ref, o_ref, os_ref, h_ref):
    @pl.when(pl.program_id(1) == 0)
    def _():
        h = _rms(x_ref[...], g_ref[...]).astype(bf16)
        h_ref[...] = h
        os_ref[...] = _dot(h, ws_ref[...])

    o_ref[...] = _dot(h_ref[...], w_ref[...]).astype(bf16)


def _proj(x2, g, w_main, w_small, tm, tn):
    M = x2.shape[0]
    return pl.pallas_call(
        _proj_kernel,
        grid=(M // tm, N_MAIN // tn),
        in_specs=[
            pl.BlockSpec((tm, D), lambda i, j: (i, 0), pipeline_mode=pl.Buffered(1)),
            pl.BlockSpec((1, D), lambda i, j: (0, 0)),
            pl.BlockSpec((D, tn), lambda i, j: (0, j)),
            pl.BlockSpec((D, N_SMALL), lambda i, j: (0, 0), pipeline_mode=pl.Buffered(1)),
        ],
        out_specs=[
            pl.BlockSpec((tm, tn), lambda i, j: (i, j)),
            pl.BlockSpec((tm, N_SMALL), lambda i, j: (i, 0)),
        ],
        out_shape=[jax.ShapeDtypeStruct((M, N_MAIN), bf16),
                   jax.ShapeDtypeStruct((M, N_SMALL), f32)],
        scratch_shapes=[pltpu.VMEM((tm, D), bf16)],
        compiler_params=_cparams(("parallel", "arbitrary")),
        name="proj",
    )(x2, g, w_main, w_small)


def _rope(x, cosf, sinf):
    x = x.astype(f32)
    return x * cosf + pltpu.roll(x, DK // 2, axis=1) * sinf


def _prep_kernel(kc_ref, vc_ref, ks_ref, vs_ref, kw_ref, vw_ref, cos_ref, sin_ref,
                 kco_ref, vco_ref, ksa_ref, vso_ref, kwo_ref, vwo_ref, stage_ref, *, tt):
    cosf = cos_ref[...]
    sinf = sin_ref[...]
    t = pl.program_id(1) * tt + lax.broadcasted_iota(jnp.int32, (tt, NSB), 0)
    j = lax.broadcasted_iota(jnp.int32, (tt, NSB), 1)
    onehot = jnp.where(t // SLC == j, 1.0, 0.0).astype(bf16)
    rows = tt // CMP_STRIDE

    def to_block_rows(x, o_ref, g):
        stage_ref[...] = x
        for l in range(CMP_STRIDE):
            o_ref[g, :, l * DK:(l + 1) * DK] = stage_ref[pl.ds(l, rows, stride=CMP_STRIDE), :].astype(bf16)

    for g in range(G):
        cols = slice(g * DK, (g + 1) * DK)
        to_block_rows(_rope(kc_ref[:, cols], cosf, sinf), kco_ref, g)
        to_block_rows(vc_ref[:, cols].astype(f32), vco_ref, g)
        ksa_ref[g, :, 0:DK] = _rope(ks_ref[:, cols], cosf, sinf).astype(bf16)
        ksa_ref[g, :, DK:DK + NSB] = onehot
        vso_ref[g] = vs_ref[:, cols].astype(f32).T.astype(bf16)
        kwo_ref[g] = _rope(kw_ref[:, cols], cosf, sinf).astype(bf16)
        vwo_ref[g] = vw_ref[:, cols].astype(f32).T.astype(bf16)


def _prep(proj, cosf, sinf, B, T, tt):
    nt = T // tt
    gw = G * DK
    rows = tt // CMP_STRIDE

    def col(name):
        base = _blk(name, gw)
        return pl.BlockSpec((tt, gw), lambda b, i: (b * nt + i, base))

    tab = pl.BlockSpec((tt, DK), lambda b, i: (b * nt + i, 0))

    def out(n, w):
        return pl.BlockSpec((None, G, n, w), lambda b, i: (b, 0, i, 0))

    out_t = pl.BlockSpec((None, G, None, DK, tt), lambda b, i: (b, 0, i, 0, 0))
    return pl.pallas_call(
        functools.partial(_prep_kernel, tt=tt),
        grid=(B, nt),
        in_specs=[col('k_c'), col('v_c'), col('k_s'), col('v_s'), col('k_w'), col('v_w'), tab, tab],
        out_specs=[out(rows, CMP_STRIDE * DK), out(rows, CMP_STRIDE * DK), out(tt, DK + NSB), out_t,
                   out(tt, DK), out_t],
        out_shape=[jax.ShapeDtypeStruct((B, G, T // CMP_STRIDE, CMP_STRIDE * DK), bf16),
                   jax.ShapeDtypeStruct((B, G, T // CMP_STRIDE, CMP_STRIDE * DK), bf16),
                   jax.ShapeDtypeStruct((B, G, T, DK + NSB), bf16),
                   jax.ShapeDtypeStruct((B, G, nt, DK, tt), bf16),
                   jax.ShapeDtypeStruct((B, G, T, DK), bf16),
                   jax.ShapeDtypeStruct((B, G, nt, DK, tt), bf16)],
        scratch_shapes=[pltpu.VMEM((tt, DK), f32)],
        compiler_params=_cparams(("parallel", "parallel")),
        name="kv_prep",
    )(proj, proj, proj, proj, proj, proj, cosf, sinf)


def _compress_one(x_ref, pe_ref, w1_ref, b1_ref, w2_ref, b2_ref, o_ref, transpose):
    nc = x_ref.shape[0]
    half = CMP_STRIDE * DK
    x = x_ref[...]
    first = _dot(x, w1_ref[0:half, :])
    second = _dot(x, w1_ref[half:2 * half, :])
    second = pltpu.roll(second, nc - 1, axis=0)
    pe = _dot(pe_ref[...], w1_ref[...])[0:1, :]
    h = first + second + pe + b1_ref[...]
    h = h * _sigmoid(h)
    y = _dot(h.astype(bf16), w2_ref[...]) + b2_ref[...]
    o_ref[...] = (y.T if transpose else y).astype(bf16)


def _compress_kernel(k_ref, v_ref, pek, w1k, b1k, w2k, b2k, pev, w1v, b1v, w2v, b2v, ko_ref, vo_ref):
    _compress_one(k_ref, pek, w1k, b1k, w2k, b2k, ko_ref, False)
    _compress_one(v_ref, pev, w1v, b1v, w2v, b2v, vo_ref, True)


def _compress(kc2, vc2, wk, wv):
    B, _, nc, width = kc2.shape
    xin = pl.BlockSpec((None, None, nc, width), lambda b, g: (b, g, 0, 0))

    def full(a):
        return pl.BlockSpec(a.shape, lambda b, g: (0,) * a.ndim)

    out = pl.BlockSpec((None, None, nc, DK), lambda b, g: (b, g, 0, 0))
    out_t = pl.BlockSpec((None, None, DK, nc), lambda b, g: (b, g, 0, 0))
    return pl.pallas_call(
        _compress_kernel,
        grid=(B, G),
        in_specs=[xin, xin] + [full(a) for a in wk] + [full(a) for a in wv],
        out_specs=[out, out_t],
        out_shape=[jax.ShapeDtypeStruct((B, G, nc, DK), bf16),
                   jax.ShapeDtypeStruct((B, G, DK, nc), bf16)],
        compiler_params=_cparams(("parallel", "parallel")),
        name="compress",
    )(kc2, vc2, *wk, *wv)


def _nsa_kernel(q_ref, cos_ref, sin_ref, gate_ref, ovl_ref, kcmp_ref, vcmp_ref,
                ksa_ref, vs_ref, kw_ref, vw_ref, o_ref,
                qa_ref, ms_ref, ls_ref, accs_ref, mw_ref, lw_ref, accw_ref, out_ref,
                sc_ref, mx_ref, sd_ref, md_ref,
                *, tq, n_sel):
    qi = pl.program_id(2)
    q0 = qi * tq
    heads = [slice(r * tq, (r + 1) * tq) for r in range(R)]
    qscale = DK ** -0.5 * LOG2E
    cosf = cos_ref[...]
    sinf = sin_ref[...]
    for r in range(R):
        qr = _rope(q_ref[:, r * DK:(r + 1) * DK], cosf, sinf) * qscale
        qa_ref[0:DK, heads[r]] = qr.T.astype(bf16)
    t1 = q0 + lax.broadcasted_iota(jnp.int32, (1, tq), 1)
    gate_t = _sigmoid(gate_ref[...]).T

    def gate_row(c, r):
        return gate_t[3 * r + c:3 * r + c + 1, :]

    nc = kcmp_ref.shape[0]
    n_idx = lax.broadcasted_iota(jnp.int32, (nc, 1), 0)
    vis = (n_idx * CMP_STRIDE + (CMP_BLOCK - 1)) <= t1
    has_any = t1 >= CMP_BLOCK - 1
    psum = jnp.zeros((nc, tq), f32)
    for r in range(R):
        s = jnp.where(vis, _dot(kcmp_ref[...], qa_ref[0:DK, heads[r]]), NEG)
        e = jnp.exp2(s - jnp.max(s, axis=0, keepdims=True))
        inv = jnp.where(has_any, 1.0 / jnp.sum(e, axis=0, keepdims=True), 0.0)
        out_ref[:, heads[r]] = (gate_row(0, r) * inv) * _dot(vcmp_ref[...], e.astype(bf16))
        psum = psum + e * inv
    p_hi, p_lo = _split2(psum)
    imp = _dot(ovl_ref[...], p_hi) + _dot(ovl_ref[...], p_lo)

    def reset(m_ref, l_ref, acc_ref):
        m_ref[...] = jnp.full(m_ref.shape, NEG, f32)
        l_ref[...] = jnp.zeros(l_ref.shape, f32)
        acc_ref[...] = jnp.zeros(acc_ref.shape, f32)

    def scores_to(buf, k_ref, kt, q_rows, mask_fn, lanes, krows):
        n = krows.stop - krows.start
        start = pl.multiple_of(kt * tq + krows.start, n)
        s = _dot(k_ref[pl.ds(start, n), :], qa_ref[q_rows, lanes])
        if mask_fn is not None:
            s = jnp.where(mask_fn(krows, lanes), s, NEG)
        buf[0][krows, lanes] = s
        buf[1][:, lanes] = jnp.max(s, axis=0, keepdims=True)

    def softmax_pv(buf, state, v_ref, kt, lanes, krows):
        m_ref, l_ref, acc_ref = state
        m_prev = m_ref[:, lanes]
        m_new = jnp.maximum(m_prev, buf[1][:, lanes])
        alpha = jnp.exp2(m_prev - m_new)
        pt = jnp.exp2(buf[0][krows, lanes] - m_new)
        l_ref[:, lanes] = alpha * l_ref[:, lanes] + jnp.sum(pt, axis=0, keepdims=True)
        acc_ref[:, lanes] = alpha * acc_ref[:, lanes] + _dot(v_ref[kt, :, krows], pt.astype(bf16))
        m_ref[:, lanes] = m_new

    def finish(state, c):
        m_ref, l_ref, acc_ref = state
        for r in range(R):
            out_ref[:, heads[r]] += (gate_row(c, r) * (1.0 / l_ref[:, heads[r]])) * acc_ref[:, heads[r]]

    plain = slice(0, DK)
    aug = slice(0, DK + NSB)

    buf_a = (sd_ref, md_ref)
    buf_b = (sc_ref, mx_ref)

    half = tq // 2
    every = slice(0, tq)
    full_units = [(heads[r], every) for r in range(R)]

    def half_units(rows_lo, rows_hi):
        return [(slice(r * tq + h * half, r * tq + (h + 1) * half), rows)
                for r in range(R) for h, rows in enumerate((rows_lo, rows_hi))]

    diag_units = half_units(slice(0, half), every)
    prev_units = half_units(every, slice(half, tq))

    def tile_scores(buf, k_ref, kt, q_rows, mask_fn, units):
        for lanes, krows in units:
            scores_to(buf, k_ref, kt, q_rows, mask_fn, lanes, krows)

    def tile_softmax(buf, state, v_ref, kt, units):
        for lanes, krows in units:
            softmax_pv(buf, state, v_ref, kt, lanes, krows)

    kpos = lax.broadcasted_iota(jnp.int32, (tq, 1), 0)
    t_all = q0 + (lax.broadcasted_iota(jnp.int32, (1, R * tq), 1) & (tq - 1))

    def causal(krows, lanes):
        return q0 + kpos[krows] <= t_all[:, lanes]

    def inside(krows, lanes):
        return ((q0 - tq) + kpos[krows] > t_all[:, lanes] - WINDOW) & (qi >= 1)

    win = (mw_ref, lw_ref, accw_ref)
    reset(*win)
    prev = jnp.maximum(qi - 1, 0)
    tile_scores(buf_a, kw_ref, qi, plain, causal, diag_units)
    tile_scores(buf_b, kw_ref, prev, plain, inside, prev_units)
    tile_softmax(buf_a, win, vw_ref, qi, diag_units)
    tile_softmax(buf_b, win, vw_ref, prev, prev_units)
    finish(win, 2)

    j_idx = lax.broadcasted_iota(jnp.int32, (NSB, 1), 0)
    j_f = j_idx.astype(f32)
    cur = t1 // SLC
    valid = (j_idx * SLC) <= t1
    forced = (j_idx == 0) | (j_idx == cur) | (j_idx == cur - 1)
    score = jnp.where(valid, imp + jnp.where(forced, FORCE_BONUS, 0.0), -jnp.inf)
    for _ in range(n_sel):
        mx = jnp.max(score, axis=0, keepdims=True)
        first = jnp.min(jnp.where(score == mx, j_f, float(NSB)), axis=0, keepdims=True)
        score = jnp.where(j_f == first, -jnp.inf, score)
    nsel = jnp.where(score == -jnp.inf, 0.0, NEG).astype(bf16)
    for r in range(R):
        qa_ref[DK:DK + NSB, heads[r]] = nsel

    selst = (ms_ref, ls_ref, accs_ref)
    reset(*selst)
    tile_scores(buf_a, ksa_ref, qi, aug, causal, diag_units)
    tile_scores(buf_b, ksa_ref, 0, aug, None, full_units)
    tile_softmax(buf_a, selst, vs_ref, qi, diag_units)

    def sel_body(kt, carry):
        nxt = jnp.minimum(kt + 1, jnp.maximum(qi - 1, 0))
        for lanes, krows in full_units:
            softmax_pv(buf_b, selst, vs_ref, kt, lanes, krows)
            scores_to(buf_b, ksa_ref, nxt, aug, None, lanes, krows)
        return carry

    lax.fori_loop(0, qi, sel_body, 0)
    finish(selst, 1)

    for r in range(R):
        o_ref[:, r * DK:(r + 1) * DK] = out_ref[:, heads[r]].T.astype(bf16)


def _nsa(proj, small, cosf, sinf, ovl_t, kcmp, vcmp_t, ksa, vs_t, kw, vw_t, B, T, tq):
    nq = T // tq
    nc = kcmp.shape[2]
    rows = R * tq
    n_sel = min(N_SELECT, T // SLC)
    assert tq >= WINDOW and nq >= 2

    def per_bg(*shape):
        return pl.BlockSpec((None, None) + shape, lambda b, g, i: (b, g) + (0,) * len(shape))

    tab = pl.BlockSpec((tq, DK), lambda b, g, i: (b * nq + i, 0))
    return pl.pallas_call(
        functools.partial(_nsa_kernel, tq=tq, n_sel=n_sel),
        grid=(B, G, nq),
        in_specs=[
            pl.BlockSpec((tq, R * DK), lambda b, g, i: (b * nq + i, g)),
            tab, tab,
            pl.BlockSpec((tq, 128), lambda b, g, i: (b * nq + i, g)),
            pl.BlockSpec((NSB, nc), lambda b, g, i: (0, 0)),
            per_bg(nc, DK), per_bg(DK, nc),
            per_bg(T, DK + NSB), per_bg(nq, DK, tq), per_bg(T, DK), per_bg(nq, DK, tq),
        ],
        out_specs=pl.BlockSpec((tq, R * DK), lambda b, g, i: (b * nq + i, g)),
        out_shape=jax.ShapeDtypeStruct((B * T, D), bf16),
        scratch_shapes=[pltpu.VMEM((DK + NSB, rows), bf16)]
        + [pltpu.VMEM((1, rows), f32), pltpu.VMEM((1, rows), f32), pltpu.VMEM((DK, rows), f32)] * 2
        + [pltpu.VMEM((DK, rows), f32)]
        + [pltpu.VMEM((tq, rows), f32), pltpu.VMEM((1, rows), f32)] * 2,
        compiler_params=_cparams(("parallel", "parallel", "arbitrary")),
        name="nsa_attention",
    )(proj, cosf, sinf, small, ovl_t, kcmp, vcmp_t, ksa, vs_t, kw, vw_t)


def _ssd_kernel(z_ref, xs_ref, bm_ref, cm_ref, dt_ref,
                wx_ref, bx_ref, wb_ref, bb_ref, wc_ref, bc_ref,
                dtb_ref, alog_ref, dskip_ref, ng_ref, eh_ref, shift_ref,
                o_ref, st_ref, prev_ref):
    c = pl.program_id(1)
    L = SSD_L
    HP = 2 * SSD_P
    GW = D // SSD_G

    @pl.when(c == 0)
    def _():
        st_ref[...] = jnp.zeros_like(st_ref)
        prev_ref[...] = jnp.zeros_like(prev_ref)

    def conv_silu(u_ref, ucols, col, w_ref, b_ref):
        width = ucols.stop - ucols.start
        cur = u_ref[:, ucols]
        ext = jnp.concatenate([prev_ref[:, col:col + width], cur], axis=0)
        sh = _dot(shift_ref[...], ext)
        y = b_ref[:, ucols] + w_ref[SSD_CONV - 1:SSD_CONV, ucols] * cur.astype(f32)
        for j in range(SSD_CONV - 1):
            y = y + w_ref[j:j + 1, ucols] * sh[j * L:(j + 1) * L]
        prev_ref[:, col:col + width] = cur
        return y * _sigmoid(y)

    lane = lax.broadcasted_iota(jnp.int32, (1, 128), 1)
    xdt_in = dt_ref[...] + dtb_ref[...]
    dt = jnp.maximum(xdt_in, 0.0) + jnp.log1p(jnp.exp(-jnp.abs(xdt_in)))
    dt = jnp.where(lane < SSD_HEADS, dt, 0.0)
    da = dt * (-jnp.exp(alog_ref[...]))

    row = lax.broadcasted_iota(jnp.int32, (L, L), 0)
    colm = lax.broadcasted_iota(jnp.int32, (L, L), 1)
    causal = colm <= row
    tril = jnp.where(causal, 1.0, 0.0).astype(bf16)
    d_hi, d_mid, d_lo = _split3(da)
    acs = _dot(tril, d_hi) + _dot(tril, d_mid) + _dot(tril, d_lo)
    acs_t = acs.T
    last = acs[L - 1:L, :]
    dt_b = dt.astype(bf16)
    eacs_b = jnp.exp(acs).astype(bf16)
    dec_b = jnp.exp(last - acs).astype(bf16)
    c_hi, c_lo = _split2(jnp.exp(jnp.broadcast_to(last, (8, 128))))
    lane_hp = lax.broadcasted_iota(jnp.int32, (L, HP), 1)
    gn = SSD_G * SSD_N

    for g in range(SSD_G):
        cols = slice(g * GW, (g + 1) * GW)
        ncols = slice(g * SSD_N, (g + 1) * SSD_N)
        xs = conv_silu(xs_ref, cols, g * GW, wx_ref, bx_ref)
        bm_g = conv_silu(bm_ref, ncols, D + g * SSD_N, wb_ref, bb_ref)
        cm_g = conv_silu(cm_ref, ncols, D + gn + g * SSD_N, wc_ref, bc_ref).astype(bf16)
        eh = eh_ref[:, cols]
        xdt = xs * _dot(dt_b, eh)
        xdt_b = xdt.astype(bf16)
        xdec_b = (xdt * _dot(dec_b, eh)).astype(bf16)
        eacs_x = _dot(eacs_b, eh)
        cdec_x = (_dot(c_hi, eh) + _dot(c_lo, eh))[0:1, :]

        cb = _dot_nt(cm_g, bm_g.astype(bf16))
        s_in = st_ref[g]
        y_off = _dot(cm_g, s_in.astype(bf16))
        st_ref[g] = s_in * cdec_x + _dot(bm_g.T.astype(bf16), xdec_b)
        ys = []
        for pp in range(GW // HP):
            h0 = g * (GW // SSD_P) + 2 * pp
            pc = slice(pp * HP, (pp + 1) * HP)
            ms = []
            for h in (h0, h0 + 1):
                diff = acs[:, h:h + 1] - acs_t[h:h + 1, :]
                ms.append(jnp.where(causal, cb * jnp.exp(jnp.where(causal, diff, NEG)), 0.0))
            lhs = jnp.concatenate(ms, axis=1).astype(bf16)
            xp = xdt_b[:, pc]
            zero = jnp.zeros_like(xp)
            rhs = jnp.concatenate([jnp.where(lane_hp < SSD_P, xp, zero),
                                   jnp.where(lane_hp >= SSD_P, xp, zero)], axis=0)
            ys.append(_dot(lhs, rhs) + eacs_x[:, pc] * y_off[:, pc]
                      + dskip_ref[:, g * GW + pp * HP:g * GW + (pp + 1) * HP] * xs[:, pc])
        z = z_ref[:, cols].astype(f32)
        yg = jnp.concatenate(ys, axis=1) * (z * _sigmoid(z))
        ms = jnp.mean(yg * yg, axis=-1, keepdims=True)
        o_ref[:, cols] = (yg * lax.rsqrt(ms + EPS) * ng_ref[:, cols]).astype(bf16)


def _ssd(proj, small, wts, B, T):
    nch = T // SSD_L
    L = SSD_L

    def main(name, w):
        base = _blk(name, w)
        return pl.BlockSpec((L, w), lambda b, c: (b * nch + c, base))

    def full(a):
        return pl.BlockSpec(a.shape, lambda b, c: (0,) * a.ndim)

    gn = SSD_G * SSD_N
    return pl.pallas_call(
        _ssd_kernel,
        grid=(B, nch),
        in_specs=[main('s_z', D), main('xs', D), main('bm', gn), main('cm', gn),
                  pl.BlockSpec((L, 128), lambda b, c: (b * nch + c, G))] + [full(a) for a in wts],
        out_specs=pl.BlockSpec((L, D), lambda b, c: (b * nch + c, 0)),
        out_shape=jax.ShapeDtypeStruct((B * T, D), bf16),
        scratch_shapes=[pltpu.VMEM((SSD_G, SSD_N, D // SSD_G), f32),
                        pltpu.VMEM((L, D + 2 * gn), bf16)],
        compiler_params=_cparams(("parallel", "arbitrary")),
        name="ssd",
    )(proj, proj, proj, proj, small, *wts)


def _merge_kernel(ya_ref, cb_ref, cc_ref, cu_ref, cch_ref, cuh_ref, yc_ref,
                  g0_ref, g1_ref, g2_ref, x_ref, cw_ref, wo_ref, o_ref, ext_ref, *, tm, tiles_per_seq):
    first = (pl.program_id(0) % tiles_per_seq) == 0
    up = lambda ref: ref[...].astype(f32)
    ext_ref[0:HALO, :] = jnp.where(first, 0.0, up(cch_ref) * up(cuh_ref))
    ext_ref[HALO:HALO + tm, :] = up(cc_ref) * up(cu_ref)
    lo = HALO - (CONV_W - 1)
    conv = cw_ref[0:1, :] * ext_ref[lo:lo + tm, :]
    for j in range(1, CONV_W):
        conv = conv + cw_ref[j:j + 1, :] * ext_ref[lo + j:lo + j + tm, :]
    yb = up(cb_ref) * conv
    merged = (_sigmoid(up(g0_ref)) * up(ya_ref) + _sigmoid(up(g1_ref)) * yb
              + _sigmoid(up(g2_ref)) * up(yc_ref))
    o_ref[...] = x_ref[...] + _dot(merged.astype(bf16), wo_ref[...])


def _merge(ya, yc, proj, x2, cw, wo, T, tm):
    M = x2.shape[0]

    def col(name):
        base = _blk(name, D)
        return pl.BlockSpec((tm, D), lambda i: (i, base))

    def halo(name):
        base = _blk(name, D)
        return pl.BlockSpec((HALO, D), lambda i: (jnp.maximum(i * (tm // HALO) - 1, 0), base))

    row = pl.BlockSpec((tm, D), lambda i: (i, 0))
    return pl.pallas_call(
        functools.partial(_merge_kernel, tm=tm, tiles_per_seq=T // tm),
        grid=(M // tm,),
        in_specs=[row, col('cb'), col('cc'), col('cu'), halo('cc'), halo('cu'), row,
                  col('gm0'), col('gm1'), col('gm2'), row,
                  pl.BlockSpec((CONV_W, D), lambda i: (0, 0)),
                  pl.BlockSpec((D, D), lambda i: (0, 0))],
        out_specs=row,
        out_shape=jax.ShapeDtypeStruct((M, D), f32),
        scratch_shapes=[pltpu.VMEM((HALO + tm, D), f32)],
        compiler_params=_cparams(("parallel",)),
        name="merge_oproj",
    )(ya, proj, proj, proj, proj, proj, yc, proj, proj, proj, x2, cw, wo)


def _ffn_kernel(x_ref, g_ref, wu_ref, wd_ref, o_ref, h_ref):
    @pl.when(pl.program_id(1) == 0)
    def _():
        x = x_ref[...]
        h_ref[...] = _rms(x, g_ref[...]).astype(bf16)
        o_ref[...] = x

    u = jnp.maximum(_dot(h_ref[...], wu_ref[...]), 0.0)
    o_ref[...] += _dot((u * u).astype(bf16), wd_ref[...])


def _ffn(x2, g, wu, wd, tm, tf):
    M = x2.shape[0]
    return pl.pallas_call(
        _ffn_kernel,
        grid=(M // tm, D_FF // tf),
        in_specs=[pl.BlockSpec((tm, D), lambda i, j: (i, 0)),
                  pl.BlockSpec((1, D), lambda i, j: (0, 0)),
                  pl.BlockSpec((D, tf), lambda i, j: (0, j)),
                  pl.BlockSpec((tf, D), lambda i, j: (j, 0))],
        out_specs=pl.BlockSpec((tm, D), lambda i, j: (i, 0)),
        out_shape=jax.ShapeDtypeStruct((M, D), f32),
        scratch_shapes=[pltpu.VMEM((tm, D), bf16)],
        compiler_params=_cparams(("parallel", "arbitrary")),
        name="ffn",
    )(x2, g, wu, wd)


def _ple_kernel(x_ref, p_ref, g_ref, wp_ref, wg_ref, gf_ref, o_ref, *, final):
    x = x_ref[...]
    gate = _sigmoid(_dot(_rms(x, g_ref[...]).astype(bf16), wg_ref[...]))
    y = x + _dot(p_ref[...].astype(bf16), wp_ref[...]) * gate
    if final:
        y = _rms(y, gf_ref[...])
    o_ref[...] = y


def _ple(x2, p2, layer, g, wp, wg, gf, tm, final):
    M = x2.shape[0]
    base = layer * (M // tm)
    row = pl.BlockSpec((tm, D), lambda i: (i, 0))
    vec = pl.BlockSpec((1, D), lambda i: (0, 0))
    return pl.pallas_call(
        functools.partial(_ple_kernel, final=final),
        grid=(M // tm,),
        in_specs=[row, pl.BlockSpec((tm, PLE), lambda i: (base + i, 0)), vec,
                  pl.BlockSpec((PLE, D), lambda i: (0, 0)),
                  pl.BlockSpec((D, D), lambda i: (0, 0)), vec],
        out_specs=row,
        out_shape=jax.ShapeDtypeStruct((M, D), f32),
        compiler_params=_cparams(("parallel",)),
        name="ple",
    )(x2, p2, g, wp, wg, gf)


def _source_runs():
    runs = []
    for n in _ORDER:
        a, wd = _SRC[n]
        if runs and runs[-1][1] == a:
            runs[-1][1] = a + wd
        else:
            runs.append([a, a + wd])
    return runs


def _prep_w_in(w):
    depth = w.shape[0]
    main = jnp.concatenate([w[:, :, a:b] for a, b in _source_runs()], axis=2)
    a, wd = _SRC['g_nsa']
    gn = w[:, :, a:a + wd].reshape(depth, D, G, R * 3)
    gn = jnp.pad(gn, ((0, 0), (0, 0), (0, 0), (0, 128 - R * 3))).reshape(depth, D, G * 128)
    a, wd = _SRC['s_dt']
    dt = jnp.pad(w[:, :, a:a + wd], ((0, 0), (0, 0), (0, 128 - wd)))
    return main.astype(bf16), jnp.concatenate([gn, dt], axis=2).astype(bf16)


def _overlap_matrix(nc):
    i = np.arange(nc)[:, None]
    j = np.arange(NSB)[None, :]
    ovl = (i * CMP_STRIDE < j * SLC + SLC) & (i * CMP_STRIDE + CMP_BLOCK > j * SLC)
    return jnp.asarray(ovl.T.astype(np.float32), dtype=bf16)


def _head_expand():
    h = np.arange(128)[:, None]
    ch = np.arange(D)[None, :]
    return jnp.asarray((ch // SSD_P == h).astype(np.float32), dtype=bf16)


def _conv_shift():
    i = np.arange((SSD_CONV - 1) * SSD_L)[:, None]
    r = np.arange(2 * SSD_L)[None, :]
    pick = r == SSD_L + i % SSD_L - (SSD_CONV - 1) + i // SSD_L
    return jnp.asarray(pick.astype(np.float32), dtype=bf16)


def kernel(x, p, positions, g_mix, w_in, nsa_pe_k, nsa_pe_v, phi_k_w1, phi_k_b1, phi_k_w2, phi_k_b2,
           phi_v_w1, phi_v_b1, phi_v_w2, phi_v_b2, sconv_w, ssd_conv_w, ssd_conv_b, ssd_dt_bias,
           ssd_a_log, ssd_d, ssd_norm_g, w_o, g_mlp, w_up, w_down, g_ple, w_ple, w_ple_gate, g_final):
    B, T, _ = x.shape
    depth = w_in.shape[0]
    M = B * T
    assert T % 256 == 0 and T // SLC <= NSB
    tq = 512
    tm_proj = 1024 if M % 1024 == 0 else 256
    tm = 512 if T % 512 == 0 else 256
    nc = T // CMP_STRIDE

    inv_freq = 1.0 / (10000.0 ** (jnp.arange(0, DK, 2, dtype=f32) / DK))
    ang = positions.astype(f32)[..., None] * inv_freq
    cosf = jnp.concatenate([jnp.cos(ang), jnp.cos(ang)], axis=-1).reshape(M, DK)
    sinf = jnp.concatenate([-jnp.sin(ang), jnp.sin(ang)], axis=-1).reshape(M, DK)
    ovl = _overlap_matrix(nc)
    eh = _head_expand()
    shift = _conv_shift()

    def vec(a, n=None):
        a = a.reshape(1, -1).astype(f32)
        return a if n is None else jnp.pad(a, ((0, 0), (0, n - a.shape[1])))

    x2 = x.reshape(M, D)
    p2 = p.reshape(depth * M, PLE)
    w_main, w_small = _prep_w_in(w_in)
    for i in range(depth):
        proj, small = _proj(x2, vec(g_mix[i]), w_main[i], w_small[i], tm_proj, 2048)

        kc, vc, ksa, vs, kw, vw = _prep(proj, cosf, sinf, B, T, tq)

        def phi(pe, w1, b1, w2, b2):
            pe8 = jnp.broadcast_to(pe.reshape(1, -1), (8, CMP_BLOCK * DK)).astype(bf16)
            return (pe8, w1.astype(bf16), vec(b1), w2.astype(bf16), vec(b2))

        kcmp, vcmp = _compress(kc, vc,
                               phi(nsa_pe_k[i], phi_k_w1[i], phi_k_b1[i], phi_k_w2[i], phi_k_b2[i]),
                               phi(nsa_pe_v[i], phi_v_w1[i], phi_v_b1[i], phi_v_w2[i], phi_v_b2[i]))
        ya = _nsa(proj, small, cosf, sinf, ovl, kcmp, vcmp, ksa, vs, kw, vw, B, T, tq)

        cw, cbias = ssd_conv_w[i], ssd_conv_b[i]
        gn = SSD_G * SSD_N
        ssd_w = (cw[:, :D], vec(cbias[:D]), cw[:, D:D + gn], vec(cbias[D:D + gn]),
                 cw[:, D + gn:], vec(cbias[D + gn:]),
                 vec(ssd_dt_bias[i], 128), vec(ssd_a_log[i], 128),
                 vec(jnp.repeat(ssd_d[i], SSD_P)), vec(ssd_norm_g[i]), eh, shift)
        yc = _ssd(proj, small, ssd_w, B, T)

        x2 = _merge(ya, yc, proj, x2, sconv_w[i], w_o[i].astype(bf16), T, 256)
        x2 = _ffn(x2, vec(g_mlp[i]), w_up[i].astype(bf16), w_down[i].astype(bf16), tm_proj, 512)
        x2 = _ple(x2, p2, i, vec(g_ple[i]), w_ple[i].astype(bf16),
                  w_ple_gate[i].astype(bf16), vec(g_final), tm, final=(i == depth - 1))
    return x2.reshape(B, T, D)
```

```python
import functools

import numpy as np
import jax
import jax.numpy as jnp
from jax import lax
from jax.experimental import pallas as pl
from jax.experimental.pallas import tpu as pltpu

f32 = jnp.float32
bf16 = jnp.bfloat16

D = 2048
N_HEADS = 16
DK = 128
G = 4
R = N_HEADS // G
CMP_BLOCK = 32
CMP_STRIDE = 16
SLC = 64
N_SELECT = 16
WINDOW = 512
FORCE_BONUS = 1.0e4
CONV_W = 3
SSD_HEADS = 32
SSD_P = 64
SSD_G = 4
SSD_N = 128
SSD_CONV = 4
SSD_L = 128
D_FF = 4 * D
PLE = 256
EPS = 1e-6
NSB = 128
HALO = 16
NEG = -1e30
LOG2E = 1.4426950408889634

VMEM_LIMIT = 56 * 1024 * 1024

_SRC = {
    'q': (0, 2048), 'k_c': (2048, 512), 'v_c': (2560, 512), 'k_s': (3072, 512), 'v_s': (3584, 512),
    'k_w': (4096, 512), 'v_w': (4608, 512), 'g_nsa': (5120, 48),
    'cb': (5168, 2048), 'cc': (7216, 2048), 'cu': (9264, 2048),
    's_z': (11312, 2048), 'xs': (13360, 2048), 'bm': (15408, 512), 'cm': (15920, 512),
    's_dt': (16432, 32), 'gm0': (16464, 2048), 'gm1': (18512, 2048), 'gm2': (20560, 2048),
}
_ORDER = ['q', 'cb', 'cc', 'cu', 's_z', 'xs', 'gm0', 'gm1', 'gm2',
          'k_c', 'v_c', 'k_s', 'v_s', 'k_w', 'v_w', 'bm', 'cm']
_OFF = {}
_o = 0
for _n in _ORDER:
    _OFF[_n] = _o
    _o += _SRC[_n][1]
N_MAIN = _o
N_SMALL = G * 128 + 128


def _blk(name, width):
    off = _OFF[name]
    assert off % width == 0
    return off // width


def _cparams(sem):
    return pltpu.CompilerParams(dimension_semantics=sem, vmem_limit_bytes=VMEM_LIMIT)


def _dot(a, b):
    return jnp.dot(a, b, preferred_element_type=f32)


def _dot_nt(a, b):
    return lax.dot_general(a, b, (((1,), (1,)), ((), ())), preferred_element_type=f32)


def _sigmoid(x):
    return jax.nn.sigmoid(x)


def _split2(x):
    hi = x.astype(bf16)
    lo = (x - hi.astype(f32)).astype(bf16)
    return hi, lo


def _split3(x):
    hi = x.astype(bf16)
    r1 = x - hi.astype(f32)
    mid = r1.astype(bf16)
    lo = (r1 - mid.astype(f32)).astype(bf16)
    return hi, mid, lo


def _rms(x, g):
    ms = jnp.mean(x * x, axis=-1, keepdims=True)
    return x * lax.rsqrt(ms + EPS) * g


def _proj_kernel(x_ref, g_ref, w_ref, ws_ref, o_ref, os_ref, h_ref):
    @pl.when(pl.program_id(1) == 0)
    def _():
        h = _rms(x_ref[...], g_ref[...]).astype(bf16)
        h_ref[...] = h
        os_ref[...] = _dot(h, ws_ref[...])

    o_ref[...] = _dot(h_ref[...], w_ref[...]).astype(bf16)


def _proj(x2, g, w_main, w_small, layer, tm, tn):
    M = x2.shape[0]
    return pl.pallas_call(
        _proj_kernel,
        grid=(M // tm, N_MAIN // tn),
        in_specs=[
            pl.BlockSpec((tm, D), lambda i, j: (i, 0), pipeline_mode=pl.Buffered(1)),
            pl.BlockSpec((1, D), lambda i, j: (0, 0)),
            pl.BlockSpec((None, D, tn), lambda i, j: (layer, 0, j)),
            pl.BlockSpec((None, D, N_SMALL), lambda i, j: (layer, 0, 0), pipeline_mode=pl.Buffered(1)),
        ],
        out_specs=[
            pl.BlockSpec((tm, tn), lambda i, j: (i, j)),
            pl.BlockSpec((tm, N_SMALL), lambda i, j: (i, 0)),
        ],
        out_shape=[jax.ShapeDtypeStruct((M, N_MAIN), bf16),
                   jax.ShapeDtypeStruct((M, N_SMALL), f32)],
        scratch_shapes=[pltpu.VMEM((tm, D), bf16)],
        compiler_params=_cparams(("parallel", "arbitrary")),
        name="proj",
    )(x2, g, w_main, w_small)


def _rope(x, cosf, sinf):
    x = x.astype(f32)
    return x * cosf + pltpu.roll(x, DK // 2, axis=1) * sinf


def _prep_kernel(kc_ref, vc_ref, ks_ref, vs_ref, kw_ref, vw_ref, cos_ref, sin_ref,
                 kco_ref, vco_ref, ksa_ref, vso_ref, kwo_ref, vwo_ref, stage_ref, *, tt):
    cosf = cos_ref[...]
    sinf = sin_ref[...]
    t = pl.program_id(1) * tt + lax.broadcasted_iota(jnp.int32, (tt, NSB), 0)
    j = lax.broadcasted_iota(jnp.int32, (tt, NSB), 1)
    onehot = jnp.where(t // SLC == j, 1.0, 0.0).astype(bf16)
    rows = tt // CMP_STRIDE

    def to_block_rows(x, o_ref, g):
        stage_ref[...] = x
        for l in range(CMP_STRIDE):
            o_ref[g, :, l * DK:(l + 1) * DK] = stage_ref[pl.ds(l, rows, stride=CMP_STRIDE), :].astype(bf16)

    for g in range(G):
        cols = slice(g * DK, (g + 1) * DK)
        to_block_rows(_rope(kc_ref[:, cols], cosf, sinf), kco_ref, g)
        to_block_rows(vc_ref[:, cols].astype(f32), vco_ref, g)
        ksa_ref[g, :, 0:DK] = _rope(ks_ref[:, cols], cosf, sinf).astype(bf16)
        ksa_ref[g, :, DK:DK + NSB] = onehot
        vso_ref[g] = vs_ref[:, cols].astype(f32).T.astype(bf16)
        kwo_ref[g] = _rope(kw_ref[:, cols], cosf, sinf).astype(bf16)
        vwo_ref[g] = vw_ref[:, cols].astype(f32).T.astype(bf16)


def _prep(proj, cosf, sinf, B, T, tt):
    nt = T // tt
    gw = G * DK
    rows = tt // CMP_STRIDE

    def col(name):
        base = _blk(name, gw)
        return pl.BlockSpec((tt, gw), lambda b, i: (b * nt + i, base))

    tab = pl.BlockSpec((tt, DK), lambda b, i: (b * nt + i, 0))

    def out(n, w):
        return pl.BlockSpec((None, G, n, w), lambda b, i: (b, 0, i, 0))

    out_t = pl.BlockSpec((None, G, None, DK, tt), lambda b, i: (b, 0, i, 0, 0))
    return pl.pallas_call(
        functools.partial(_prep_kernel, tt=tt),
        grid=(B, nt),
        in_specs=[col('k_c'), col('v_c'), col('k_s'), col('v_s'), col('k_w'), col('v_w'), tab, tab],
        out_specs=[out(rows, CMP_STRIDE * DK), out(rows, CMP_STRIDE * DK), out(tt, DK + NSB), out_t,
                   out(tt, DK), out_t],
        out_shape=[jax.ShapeDtypeStruct((B, G, T // CMP_STRIDE, CMP_STRIDE * DK), bf16),
                   jax.ShapeDtypeStruct((B, G, T // CMP_STRIDE, CMP_STRIDE * DK), bf16),
                   jax.ShapeDtypeStruct((B, G, T, DK + NSB), bf16),
                   jax.ShapeDtypeStruct((B, G, nt, DK, tt), bf16),
                   jax.ShapeDtypeStruct((B, G, T, DK), bf16),
                   jax.ShapeDtypeStruct((B, G, nt, DK, tt), bf16)],
        scratch_shapes=[pltpu.VMEM((tt, DK), f32)],
        compiler_params=_cparams(("parallel", "parallel")),
        name="kv_prep",
    )(proj, proj, proj, proj, proj, proj, cosf, sinf)


def _compress_one(x_ref, pe_ref, w1_ref, b1_ref, w2_ref, b2_ref, o_ref, transpose):
    nc = x_ref.shape[0]
    half = CMP_STRIDE * DK
    x = x_ref[...]
    first = _dot(x, w1_ref[0:half, :])
    second = _dot(x, w1_ref[half:2 * half, :])
    second = pltpu.roll(second, nc - 1, axis=0)
    pe = _dot(pe_ref[...], w1_ref[...])[0:1, :]
    h = first + second + pe + b1_ref[...]
    h = h * _sigmoid(h)
    y = _dot(h.astype(bf16), w2_ref[...]) + b2_ref[...]
    o_ref[...] = (y.T if transpose else y).astype(bf16)


def _compress_kernel(k_ref, v_ref, pek, w1k, b1k, w2k, b2k, pev, w1v, b1v, w2v, b2v, ko_ref, vo_ref):
    _compress_one(k_ref, pek, w1k, b1k, w2k, b2k, ko_ref, False)
    _compress_one(v_ref, pev, w1v, b1v, w2v, b2v, vo_ref, True)


def _compress(kc2, vc2, wk, wv):
    B, _, nc, width = kc2.shape
    xin = pl.BlockSpec((None, None, nc, width), lambda b, g: (b, g, 0, 0))

    def full(a):
        return pl.BlockSpec(a.shape, lambda b, g: (0,) * a.ndim)

    out = pl.BlockSpec((None, None, nc, DK), lambda b, g: (b, g, 0, 0))
    out_t = pl.BlockSpec((None, None, DK, nc), lambda b, g: (b, g, 0, 0))
    return pl.pallas_call(
        _compress_kernel,
        grid=(B, G),
        in_specs=[xin, xin] + [full(a) for a in wk] + [full(a) for a in wv],
        out_specs=[out, out_t],
        out_shape=[jax.ShapeDtypeStruct((B, G, nc, DK), bf16),
                   jax.ShapeDtypeStruct((B, G, DK, nc), bf16)],
        compiler_params=_cparams(("parallel", "parallel")),
        name="compress",
    )(kc2, vc2, *wk, *wv)


def _nsa_kernel(q_ref, cos_ref, sin_ref, gate_ref, ovl_ref, kcmp_ref, vcmp_ref,
                ksa_ref, vs_ref, kw_ref, vw_ref, o_ref,
                qa_ref, ms_ref, ls_ref, accs_ref, mw_ref, lw_ref, accw_ref, out_ref,
                sc_ref, mx_ref, sd_ref, md_ref,
                *, tq, n_sel):
    qi = pl.program_id(2)
    q0 = qi * tq
    heads = [slice(r * tq, (r + 1) * tq) for r in range(R)]
    qscale = DK ** -0.5 * LOG2E
    cosf = cos_ref[...]
    sinf = sin_ref[...]
    for r in range(R):
        qr = _rope(q_ref[:, r * DK:(r + 1) * DK], cosf, sinf) * qscale
        qa_ref[0:DK, heads[r]] = qr.T.astype(bf16)
    t1 = q0 + lax.broadcasted_iota(jnp.int32, (1, tq), 1)
    gate_t = _sigmoid(gate_ref[...]).T

    def gate_row(c, r):
        return gate_t[3 * r + c:3 * r + c + 1, :]

    nc = kcmp_ref.shape[0]
    n_idx = lax.broadcasted_iota(jnp.int32, (nc, 1), 0)
    vis = (n_idx * CMP_STRIDE + (CMP_BLOCK - 1)) <= t1
    has_any = t1 >= CMP_BLOCK - 1
    psum = jnp.zeros((nc, tq), f32)
    for r in range(R):
        s = jnp.where(vis, _dot(kcmp_ref[...], qa_ref[0:DK, heads[r]]), NEG)
        e = jnp.exp2(s - jnp.max(s, axis=0, keepdims=True))
        inv = jnp.where(has_any, 1.0 / jnp.sum(e, axis=0, keepdims=True), 0.0)
        out_ref[:, heads[r]] = (gate_row(0, r) * inv) * _dot(vcmp_ref[...], e.astype(bf16))
        psum = psum + e * inv
    p_hi, p_lo = _split2(psum)
    imp = _dot(ovl_ref[...], p_hi) + _dot(ovl_ref[...], p_lo)

    def reset(m_ref, l_ref, acc_ref):
        m_ref[...] = jnp.full(m_ref.shape, NEG, f32)
        l_ref[...] = jnp.zeros(l_ref.shape, f32)
        acc_ref[...] = jnp.zeros(acc_ref.shape, f32)

    def scores_to(buf, k_ref, kt, q_rows, mask_fn, lanes, krows):
        n = krows.stop - krows.start
        start = pl.multiple_of(kt * tq + krows.start, n)
        s = _dot(k_ref[pl.ds(start, n), :], qa_ref[q_rows, lanes])
        if mask_fn is not None:
            s = jnp.where(mask_fn(krows, lanes), s, NEG)
        buf[0][krows, lanes] = s
        buf[1][:, lanes] = jnp.max(s, axis=0, keepdims=True)

    def softmax_pv(buf, state, v_ref, kt, lanes, krows):
        m_ref, l_ref, acc_ref = state
        m_prev = m_ref[:, lanes]
        m_new = jnp.maximum(m_prev, buf[1][:, lanes])
        alpha = jnp.exp2(m_prev - m_new)
        pt = jnp.exp2(buf[0][krows, lanes] - m_new)
        l_ref[:, lanes] = alpha * l_ref[:, lanes] + jnp.sum(pt, axis=0, keepdims=True)
        acc_ref[:, lanes] = alpha * acc_ref[:, lanes] + _dot(v_ref[kt, :, krows], pt.astype(bf16))
        m_ref[:, lanes] = m_new

    def finish(state, c):
        m_ref, l_ref, acc_ref = state
        for r in range(R):
            out_ref[:, heads[r]] += (gate_row(c, r) * (1.0 / l_ref[:, heads[r]])) * acc_ref[:, heads[r]]

    plain = slice(0, DK)
    aug = slice(0, DK + NSB)

    buf_a = (sd_ref, md_ref)
    buf_b = (sc_ref, mx_ref)

    half = tq // 2
    every = slice(0, tq)
    full_units = [(heads[r], every) for r in range(R)]

    def half_units(rows_lo, rows_hi):
        return [(slice(r * tq + h * half, r * tq + (h + 1) * half), rows)
                for r in range(R) for h, rows in enumerate((rows_lo, rows_hi))]

    diag_units = half_units(slice(0, half), every)
    prev_units = half_units(every, slice(half, tq))

    def tile_scores(buf, k_ref, kt, q_rows, mask_fn, units):
        for lanes, krows in units:
            scores_to(buf, k_ref, kt, q_rows, mask_fn, lanes, krows)

    def tile_softmax(buf, state, v_ref, kt, units):
        for lanes, krows in units:
            softmax_pv(buf, state, v_ref, kt, lanes, krows)

    kpos = lax.broadcasted_iota(jnp.int32, (tq, 1), 0)
    t_all = q0 + (lax.broadcasted_iota(jnp.int32, (1, R * tq), 1) & (tq - 1))

    def causal(krows, lanes):
        return q0 + kpos[krows] <= t_all[:, lanes]

    def inside(krows, lanes):
        return ((q0 - tq) + kpos[krows] > t_all[:, lanes] - WINDOW) & (qi >= 1)

    win = (mw_ref, lw_ref, accw_ref)
    reset(*win)
    prev = jnp.maximum(qi - 1, 0)
    tile_scores(buf_a, kw_ref, qi, plain, causal, diag_units)
    tile_scores(buf_b, kw_ref, prev, plain, inside, prev_units)
    tile_softmax(buf_a, win, vw_ref, qi, diag_units)
    tile_softmax(buf_b, win, vw_ref, prev, prev_units)
    finish(win, 2)

    j_idx = lax.broadcasted_iota(jnp.int32, (NSB, 1), 0)
    j_f = j_idx.astype(f32)
    cur = t1 // SLC
    valid = (j_idx * SLC) <= t1
    forced = (j_idx == 0) | (j_idx == cur) | (j_idx == cur - 1)
    score = jnp.where(valid, imp + jnp.where(forced, FORCE_BONUS, 0.0), -jnp.inf)
    for _ in range(n_sel):
        mx = jnp.max(score, axis=0, keepdims=True)
        first = jnp.min(jnp.where(score == mx, j_f, float(NSB)), axis=0, keepdims=True)
        score = jnp.where(j_f == first, -jnp.inf, score)
    nsel = jnp.where(score == -jnp.inf, 0.0, NEG).astype(bf16)
    for r in range(R):
        qa_ref[DK:DK + NSB, heads[r]] = nsel

    selst = (ms_ref, ls_ref, accs_ref)
    reset(*selst)
    tile_scores(buf_a, ksa_ref, qi, aug, causal, diag_units)
    tile_scores(buf_b, ksa_ref, 0, aug, None, full_units)
    tile_softmax(buf_a, selst, vs_ref, qi, diag_units)

    def sel_body(kt, carry):
        nxt = jnp.minimum(kt + 1, jnp.maximum(qi - 1, 0))
        for lanes, krows in full_units:
            softmax_pv(buf_b, selst, vs_ref, kt, lanes, krows)
            scores_to(buf_b, ksa_ref, nxt, aug, None, lanes, krows)
        return carry

    lax.fori_loop(0, qi, sel_body, 0)
    finish(selst, 1)

    for r in range(R):
        o_ref[:, r * DK:(r + 1) * DK] = out_ref[:, heads[r]].T.astype(bf16)


def _nsa(proj, small, cosf, sinf, ovl_t, kcmp, vcmp_t, ksa, vs_t, kw, vw_t, B, T, tq):
    nq = T // tq
    nc = kcmp.shape[2]
    rows = R * tq
    n_sel = min(N_SELECT, T // SLC)
    assert tq >= WINDOW and nq >= 2

    def per_bg(*shape):
        return pl.BlockSpec((None, None) + shape, lambda b, g, i: (b, g) + (0,) * len(shape))

    tab = pl.BlockSpec((tq, DK), lambda b, g, i: (b * nq + i, 0))
    return pl.pallas_call(
        functools.partial(_nsa_kernel, tq=tq, n_sel=n_sel),
        grid=(B, G, nq),
        in_specs=[
            pl.BlockSpec((tq, R * DK), lambda b, g, i: (b * nq + i, g)),
            tab, tab,
            pl.BlockSpec((tq, 128), lambda b, g, i: (b * nq + i, g)),
            pl.BlockSpec((NSB, nc), lambda b, g, i: (0, 0)),
            per_bg(nc, DK), per_bg(DK, nc),
            per_bg(T, DK + NSB), per_bg(nq, DK, tq), per_bg(T, DK), per_bg(nq, DK, tq),
        ],
        out_specs=pl.BlockSpec((tq, R * DK), lambda b, g, i: (b * nq + i, g)),
        out_shape=jax.ShapeDtypeStruct((B * T, D), bf16),
        scratch_shapes=[pltpu.VMEM((DK + NSB, rows), bf16)]
        + [pltpu.VMEM((1, rows), f32), pltpu.VMEM((1, rows), f32), pltpu.VMEM((DK, rows), f32)] * 2
        + [pltpu.VMEM((DK, rows), f32)]
        + [pltpu.VMEM((tq, rows), f32), pltpu.VMEM((1, rows), f32)] * 2,
        compiler_params=_cparams(("parallel", "parallel", "arbitrary")),
        name="nsa_attention",
    )(proj, cosf, sinf, small, ovl_t, kcmp, vcmp_t, ksa, vs_t, kw, vw_t)


def _ssd_kernel(z_ref, xs_ref, bm_ref, cm_ref, dt_ref,
                wx_ref, bx_ref, wb_ref, bb_ref, wc_ref, bc_ref,
                dtb_ref, alog_ref, dskip_ref, ng_ref, eh_ref, shift_ref,
                o_ref, st_ref, prev_ref):
    c = pl.program_id(1)
    L = SSD_L
    HP = 2 * SSD_P
    GW = D // SSD_G

    @pl.when(c == 0)
    def _():
        st_ref[...] = jnp.zeros_like(st_ref)
        prev_ref[...] = jnp.zeros_like(prev_ref)

    def conv_silu(u_ref, ucols, col, w_ref, b_ref):
        width = ucols.stop - ucols.start
        cur = u_ref[:, ucols]
        ext = jnp.concatenate([prev_ref[:, col:col + width], cur], axis=0)
        sh = _dot(shift_ref[...], ext)
        y = b_ref[:, ucols] + w_ref[SSD_CONV - 1:SSD_CONV, ucols] * cur.astype(f32)
        for j in range(SSD_CONV - 1):
            y = y + w_ref[j:j + 1, ucols] * sh[j * L:(j + 1) * L]
        prev_ref[:, col:col + width] = cur
        return y * _sigmoid(y)

    lane = lax.broadcasted_iota(jnp.int32, (1, 128), 1)
    xdt_in = dt_ref[...] + dtb_ref[...]
    dt = jnp.maximum(xdt_in, 0.0) + jnp.log1p(jnp.exp(-jnp.abs(xdt_in)))
    dt = jnp.where(lane < SSD_HEADS, dt, 0.0)
    da = dt * (-jnp.exp(alog_ref[...]))

    row = lax.broadcasted_iota(jnp.int32, (L, L), 0)
    colm = lax.broadcasted_iota(jnp.int32, (L, L), 1)
    causal = colm <= row
    tril = jnp.where(causal, 1.0, 0.0).astype(bf16)
    d_hi, d_mid, d_lo = _split3(da)
    acs = _dot(tril, d_hi) + _dot(tril, d_mid) + _dot(tril, d_lo)
    acs_t = acs.T
    last = acs[L - 1:L, :]
    dt_b = dt.astype(bf16)
    eacs_b = jnp.exp(acs).astype(bf16)
    dec_b = jnp.exp(last - acs).astype(bf16)
    c_hi, c_lo = _split2(jnp.exp(jnp.broadcast_to(last, (8, 128))))
    lane_hp = lax.broadcasted_iota(jnp.int32, (L, HP), 1)
    gn = SSD_G * SSD_N

    for g in range(SSD_G):
        cols = slice(g * GW, (g + 1) * GW)
        ncols = slice(g * SSD_N, (g + 1) * SSD_N)
        xs = conv_silu(xs_ref, cols, g * GW, wx_ref, bx_ref)
        bm_g = conv_silu(bm_ref, ncols, D + g * SSD_N, wb_ref, bb_ref)
        cm_g = conv_silu(cm_ref, ncols, D + gn + g * SSD_N, wc_ref, bc_ref).astype(bf16)
        eh = eh_ref[:, cols]
        xdt = xs * _dot(dt_b, eh)
        xdt_b = xdt.astype(bf16)
        xdec_b = (xdt * _dot(dec_b, eh)).astype(bf16)
        eacs_x = _dot(eacs_b, eh)
        cdec_x = (_dot(c_hi, eh) + _dot(c_lo, eh))[0:1, :]

        cb = _dot_nt(cm_g, bm_g.astype(bf16))
        s_in = st_ref[g]
        y_off = _dot(cm_g, s_in.astype(bf16))
        st_ref[g] = s_in * cdec_x + _dot(bm_g.T.astype(bf16), xdec_b)
        ys = []
        for pp in range(GW // HP):
            h0 = g * (GW // SSD_P) + 2 * pp
            pc = slice(pp * HP, (pp + 1) * HP)
            ms = []
            for h in (h0, h0 + 1):
                diff = acs[:, h:h + 1] - acs_t[h:h + 1, :]
                ms.append(jnp.where(causal, cb * jnp.exp(jnp.where(causal, diff, NEG)), 0.0))
            lhs = jnp.concatenate(ms, axis=1).astype(bf16)
            xp = xdt_b[:, pc]
            zero = jnp.zeros_like(xp)
            rhs = jnp.concatenate([jnp.where(lane_hp < SSD_P, xp, zero),
                                   jnp.where(lane_hp >= SSD_P, xp, zero)], axis=0)
            ys.append(_dot(lhs, rhs) + eacs_x[:, pc] * y_off[:, pc]
                      + dskip_ref[:, g * GW + pp * HP:g * GW + (pp + 1) * HP] * xs[:, pc])
        z = z_ref[:, cols].astype(f32)
        yg = jnp.concatenate(ys, axis=1) * (z * _sigmoid(z))
        ms = jnp.mean(yg * yg, axis=-1, keepdims=True)
        o_ref[:, cols] = (yg * lax.rsqrt(ms + EPS) * ng_ref[:, cols]).astype(bf16)


def _ssd(proj, small, wts, B, T):
    nch = T // SSD_L
    L = SSD_L

    def main(name, w):
        base = _blk(name, w)
        return pl.BlockSpec((L, w), lambda b, c: (b * nch + c, base))

    def full(a):
        return pl.BlockSpec(a.shape, lambda b, c: (0,) * a.ndim)

    gn = SSD_G * SSD_N
    return pl.pallas_call(
        _ssd_kernel,
        grid=(B, nch),
        in_specs=[main('s_z', D), main('xs', D), main('bm', gn), main('cm', gn),
                  pl.BlockSpec((L, 128), lambda b, c: (b * nch + c, G))] + [full(a) for a in wts],
        out_specs=pl.BlockSpec((L, D), lambda b, c: (b * nch + c, 0)),
        out_shape=jax.ShapeDtypeStruct((B * T, D), bf16),
        scratch_shapes=[pltpu.VMEM((SSD_G, SSD_N, D // SSD_G), f32),
                        pltpu.VMEM((L, D + 2 * gn), bf16)],
        compiler_params=_cparams(("parallel", "arbitrary")),
        name="ssd",
    )(proj, proj, proj, proj, small, *wts)


def _merge_kernel(ya_ref, cb_ref, cc_ref, cu_ref, cch_ref, cuh_ref, yc_ref,
                  g0_ref, g1_ref, g2_ref, x_ref, cw_ref, wo_ref, o_ref, ext_ref, *, tm, tiles_per_seq):
    first = (pl.program_id(0) % tiles_per_seq) == 0
    up = lambda ref: ref[...].astype(f32)
    ext_ref[0:HALO, :] = jnp.where(first, 0.0, up(cch_ref) * up(cuh_ref))
    ext_ref[HALO:HALO + tm, :] = up(cc_ref) * up(cu_ref)
    lo = HALO - (CONV_W - 1)
    conv = cw_ref[0:1, :] * ext_ref[lo:lo + tm, :]
    for j in range(1, CONV_W):
        conv = conv + cw_ref[j:j + 1, :] * ext_ref[lo + j:lo + j + tm, :]
    yb = up(cb_ref) * conv
    merged = (_sigmoid(up(g0_ref)) * up(ya_ref) + _sigmoid(up(g1_ref)) * yb
              + _sigmoid(up(g2_ref)) * up(yc_ref))
    o_ref[...] = x_ref[...] + _dot(merged.astype(bf16), wo_ref[...])


def _merge(ya, yc, proj, x2, cw, wo, layer, T, tm):
    M = x2.shape[0]

    def col(name):
        base = _blk(name, D)
        return pl.BlockSpec((tm, D), lambda i: (i, base))

    def halo(name):
        base = _blk(name, D)
        return pl.BlockSpec((HALO, D), lambda i: (jnp.maximum(i * (tm // HALO) - 1, 0), base))

    row = pl.BlockSpec((tm, D), lambda i: (i, 0))
    return pl.pallas_call(
        functools.partial(_merge_kernel, tm=tm, tiles_per_seq=T // tm),
        grid=(M // tm,),
        in_specs=[row, col('cb'), col('cc'), col('cu'), halo('cc'), halo('cu'), row,
                  col('gm0'), col('gm1'), col('gm2'), row,
                  pl.BlockSpec((CONV_W, D), lambda i: (0, 0)),
                  pl.BlockSpec((None, D, D), lambda i: (layer, 0, 0))],
        out_specs=row,
        out_shape=jax.ShapeDtypeStruct((M, D), f32),
        scratch_shapes=[pltpu.VMEM((HALO + tm, D), f32)],
        compiler_params=_cparams(("parallel",)),
        name="merge_oproj",
    )(ya, proj, proj, proj, proj, proj, yc, proj, proj, proj, x2, cw, wo)


def _ffn_kernel(x_ref, g_ref, wu_ref, wd_ref, o_ref, h_ref):
    @pl.when(pl.program_id(1) == 0)
    def _():
        x = x_ref[...]
        h_ref[...] = _rms(x, g_ref[...]).astype(bf16)
        o_ref[...] = x

    u = jnp.maximum(_dot(h_ref[...], wu_ref[...]), 0.0)
    o_ref[...] += _dot((u * u).astype(bf16), wd_ref[...])


def _ffn(x2, g, wu, wd, layer, tm, tf):
    M = x2.shape[0]
    return pl.pallas_call(
        _ffn_kernel,
        grid=(M // tm, D_FF // tf),
        in_specs=[pl.BlockSpec((tm, D), lambda i, j: (i, 0)),
                  pl.BlockSpec((1, D), lambda i, j: (0, 0)),
                  pl.BlockSpec((None, D, tf), lambda i, j: (layer, 0, j)),
                  pl.BlockSpec((None, tf, D), lambda i, j: (layer, j, 0))],
        out_specs=pl.BlockSpec((tm, D), lambda i, j: (i, 0)),
        out_shape=jax.ShapeDtypeStruct((M, D), f32),
        scratch_shapes=[pltpu.VMEM((tm, D), bf16)],
        compiler_params=_cparams(("parallel", "arbitrary")),
        name="ffn",
    )(x2, g, wu, wd)


def _ple_kernel(x_ref, p_ref, g_ref, wp_ref, wg_ref, gf_ref, o_ref, *, final):
    x = x_ref[...]
    gate = _sigmoid(_dot(_rms(x, g_ref[...]).astype(bf16), wg_ref[...]))
    y = x + _dot(p_ref[...].astype(bf16), wp_ref[...]) * gate
    if final:
        y = _rms(y, gf_ref[...])
    o_ref[...] = y


def _ple(x2, p2, layer, g, wp, wg, gf, tm, final):
    M = x2.shape[0]
    base = layer * (M // tm)
    row = pl.BlockSpec((tm, D), lambda i: (i, 0))
    vec = pl.BlockSpec((1, D), lambda i: (0, 0))
    return pl.pallas_call(
        functools.partial(_ple_kernel, final=final),
        grid=(M // tm,),
        in_specs=[row, pl.BlockSpec((tm, PLE), lambda i: (base + i, 0)), vec,
                  pl.BlockSpec((None, PLE, D), lambda i: (layer, 0, 0)),
                  pl.BlockSpec((None, D, D), lambda i: (layer, 0, 0)), vec],
        out_specs=row,
        out_shape=jax.ShapeDtypeStruct((M, D), f32),
        compiler_params=_cparams(("parallel",)),
        name="ple",
    )(x2, p2, g, wp, wg, gf)


def _source_runs():
    runs = []
    for n in _ORDER:
        a, wd = _SRC[n]
        if runs and runs[-1][1] == a:
            runs[-1][1] = a + wd
        else:
            runs.append([a, a + wd])
    return runs


def _prep_w_in(w):
    depth = w.shape[0]
    w = w.astype(bf16)
    main = jnp.concatenate([w[:, :, a:b] for a, b in _source_runs()], axis=2)
    a, wd = _SRC['g_nsa']
    gn = w[:, :, a:a + wd].reshape(depth, D, G, R * 3)
    gn = jnp.pad(gn, ((0, 0), (0, 0), (0, 0), (0, 128 - R * 3))).reshape(depth, D, G * 128)
    a, wd = _SRC['s_dt']
    dt = jnp.pad(w[:, :, a:a + wd], ((0, 0), (0, 0), (0, 128 - wd)))
    return main, jnp.concatenate([gn, dt], axis=2)


def _overlap_matrix(nc):
    i = np.arange(nc)[:, None]
    j = np.arange(NSB)[None, :]
    ovl = (i * CMP_STRIDE < j * SLC + SLC) & (i * CMP_STRIDE + CMP_BLOCK > j * SLC)
    return jnp.asarray(ovl.T.astype(np.float32), dtype=bf16)


def _head_expand():
    h = np.arange(128)[:, None]
    ch = np.arange(D)[None, :]
    return jnp.asarray((ch // SSD_P == h).astype(np.float32), dtype=bf16)


def _conv_shift():
    i = np.arange((SSD_CONV - 1) * SSD_L)[:, None]
    r = np.arange(2 * SSD_L)[None, :]
    pick = r == SSD_L + i % SSD_L - (SSD_CONV - 1) + i // SSD_L
    return jnp.asarray(pick.astype(np.float32), dtype=bf16)


def kernel(x, p, positions, g_mix, w_in, nsa_pe_k, nsa_pe_v, phi_k_w1, phi_k_b1, phi_k_w2, phi_k_b2,
           phi_v_w1, phi_v_b1, phi_v_w2, phi_v_b2, sconv_w, ssd_conv_w, ssd_conv_b, ssd_dt_bias,
           ssd_a_log, ssd_d, ssd_norm_g, w_o, g_mlp, w_up, w_down, g_ple, w_ple, w_ple_gate, g_final):
    B, T, _ = x.shape
    depth = w_in.shape[0]
    M = B * T
    assert T % 256 == 0 and T // SLC <= NSB
    tq = 512
    tm_proj = 1024 if M % 1024 == 0 else 256
    tm = 512 if T % 512 == 0 else 256
    nc = T // CMP_STRIDE

    inv_freq = 1.0 / (10000.0 ** (jnp.arange(0, DK, 2, dtype=f32) / DK))
    ang = positions.astype(f32)[..., None] * inv_freq
    cosf = jnp.concatenate([jnp.cos(ang), jnp.cos(ang)], axis=-1).reshape(M, DK)
    sinf = jnp.concatenate([-jnp.sin(ang), jnp.sin(ang)], axis=-1).reshape(M, DK)
    ovl = _overlap_matrix(nc)
    eh = _head_expand()
    shift = _conv_shift()

    def vec(a, n=None):
        a = a.reshape(1, -1).astype(f32)
        return a if n is None else jnp.pad(a, ((0, 0), (0, n - a.shape[1])))

    x2 = x.reshape(M, D)
    p2 = p.reshape(depth * M, PLE)
    w_main, w_small = _prep_w_in(w_in)
    w_o16, w_up16, w_down16 = w_o.astype(bf16), w_up.astype(bf16), w_down.astype(bf16)
    w_ple16, w_gate16 = w_ple.astype(bf16), w_ple_gate.astype(bf16)
    for i in range(depth):
        proj, small = _proj(x2, vec(g_mix[i]), w_main, w_small, i, tm_proj, 2048)

        kc, vc, ksa, vs, kw, vw = _prep(proj, cosf, sinf, B, T, tq)

        def phi(pe, w1, b1, w2, b2):
            pe8 = jnp.broadcast_to(pe.reshape(1, -1), (8, CMP_BLOCK * DK)).astype(bf16)
            return (pe8, w1.astype(bf16), vec(b1), w2.astype(bf16), vec(b2))

        kcmp, vcmp = _compress(kc, vc,
                               phi(nsa_pe_k[i], phi_k_w1[i], phi_k_b1[i], phi_k_w2[i], phi_k_b2[i]),
                               phi(nsa_pe_v[i], phi_v_w1[i], phi_v_b1[i], phi_v_w2[i], phi_v_b2[i]))
        ya = _nsa(proj, small, cosf, sinf, ovl, kcmp, vcmp, ksa, vs, kw, vw, B, T, tq)

        cw, cbias = ssd_conv_w[i], ssd_conv_b[i]
        gn = SSD_G * SSD_N
        ssd_w = (cw[:, :D], vec(cbias[:D]), cw[:, D:D + gn], vec(cbias[D:D + gn]),
                 cw[:, D + gn:], vec(cbias[D + gn:]),
                 vec(ssd_dt_bias[i], 128), vec(ssd_a_log[i], 128),
                 vec(jnp.repeat(ssd_d[i], SSD_P)), vec(ssd_norm_g[i]), eh, shift)
        yc = _ssd(proj, small, ssd_w, B, T)

        x2 = _merge(ya, yc, proj, x2, sconv_w[i], w_o16, i, T, 256)
        x2 = _ffn(x2, vec(g_mlp[i]), w_up16, w_down16, i, tm_proj, 512)
        x2 = _ple(x2, p2, i, vec(g_ple[i]), w_ple16, w_gate16, vec(g_final), tm, final=(i == depth - 1))
    return x2.reshape(B, T, D)
```

```python
import functools

import numpy as np
import jax
import jax.numpy as jnp
from jax import lax
from jax.experimental import pallas as pl
from jax.experimental.pallas import tpu as pltpu

f32 = jnp.float32
bf16 = jnp.bfloat16

D = 2048
N_HEADS = 16
DK = 128
G = 4
R = N_HEADS // G
CMP_BLOCK = 32
CMP_STRIDE = 16
SLC = 64
N_SELECT = 16
WINDOW = 512
FORCE_BONUS = 1.0e4
CONV_W = 3
SSD_HEADS = 32
SSD_P = 64
SSD_G = 4
SSD_N = 128
SSD_CONV = 4
SSD_L = 128
D_FF = 4 * D
PLE = 256
EPS = 1e-6
NSB = 128
HALO = 16
NEG = -1e30
LOG2E = 1.4426950408889634

VMEM_LIMIT = 56 * 1024 * 1024

_SRC = {
    'q': (0, 2048), 'k_c': (2048, 512), 'v_c': (2560, 512), 'k_s': (3072, 512), 'v_s': (3584, 512),
    'k_w': (4096, 512), 'v_w': (4608, 512), 'g_nsa': (5120, 48),
    'cb': (5168, 2048), 'cc': (7216, 2048), 'cu': (9264, 2048),
    's_z': (11312, 2048), 'xs': (13360, 2048), 'bm': (15408, 512), 'cm': (15920, 512),
    's_dt': (16432, 32), 'gm0': (16464, 2048), 'gm1': (18512, 2048), 'gm2': (20560, 2048),
}
_ORDER = ['q', 'cb', 'cc', 'cu', 's_z', 'xs', 'gm0', 'gm1', 'gm2',
          'k_c', 'v_c', 'k_s', 'v_s', 'k_w', 'v_w', 'bm', 'cm']
_OFF = {}
_o = 0
for _n in _ORDER:
    _OFF[_n] = _o
    _o += _SRC[_n][1]
N_MAIN = _o
N_SMALL = G * 128 + 128


def _blk(name, width):
    off = _OFF[name]
    assert off % width == 0
    return off // width


def _cparams(sem):
    return pltpu.CompilerParams(dimension_semantics=sem, vmem_limit_bytes=VMEM_LIMIT)


def _dot(a, b):
    return jnp.dot(a, b, preferred_element_type=f32)


def _dot_nt(a, b):
    return lax.dot_general(a, b, (((1,), (1,)), ((), ())), preferred_element_type=f32)


def _sigmoid(x):
    return jax.nn.sigmoid(x)


def _split2(x):
    hi = x.astype(bf16)
    lo = (x - hi.astype(f32)).astype(bf16)
    return hi, lo


def _split3(x):
    hi = x.astype(bf16)
    r1 = x - hi.astype(f32)
    mid = r1.astype(bf16)
    lo = (r1 - mid.astype(f32)).astype(bf16)
    return hi, mid, lo


def _rms(x, g):
    ms = jnp.mean(x * x, axis=-1, keepdims=True)
    return x * lax.rsqrt(ms + EPS) * g


def _proj_kernel(x_ref, g_ref, w_ref, ws_ref, o_ref, os_ref, h_ref):
    @pl.when(pl.program_id(1) == 0)
    def _():
        h = _rms(x_ref[...], g_ref[...]).astype(bf16)
        h_ref[...] = h
        os_ref[...] = _dot(h, ws_ref[...])

    o_ref[...] = _dot(h_ref[...], w_ref[...]).astype(bf16)


def _proj(x2, g, w_main, w_small, layer, tm, tn):
    M = x2.shape[0]
    return pl.pallas_call(
        _proj_kernel,
        grid=(M // tm, N_MAIN // tn),
        in_specs=[
            pl.BlockSpec((tm, D), lambda i, j: (i, 0), pipeline_mode=pl.Buffered(1)),
            pl.BlockSpec((1, D), lambda i, j: (0, 0)),
            pl.BlockSpec((None, D, tn), lambda i, j: (layer, 0, j)),
            pl.BlockSpec((None, D, N_SMALL), lambda i, j: (layer, 0, 0), pipeline_mode=pl.Buffered(1)),
        ],
        out_specs=[
            pl.BlockSpec((tm, tn), lambda i, j: (i, j)),
            pl.BlockSpec((tm, N_SMALL), lambda i, j: (i, 0)),
        ],
        out_shape=[jax.ShapeDtypeStruct((M, N_MAIN), bf16),
                   jax.ShapeDtypeStruct((M, N_SMALL), f32)],
        scratch_shapes=[pltpu.VMEM((tm, D), bf16)],
        compiler_params=_cparams(("parallel", "arbitrary")),
        name="proj",
    )(x2, g, w_main, w_small)


def _rope(x, cosf, sinf):
    x = x.astype(f32)
    return x * cosf + pltpu.roll(x, DK // 2, axis=1) * sinf


def _prep_kernel(kc_ref, vc_ref, ks_ref, vs_ref, kw_ref, vw_ref, cos_ref, sin_ref,
                 kco_ref, vco_ref, ksa_ref, vso_ref, kwo_ref, vwo_ref, stage_ref, *, tt):
    cosf = cos_ref[...]
    sinf = sin_ref[...]
    t = pl.program_id(1) * tt + lax.broadcasted_iota(jnp.int32, (tt, NSB), 0)
    j = lax.broadcasted_iota(jnp.int32, (tt, NSB), 1)
    onehot = jnp.where(t // SLC == j, 1.0, 0.0).astype(bf16)
    rows = tt // CMP_STRIDE

    def to_block_rows(x, o_ref, g):
        stage_ref[...] = x
        for l in range(CMP_STRIDE):
            o_ref[g, :, l * DK:(l + 1) * DK] = stage_ref[pl.ds(l, rows, stride=CMP_STRIDE), :].astype(bf16)

    for g in range(G):
        cols = slice(g * DK, (g + 1) * DK)
        to_block_rows(_rope(kc_ref[:, cols], cosf, sinf), kco_ref, g)
        to_block_rows(vc_ref[:, cols].astype(f32), vco_ref, g)
        ksa_ref[g, :, 0:DK] = _rope(ks_ref[:, cols], cosf, sinf).astype(bf16)
        ksa_ref[g, :, DK:DK + NSB] = onehot
        vso_ref[g] = vs_ref[:, cols].astype(f32).T.astype(bf16)
        kwo_ref[g] = _rope(kw_ref[:, cols], cosf, sinf).astype(bf16)
        vwo_ref[g] = vw_ref[:, cols].astype(f32).T.astype(bf16)


def _prep(proj, cosf, sinf, B, T, tt):
    nt = T // tt
    gw = G * DK
    rows = tt // CMP_STRIDE

    def col(name):
        base = _blk(name, gw)
        return pl.BlockSpec((tt, gw), lambda b, i: (b * nt + i, base))

    tab = pl.BlockSpec((tt, DK), lambda b, i: (b * nt + i, 0))

    def out(n, w):
        return pl.BlockSpec((None, G, n, w), lambda b, i: (b, 0, i, 0))

    out_t = pl.BlockSpec((None, G, None, DK, tt), lambda b, i: (b, 0, i, 0, 0))
    return pl.pallas_call(
        functools.partial(_prep_kernel, tt=tt),
        grid=(B, nt),
        in_specs=[col('k_c'), col('v_c'), col('k_s'), col('v_s'), col('k_w'), col('v_w'), tab, tab],
        out_specs=[out(rows, CMP_STRIDE * DK), out(rows, CMP_STRIDE * DK), out(tt, DK + NSB), out_t,
                   out(tt, DK), out_t],
        out_shape=[jax.ShapeDtypeStruct((B, G, T // CMP_STRIDE, CMP_STRIDE * DK), bf16),
                   jax.ShapeDtypeStruct((B, G, T // CMP_STRIDE, CMP_STRIDE * DK), bf16),
                   jax.ShapeDtypeStruct((B, G, T, DK + NSB), bf16),
                   jax.ShapeDtypeStruct((B, G, nt, DK, tt), bf16),
                   jax.ShapeDtypeStruct((B, G, T, DK), bf16),
                   jax.ShapeDtypeStruct((B, G, nt, DK, tt), bf16)],
        scratch_shapes=[pltpu.VMEM((tt, DK), f32)],
        compiler_params=_cparams(("parallel", "parallel")),
        name="kv_prep",
    )(proj, proj, proj, proj, proj, proj, cosf, sinf)


def _compress_one(x_ref, pe_ref, w1_ref, b1_ref, w2_ref, b2_ref, o_ref, transpose):
    nc = x_ref.shape[0]
    half = CMP_STRIDE * DK
    x = x_ref[...]
    first = _dot(x, w1_ref[0:half, :])
    second = _dot(x, w1_ref[half:2 * half, :])
    second = pltpu.roll(second, nc - 1, axis=0)
    pe = _dot(pe_ref[...], w1_ref[...])[0:1, :]
    h = first + second + pe + b1_ref[...]
    h = h * _sigmoid(h)
    y = _dot(h.astype(bf16), w2_ref[...]) + b2_ref[...]
    o_ref[...] = (y.T if transpose else y).astype(bf16)


def _compress_kernel(k_ref, v_ref, pek, w1k, b1k, w2k, b2k, pev, w1v, b1v, w2v, b2v, ko_ref, vo_ref):
    _compress_one(k_ref, pek, w1k, b1k, w2k, b2k, ko_ref, False)
    _compress_one(v_ref, pev, w1v, b1v, w2v, b2v, vo_ref, True)


def _compress(kc2, vc2, wk, wv):
    B, _, nc, width = kc2.shape
    xin = pl.BlockSpec((None, None, nc, width), lambda b, g: (b, g, 0, 0))

    def full(a):
        return pl.BlockSpec(a.shape, lambda b, g: (0,) * a.ndim)

    out = pl.BlockSpec((None, None, nc, DK), lambda b, g: (b, g, 0, 0))
    out_t = pl.BlockSpec((None, None, DK, nc), lambda b, g: (b, g, 0, 0))
    return pl.pallas_call(
        _compress_kernel,
        grid=(B, G),
        in_specs=[xin, xin] + [full(a) for a in wk] + [full(a) for a in wv],
        out_specs=[out, out_t],
        out_shape=[jax.ShapeDtypeStruct((B, G, nc, DK), bf16),
                   jax.ShapeDtypeStruct((B, G, DK, nc), bf16)],
        compiler_params=_cparams(("parallel", "parallel")),
        name="compress",
    )(kc2, vc2, *wk, *wv)


def _nsa_kernel(q_ref, cos_ref, sin_ref, gate_ref, ovl_ref, kcmp_ref, vcmp_ref,
                ksa_ref, vs_ref, kw_ref, vw_ref, o_ref,
                qa_ref, ms_ref, ls_ref, accs_ref, mw_ref, lw_ref, accw_ref, out_ref,
                sc_ref, mx_ref, sd_ref, md_ref,
                *, tq, n_sel):
    qi = pl.program_id(2)
    q0 = qi * tq
    heads = [slice(r * tq, (r + 1) * tq) for r in range(R)]
    qscale = DK ** -0.5 * LOG2E
    cosf = cos_ref[...]
    sinf = sin_ref[...]
    for r in range(R):
        qr = _rope(q_ref[:, r * DK:(r + 1) * DK], cosf, sinf) * qscale
        qa_ref[0:DK, heads[r]] = qr.T.astype(bf16)
    t1 = q0 + lax.broadcasted_iota(jnp.int32, (1, tq), 1)
    gate_t = _sigmoid(gate_ref[...]).T

    def gate_row(c, r):
        return gate_t[3 * r + c:3 * r + c + 1, :]

    nc = kcmp_ref.shape[0]
    n_idx = lax.broadcasted_iota(jnp.int32, (nc, 1), 0)
    vis = (n_idx * CMP_STRIDE + (CMP_BLOCK - 1)) <= t1
    has_any = t1 >= CMP_BLOCK - 1
    psum = jnp.zeros((nc, tq), f32)
    for r in range(R):
        s = jnp.where(vis, _dot(kcmp_ref[...], qa_ref[0:DK, heads[r]]), NEG)
        e = jnp.exp2(s - jnp.max(s, axis=0, keepdims=True))
        inv = jnp.where(has_any, 1.0 / jnp.sum(e, axis=0, keepdims=True), 0.0)
        out_ref[:, heads[r]] = (gate_row(0, r) * inv) * _dot(vcmp_ref[...], e.astype(bf16))
        psum = psum + e * inv
    p_hi, p_lo = _split2(psum)
    imp = _dot(ovl_ref[...], p_hi) + _dot(ovl_ref[...], p_lo)

    def reset(m_ref, l_ref, acc_ref):
        m_ref[...] = jnp.full(m_ref.shape, NEG, f32)
        l_ref[...] = jnp.zeros(l_ref.shape, f32)
        acc_ref[...] = jnp.zeros(acc_ref.shape, f32)

    def scores_to(buf, k_ref, kt, q_rows, mask_fn, lanes, krows):
        n = krows.stop - krows.start
        start = pl.multiple_of(kt * tq + krows.start, n)
        s = _dot(k_ref[pl.ds(start, n), :], qa_ref[q_rows, lanes])
        if mask_fn is not None:
            s = jnp.where(mask_fn(krows, lanes), s, NEG)
        buf[0][krows, lanes] = s
        buf[1][:, lanes] = jnp.max(s, axis=0, keepdims=True)

    def softmax_pv(buf, state, v_ref, kt, lanes, krows):
        m_ref, l_ref, acc_ref = state
        m_prev = m_ref[:, lanes]
        m_new = jnp.maximum(m_prev, buf[1][:, lanes])
        alpha = jnp.exp2(m_prev - m_new)
        pt = jnp.exp2(buf[0][krows, lanes] - m_new)
        l_ref[:, lanes] = alpha * l_ref[:, lanes] + jnp.sum(pt, axis=0, keepdims=True)
        acc_ref[:, lanes] = alpha * acc_ref[:, lanes] + _dot(v_ref[kt, :, krows], pt.astype(bf16))
        m_ref[:, lanes] = m_new

    def finish(state, c):
        m_ref, l_ref, acc_ref = state
        for r in range(R):
            out_ref[:, heads[r]] += (gate_row(c, r) * (1.0 / l_ref[:, heads[r]])) * acc_ref[:, heads[r]]

    plain = slice(0, DK)
    aug = slice(0, DK + NSB)

    buf_a = (sd_ref, md_ref)
    buf_b = (sc_ref, mx_ref)

    half = tq // 2
    every = slice(0, tq)
    full_units = [(heads[r], every) for r in range(R)]

    def half_units(rows_lo, rows_hi):
        return [(slice(r * tq + h * half, r * tq + (h + 1) * half), rows)
                for r in range(R) for h, rows in enumerate((rows_lo, rows_hi))]

    diag_units = half_units(slice(0, half), every)
    prev_units = half_units(every, slice(half, tq))

    def tile_scores(buf, k_ref, kt, q_rows, mask_fn, units):
        for lanes, krows in units:
            scores_to(buf, k_ref, kt, q_rows, mask_fn, lanes, krows)

    def tile_softmax(buf, state, v_ref, kt, units):
        for lanes, krows in units:
            softmax_pv(buf, state, v_ref, kt, lanes, krows)

    kpos = lax.broadcasted_iota(jnp.int32, (tq, 1), 0)
    t_all = q0 + (lax.broadcasted_iota(jnp.int32, (1, R * tq), 1) & (tq - 1))

    def causal(krows, lanes):
        return q0 + kpos[krows] <= t_all[:, lanes]

    def inside(krows, lanes):
        return ((q0 - tq) + kpos[krows] > t_all[:, lanes] - WINDOW) & (qi >= 1)

    win = (mw_ref, lw_ref, accw_ref)
    reset(*win)
    prev = jnp.maximum(qi - 1, 0)
    tile_scores(buf_a, kw_ref, qi, plain, causal, diag_units)
    tile_scores(buf_b, kw_ref, prev, plain, inside, prev_units)
    tile_softmax(buf_a, win, vw_ref, qi, diag_units)
    tile_softmax(buf_b, win, vw_ref, prev, prev_units)
    finish(win, 2)

    j_idx = lax.broadcasted_iota(jnp.int32, (NSB, 1), 0)
    j_f = j_idx.astype(f32)
    cur = t1 // SLC
    valid = (j_idx * SLC) <= t1
    forced = (j_idx == 0) | (j_idx == cur) | (j_idx == cur - 1)
    score = jnp.where(valid, imp + jnp.where(forced, FORCE_BONUS, 0.0), -jnp.inf)
    for _ in range(n_sel):
        mx = jnp.max(score, axis=0, keepdims=True)
        first = jnp.min(jnp.where(score == mx, j_f, float(NSB)), axis=0, keepdims=True)
        score = jnp.where(j_f == first, -jnp.inf, score)
    nsel = jnp.where(score == -jnp.inf, 0.0, NEG).astype(bf16)
    for r in range(R):
        qa_ref[DK:DK + NSB, heads[r]] = nsel

    selst = (ms_ref, ls_ref, accs_ref)
    reset(*selst)
    tile_scores(buf_a, ksa_ref, qi, aug, causal, diag_units)
    tile_scores(buf_b, ksa_ref, 0, aug, None, full_units)
    tile_softmax(buf_a, selst, vs_ref, qi, diag_units)

    def sel_body(kt, carry):
        nxt = jnp.minimum(kt + 1, jnp.maximum(qi - 1, 0))
        for lanes, krows in full_units:
            softmax_pv(buf_b, selst, vs_ref, kt, lanes, krows)
            scores_to(buf_b, ksa_ref, nxt, aug, None, lanes, krows)
        return carry

    lax.fori_loop(0, qi, sel_body, 0)
    finish(selst, 1)

    for r in range(R):
        o_ref[:, r * DK:(r + 1) * DK] = out_ref[:, heads[r]].T.astype(bf16)


def _nsa(proj, small, cosf, sinf, ovl_t, kcmp, vcmp_t, ksa, vs_t, kw, vw_t, B, T, tq):
    nq = T // tq
    nc = kcmp.shape[2]
    rows = R * tq
    n_sel = min(N_SELECT, T // SLC)
    assert tq >= WINDOW and nq >= 2

    def per_bg(*shape):
        return pl.BlockSpec((None, None) + shape, lambda b, g, i: (b, g) + (0,) * len(shape))

    tab = pl.BlockSpec((tq, DK), lambda b, g, i: (b * nq + i, 0))
    return pl.pallas_call(
        functools.partial(_nsa_kernel, tq=tq, n_sel=n_sel),
        grid=(B, G, nq),
        in_specs=[
            pl.BlockSpec((tq, R * DK), lambda b, g, i: (b * nq + i, g)),
            tab, tab,
            pl.BlockSpec((tq, 128), lambda b, g, i: (b * nq + i, g)),
            pl.BlockSpec((NSB, nc), lambda b, g, i: (0, 0)),
            per_bg(nc, DK), per_bg(DK, nc),
            per_bg(T, DK + NSB), per_bg(nq, DK, tq), per_bg(T, DK), per_bg(nq, DK, tq),
        ],
        out_specs=pl.BlockSpec((tq, R * DK), lambda b, g, i: (b * nq + i, g)),
        out_shape=jax.ShapeDtypeStruct((B * T, D), bf16),
        scratch_shapes=[pltpu.VMEM((DK + NSB, rows), bf16)]
        + [pltpu.VMEM((1, rows), f32), pltpu.VMEM((1, rows), f32), pltpu.VMEM((DK, rows), f32)] * 2
        + [pltpu.VMEM((DK, rows), f32)]
        + [pltpu.VMEM((tq, rows), f32), pltpu.VMEM((1, rows), f32)] * 2,
        compiler_params=_cparams(("parallel", "parallel", "arbitrary")),
        name="nsa_attention",
    )(proj, cosf, sinf, small, ovl_t, kcmp, vcmp_t, ksa, vs_t, kw, vw_t)


def _ssd_kernel(z_ref, xs_ref, bm_ref, cm_ref, dt_ref,
                wx_ref, bx_ref, wb_ref, bb_ref, wc_ref, bc_ref,
                dtb_ref, alog_ref, dskip_ref, ng_ref, eh_ref, shift_ref,
                o_ref, st_ref, prev_ref):
    c = pl.program_id(1)
    L = SSD_L
    HP = 2 * SSD_P
    GW = D // SSD_G

    @pl.when(c == 0)
    def _():
        st_ref[...] = jnp.zeros_like(st_ref)
        prev_ref[...] = jnp.zeros_like(prev_ref)

    def conv_silu(u_ref, ucols, col, w_ref, b_ref):
        width = ucols.stop - ucols.start
        cur = u_ref[:, ucols]
        ext = jnp.concatenate([prev_ref[:, col:col + width], cur], axis=0)
        sh = _dot(shift_ref[...], ext)
        y = b_ref[:, ucols] + w_ref[SSD_CONV - 1:SSD_CONV, ucols] * cur.astype(f32)
        for j in range(SSD_CONV - 1):
            y = y + w_ref[j:j + 1, ucols] * sh[j * L:(j + 1) * L]
        prev_ref[:, col:col + width] = cur
        return y * _sigmoid(y)

    lane = lax.broadcasted_iota(jnp.int32, (1, 128), 1)
    xdt_in = dt_ref[...] + dtb_ref[...]
    dt = jnp.maximum(xdt_in, 0.0) + jnp.log1p(jnp.exp(-jnp.abs(xdt_in)))
    dt = jnp.where(lane < SSD_HEADS, dt, 0.0)
    da = dt * (-jnp.exp(alog_ref[...]))

    row = lax.broadcasted_iota(jnp.int32, (L, L), 0)
    colm = lax.broadcasted_iota(jnp.int32, (L, L), 1)
    causal = colm <= row
    tril = jnp.where(causal, 1.0, 0.0).astype(bf16)
    d_hi, d_mid, d_lo = _split3(da)
    acs = _dot(tril, d_hi) + _dot(tril, d_mid) + _dot(tril, d_lo)
    acs_t = acs.T
    last = acs[L - 1:L, :]
    dt_b = dt.astype(bf16)
    eacs_b = jnp.exp(acs).astype(bf16)
    dec_b = jnp.exp(last - acs).astype(bf16)
    c_hi, c_lo = _split2(jnp.exp(jnp.broadcast_to(last, (8, 128))))
    lane_hp = lax.broadcasted_iota(jnp.int32, (L, HP), 1)
    gn = SSD_G * SSD_N

    for g in range(SSD_G):
        cols = slice(g * GW, (g + 1) * GW)
        ncols = slice(g * SSD_N, (g + 1) * SSD_N)
        xs = conv_silu(xs_ref, cols, g * GW, wx_ref, bx_ref)
        bm_g = conv_silu(bm_ref, ncols, D + g * SSD_N, wb_ref, bb_ref)
        cm_g = conv_silu(cm_ref, ncols, D + gn + g * SSD_N, wc_ref, bc_ref).astype(bf16)
        eh = eh_ref[:, cols]
        xdt = xs * _dot(dt_b, eh)
        xdt_b = xdt.astype(bf16)
        xdec_b = (xdt * _dot(dec_b, eh)).astype(bf16)
        eacs_x = _dot(eacs_b, eh)
        cdec_x = (_dot(c_hi, eh) + _dot(c_lo, eh))[0:1, :]

        cb = _dot_nt(cm_g, bm_g.astype(bf16))
        s_in = st_ref[g]
        y_off = _dot(cm_g, s_in.astype(bf16))
        st_ref[g] = s_in * cdec_x + _dot(bm_g.T.astype(bf16), xdec_b)
        ys = []
        for pp in range(GW // HP):
            h0 = g * (GW // SSD_P) + 2 * pp
            pc = slice(pp * HP, (pp + 1) * HP)
            ms = []
            for h in (h0, h0 + 1):
                diff = acs[:, h:h + 1] - acs_t[h:h + 1, :]
                ms.append(jnp.where(causal, cb * jnp.exp(jnp.where(causal, diff, NEG)), 0.0))
            lhs = jnp.concatenate(ms, axis=1).astype(bf16)
            xp = xdt_b[:, pc]
            zero = jnp.zeros_like(xp)
            rhs = jnp.concatenate([jnp.where(lane_hp < SSD_P, xp, zero),
                                   jnp.where(lane_hp >= SSD_P, xp, zero)], axis=0)
            ys.append(_dot(lhs, rhs) + eacs_x[:, pc] * y_off[:, pc]
                      + dskip_ref[:, g * GW + pp * HP:g * GW + (pp + 1) * HP] * xs[:, pc])
        z = z_ref[:, cols].astype(f32)
        yg = jnp.concatenate(ys, axis=1) * (z * _sigmoid(z))
        ms = jnp.mean(yg * yg, axis=-1, keepdims=True)
        o_ref[:, cols] = (yg * lax.rsqrt(ms + EPS) * ng_ref[:, cols]).astype(bf16)


def _ssd(proj, small, wts, B, T):
    nch = T // SSD_L
    L = SSD_L

    def main(name, w):
        base = _blk(name, w)
        return pl.BlockSpec((L, w), lambda b, c: (b * nch + c, base))

    def full(a):
        return pl.BlockSpec(a.shape, lambda b, c: (0,) * a.ndim)

    gn = SSD_G * SSD_N
    return pl.pallas_call(
        _ssd_kernel,
        grid=(B, nch),
        in_specs=[main('s_z', D), main('xs', D), main('bm', gn), main('cm', gn),
                  pl.BlockSpec((L, 128), lambda b, c: (b * nch + c, G))] + [full(a) for a in wts],
        out_specs=pl.BlockSpec((L, D), lambda b, c: (b * nch + c, 0)),
        out_shape=jax.ShapeDtypeStruct((B * T, D), bf16),
        scratch_shapes=[pltpu.VMEM((SSD_G, SSD_N, D // SSD_G), f32),
                        pltpu.VMEM((L, D + 2 * gn), bf16)],
        compiler_params=_cparams(("parallel", "arbitrary")),
        name="ssd",
    )(proj, proj, proj, proj, small, *wts)


def _merge_kernel(ya_ref, cb_ref, cc_ref, cu_ref, cch_ref, cuh_ref, yc_ref,
                  g0_ref, g1_ref, g2_ref, x_ref, cw_ref, wo_ref, o_ref, ext_ref, *, tm, tiles_per_seq):
    first = (pl.program_id(0) % tiles_per_seq) == 0
    up = lambda ref: ref[...].astype(f32)
    ext_ref[0:HALO, :] = jnp.where(first, 0.0, up(cch_ref) * up(cuh_ref))
    ext_ref[HALO:HALO + tm, :] = up(cc_ref) * up(cu_ref)
    lo = HALO - (CONV_W - 1)
    conv = cw_ref[0:1, :] * ext_ref[lo:lo + tm, :]
    for j in range(1, CONV_W):
        conv = conv + cw_ref[j:j + 1, :] * ext_ref[lo + j:lo + j + tm, :]
    yb = up(cb_ref) * conv
    merged = (_sigmoid(up(g0_ref)) * up(ya_ref) + _sigmoid(up(g1_ref)) * yb
              + _sigmoid(up(g2_ref)) * up(yc_ref))
    o_ref[...] = x_ref[...] + _dot(merged.astype(bf16), wo_ref[...])


def _merge(ya, yc, proj, x2, cw, wo, layer, T, tm):
    M = x2.shape[0]

    def col(name):
        base = _blk(name, D)
        return pl.BlockSpec((tm, D), lambda i: (i, base))

    def halo(name):
        base = _blk(name, D)
        return pl.BlockSpec((HALO, D), lambda i: (jnp.maximum(i * (tm // HALO) - 1, 0), base))

    row = pl.BlockSpec((tm, D), lambda i: (i, 0))
    return pl.pallas_call(
        functools.partial(_merge_kernel, tm=tm, tiles_per_seq=T // tm),
        grid=(M // tm,),
        in_specs=[row, col('cb'), col('cc'), col('cu'), halo('cc'), halo('cu'), row,
                  col('gm0'), col('gm1'), col('gm2'), row,
                  pl.BlockSpec((CONV_W, D), lambda i: (0, 0)),
                  pl.BlockSpec((None, D, D), lambda i: (layer, 0, 0))],
        out_specs=row,
        out_shape=jax.ShapeDtypeStruct((M, D), f32),
        scratch_shapes=[pltpu.VMEM((HALO + tm, D), f32)],
        compiler_params=_cparams(("parallel",)),
        name="merge_oproj",
    )(ya, proj, proj, proj, proj, proj, yc, proj, proj, proj, x2, cw, wo)


def _ffn_kernel(x_ref, g_ref, wu_ref, wd_ref, o_ref, h_ref):
    @pl.when(pl.program_id(1) == 0)
    def _():
        x = x_ref[...]
        h_ref[...] = _rms(x, g_ref[...]).astype(bf16)
        o_ref[...] = x

    u = jnp.maximum(_dot(h_ref[...], wu_ref[...]), 0.0)
    o_ref[...] += _dot((u * u).astype(bf16), wd_ref[...])


def _ffn(x2, g, wu, wd, layer, tm, tf):
    M = x2.shape[0]
    return pl.pallas_call(
        _ffn_kernel,
        grid=(M // tm, D_FF // tf),
        in_specs=[pl.BlockSpec((tm, D), lambda i, j: (i, 0)),
                  pl.BlockSpec((1, D), lambda i, j: (0, 0)),
                  pl.BlockSpec((None, D, tf), lambda i, j: (layer, 0, j)),
                  pl.BlockSpec((None, tf, D), lambda i, j: (layer, j, 0))],
        out_specs=pl.BlockSpec((tm, D), lambda i, j: (i, 0)),
        out_shape=jax.ShapeDtypeStruct((M, D), f32),
        scratch_shapes=[pltpu.VMEM((tm, D), bf16)],
        compiler_params=_cparams(("parallel", "arbitrary")),
        name="ffn",
    )(x2, g, wu, wd)


def _ple_kernel(x_ref, p_ref, g_ref, wp_ref, wg_ref, gf_ref, o_ref, *, final):
    x = x_ref[...]
    gate = _sigmoid(_dot(_rms(x, g_ref[...]).astype(bf16), wg_ref[...]))
    y = x + _dot(p_ref[...].astype(bf16), wp_ref[...]) * gate
    if final:
        y = _rms(y, gf_ref[...])
    o_ref[...] = y


def _ple(x2, p2, layer, g, wp, wg, gf, tm, final):
    M = x2.shape[0]
    base = layer * (M // tm)
    row = pl.BlockSpec((tm, D), lambda i: (i, 0))
    vec = pl.BlockSpec((1, D), lambda i: (0, 0))
    return pl.pallas_call(
        functools.partial(_ple_kernel, final=final),
        grid=(M // tm,),
        in_specs=[row, pl.BlockSpec((tm, PLE), lambda i: (base + i, 0)), vec,
                  pl.BlockSpec((None, PLE, D), lambda i: (layer, 0, 0)),
                  pl.BlockSpec((None, D, D), lambda i: (layer, 0, 0)), vec],
        out_specs=row,
        out_shape=jax.ShapeDtypeStruct((M, D), f32),
        compiler_params=_cparams(("parallel",)),
        name="ple",
    )(x2, p2, g, wp, wg, gf)


def _source_runs():
    runs = []
    for n in _ORDER:
        a, wd = _SRC[n]
        if runs and runs[-1][1] == a:
            runs[-1][1] = a + wd
        else:
            runs.append([a, a + wd])
    return runs


RELAYOUT_TILE = 1024


def _relayout_tables():
    main_blk, extra_blk, shift = [], [], []
    for a, b in _source_runs():
        sh = a % 128
        a0 = a - sh
        assert a0 % RELAYOUT_TILE == 0 and (b - a) % RELAYOUT_TILE == 0
        for k in range((b - a) // RELAYOUT_TILE):
            main_blk.append(a0 // RELAYOUT_TILE + k)
            extra_blk.append((a0 + RELAYOUT_TILE * (k + 1)) // 128)
            shift.append(sh)
    shifts = sorted(set(shift))
    return np.array([main_blk, extra_blk, [shifts.index(s) for s in shift]], np.int32), shifts


def _relayout_kernel(tab_ref, main_ref, extra_ref, o_ref, *, shifts):
    x = jnp.concatenate([main_ref[...], extra_ref[...]], axis=1)
    branches = [functools.partial(lambda s: x[:, s:s + RELAYOUT_TILE].astype(bf16), s) for s in shifts]
    o_ref[...] = lax.switch(tab_ref[2, pl.program_id(2)], branches)


def _w_relayout(w, tr):
    depth = w.shape[0]
    tab, shifts = _relayout_tables()
    grid_spec = pltpu.PrefetchScalarGridSpec(
        num_scalar_prefetch=1, grid=(depth, D // tr, tab.shape[1]),
        in_specs=[pl.BlockSpec((None, tr, RELAYOUT_TILE), lambda l, i, j, t: (l, i, t[0, j])),
                  pl.BlockSpec((None, tr, 128), lambda l, i, j, t: (l, i, t[1, j]))],
        out_specs=pl.BlockSpec((None, tr, RELAYOUT_TILE), lambda l, i, j, t: (l, i, j)))
    return pl.pallas_call(
        functools.partial(_relayout_kernel, shifts=shifts), grid_spec=grid_spec,
        out_shape=jax.ShapeDtypeStruct((depth, D, N_MAIN), bf16),
        compiler_params=_cparams(("parallel", "parallel", "parallel")),
        name="w_relayout",
    )(jnp.asarray(tab), w, w)


def _prep_w_in(w):
    depth = w.shape[0]
    main = _w_relayout(w, 512)
    a, wd = _SRC['g_nsa']
    gn = w[:, :, a:a + wd].reshape(depth, D, G, R * 3)
    gn = jnp.pad(gn, ((0, 0), (0, 0), (0, 0), (0, 128 - R * 3))).reshape(depth, D, G * 128)
    a, wd = _SRC['s_dt']
    dt = jnp.pad(w[:, :, a:a + wd], ((0, 0), (0, 0), (0, 128 - wd)))
    return main, jnp.concatenate([gn, dt], axis=2).astype(bf16)


def _overlap_matrix(nc):
    i = np.arange(nc)[:, None]
    j = np.arange(NSB)[None, :]
    ovl = (i * CMP_STRIDE < j * SLC + SLC) & (i * CMP_STRIDE + CMP_BLOCK > j * SLC)
    return jnp.asarray(ovl.T.astype(np.float32), dtype=bf16)


def _head_expand():
    h = np.arange(128)[:, None]
    ch = np.arange(D)[None, :]
    return jnp.asarray((ch // SSD_P == h).astype(np.float32), dtype=bf16)


def _conv_shift():
    i = np.arange((SSD_CONV - 1) * SSD_L)[:, None]
    r = np.arange(2 * SSD_L)[None, :]
    pick = r == SSD_L + i % SSD_L - (SSD_CONV - 1) + i // SSD_L
    return jnp.asarray(pick.astype(np.float32), dtype=bf16)


def kernel(x, p, positions, g_mix, w_in, nsa_pe_k, nsa_pe_v, phi_k_w1, phi_k_b1, phi_k_w2, phi_k_b2,
           phi_v_w1, phi_v_b1, phi_v_w2, phi_v_b2, sconv_w, ssd_conv_w, ssd_conv_b, ssd_dt_bias,
           ssd_a_log, ssd_d, ssd_norm_g, w_o, g_mlp, w_up, w_down, g_ple, w_ple, w_ple_gate, g_final):
    B, T, _ = x.shape
    depth = w_in.shape[0]
    M = B * T
    assert T % 256 == 0 and T // SLC <= NSB
    tq = 512
    tm_proj = 1024 if M % 1024 == 0 else 256
    tm = 512 if T % 512 == 0 else 256
    nc = T // CMP_STRIDE

    inv_freq = 1.0 / (10000.0 ** (jnp.arange(0, DK, 2, dtype=f32) / DK))
    ang = positions.astype(f32)[..., None] * inv_freq
    cosf = jnp.concatenate([jnp.cos(ang), jnp.cos(ang)], axis=-1).reshape(M, DK)
    sinf = jnp.concatenate([-jnp.sin(ang), jnp.sin(ang)], axis=-1).reshape(M, DK)
    ovl = _overlap_matrix(nc)
    eh = _head_expand()
    shift = _conv_shift()

    def vec(a, n=None):
        a = a.reshape(1, -1).astype(f32)
        return a if n is None else jnp.pad(a, ((0, 0), (0, n - a.shape[1])))

    x2 = x.reshape(M, D)
    p2 = p.reshape(depth * M, PLE)
    w_main, w_small = _prep_w_in(w_in)
    w_o16, w_up16, w_down16 = w_o.astype(bf16), w_up.astype(bf16), w_down.astype(bf16)
    w_ple16, w_gate16 = w_ple.astype(bf16), w_ple_gate.astype(bf16)
    for i in range(depth):
        proj, small = _proj(x2, vec(g_mix[i]), w_main, w_small, i, tm_proj, 2048)

        kc, vc, ksa, vs, kw, vw = _prep(proj, cosf, sinf, B, T, tq)

        def phi(pe, w1, b1, w2, b2):
            pe8 = jnp.broadcast_to(pe.reshape(1, -1), (8, CMP_BLOCK * DK)).astype(bf16)
            return (pe8, w1.astype(bf16), vec(b1), w2.astype(bf16), vec(b2))

        kcmp, vcmp = _compress(kc, vc,
                               phi(nsa_pe_k[i], phi_k_w1[i], phi_k_b1[i], phi_k_w2[i], phi_k_b2[i]),
                               phi(nsa_pe_v[i], phi_v_w1[i], phi_v_b1[i], phi_v_w2[i], phi_v_b2[i]))
        ya = _nsa(proj, small, cosf, sinf, ovl, kcmp, vcmp, ksa, vs, kw, vw, B, T, tq)

        cw, cbias = ssd_conv_w[i], ssd_conv_b[i]
        gn = SSD_G * SSD_N
        ssd_w = (cw[:, :D], vec(cbias[:D]), cw[:, D:D + gn], vec(cbias[D:D + gn]),
                 cw[:, D + gn:], vec(cbias[D + gn:]),
                 vec(ssd_dt_bias[i], 128), vec(ssd_a_log[i], 128),
                 vec(jnp.repeat(ssd_d[i], SSD_P)), vec(ssd_norm_g[i]), eh, shift)
        yc = _ssd(proj, small, ssd_w, B, T)

        x2 = _merge(ya, yc, proj, x2, sconv_w[i], w_o16, i, T, 256)
        x2 = _ffn(x2, vec(g_mlp[i]), w_up16, w_down16, i, tm_proj, 512)
        x2 = _ple(x2, p2, i, vec(g_ple[i]), w_ple16, w_gate16, vec(g_final), tm, final=(i == depth - 1))
    return x2.reshape(B, T, D)
```

```python
import functools

import numpy as np
import jax
import jax.numpy as jnp
from jax import lax
from jax.experimental import pallas as pl
from jax.experimental.pallas import tpu as pltpu

f32 = jnp.float32
bf16 = jnp.bfloat16

D = 2048
N_HEADS = 16
DK = 128
G = 4
R = N_HEADS // G
CMP_BLOCK = 32
CMP_STRIDE = 16
SLC = 64
N_SELECT = 16
WINDOW = 512
FORCE_BONUS = 1.0e4
CONV_W = 3
SSD_HEADS = 32
SSD_P = 64
SSD_G = 4
SSD_N = 128
SSD_CONV = 4
SSD_L = 128
D_FF = 4 * D
PLE = 256
EPS = 1e-6
NSB = 128
HALO = 16
DKX = DK + 16
NEG = -1e30
LOG2E = 1.4426950408889634

VMEM_LIMIT = 56 * 1024 * 1024

_SRC = {
    'q': (0, 2048), 'k_c': (2048, 512), 'v_c': (2560, 512), 'k_s': (3072, 512), 'v_s': (3584, 512),
    'k_w': (4096, 512), 'v_w': (4608, 512), 'g_nsa': (5120, 48),
    'cb': (5168, 2048), 'cc': (7216, 2048), 'cu': (9264, 2048),
    's_z': (11312, 2048), 'xs': (13360, 2048), 'bm': (15408, 512), 'cm': (15920, 512),
    's_dt': (16432, 32), 'gm0': (16464, 2048), 'gm1': (18512, 2048), 'gm2': (20560, 2048),
}
_ORDER = ['q', 'cb', 'cc', 'cu', 's_z', 'xs', 'gm0', 'gm1', 'gm2',
          'k_c', 'v_c', 'k_s', 'v_s', 'k_w', 'v_w', 'bm', 'cm']
_OFF = {}
_o = 0
for _n in _ORDER:
    _OFF[_n] = _o
    _o += _SRC[_n][1]
N_MAIN = _o
N_SMALL = G * 128 + 128


def _blk(name, width):
    off = _OFF[name]
    assert off % width == 0
    return off // width


def _cparams(sem):
    return pltpu.CompilerParams(dimension_semantics=sem, vmem_limit_bytes=VMEM_LIMIT)


def _dot(a, b):
    return jnp.dot(a, b, preferred_element_type=f32)


def _dot_nt(a, b):
    return lax.dot_general(a, b, (((1,), (1,)), ((), ())), preferred_element_type=f32)


def _sigmoid(x):
    return jax.nn.sigmoid(x)


def _split2(x):
    hi = x.astype(bf16)
    lo = (x - hi.astype(f32)).astype(bf16)
    return hi, lo


def _split3(x):
    hi = x.astype(bf16)
    r1 = x - hi.astype(f32)
    mid = r1.astype(bf16)
    lo = (r1 - mid.astype(f32)).astype(bf16)
    return hi, mid, lo


def _rms(x, g):
    ms = jnp.mean(x * x, axis=-1, keepdims=True)
    return x * lax.rsqrt(ms + EPS) * g


def _proj_kernel(x_ref, g_ref, w_ref, ws_ref, o_ref, os_ref, h_ref):
    @pl.when(pl.program_id(1) == 0)
    def _():
        h = _rms(x_ref[...], g_ref[...]).astype(bf16)
        h_ref[...] = h
        os_ref[...] = _dot(h, ws_ref[...])

    o_ref[...] = _dot(h_ref[...], w_ref[...]).astype(bf16)


def _proj(x2, g, w_main, w_small, layer, tm, tn):
    M = x2.shape[0]
    return pl.pallas_call(
        _proj_kernel,
        grid=(M // tm, N_MAIN // tn),
        in_specs=[
            pl.BlockSpec((tm, D), lambda i, j: (i, 0), pipeline_mode=pl.Buffered(1)),
            pl.BlockSpec((1, D), lambda i, j: (0, 0)),
            pl.BlockSpec((None, D, tn), lambda i, j: (layer, 0, j)),
            pl.BlockSpec((None, D, N_SMALL), lambda i, j: (layer, 0, 0), pipeline_mode=pl.Buffered(1)),
        ],
        out_specs=[
            pl.BlockSpec((tm, tn), lambda i, j: (i, j)),
            pl.BlockSpec((tm, N_SMALL), lambda i, j: (i, 0)),
        ],
        out_shape=[jax.ShapeDtypeStruct((M, N_MAIN), bf16),
                   jax.ShapeDtypeStruct((M, N_SMALL), f32)],
        scratch_shapes=[pltpu.VMEM((tm, D), bf16)],
        compiler_params=_cparams(("parallel", "arbitrary")),
        name="proj",
    )(x2, g, w_main, w_small)


def _rope(x, cosf, sinf):
    x = x.astype(f32)
    return x * cosf + pltpu.roll(x, DK // 2, axis=1) * sinf


def _prep_kernel(kc_ref, vc_ref, ks_ref, vs_ref, kw_ref, vw_ref, cos_ref, sin_ref,
                 kco_ref, vco_ref, ksa_ref, vso_ref, kwo_ref, vwo_ref, stage_ref, *, tt):
    cosf = cos_ref[...]
    sinf = sin_ref[...]
    t = pl.program_id(1) * tt + lax.broadcasted_iota(jnp.int32, (tt, NSB), 0)
    j = lax.broadcasted_iota(jnp.int32, (tt, NSB), 1)
    onehot = jnp.where(t // SLC == j, 1.0, 0.0).astype(bf16)
    rows = tt // CMP_STRIDE
    ones_rows = jnp.where(lax.broadcasted_iota(jnp.int32, (DKX - DK, tt), 0) == 0, 1.0, 0.0).astype(bf16)

    def to_block_rows(x, o_ref, g):
        stage_ref[...] = x
        for l in range(CMP_STRIDE):
            o_ref[g, :, l * DK:(l + 1) * DK] = stage_ref[pl.ds(l, rows, stride=CMP_STRIDE), :].astype(bf16)

    for g in range(G):
        cols = slice(g * DK, (g + 1) * DK)
        to_block_rows(_rope(kc_ref[:, cols], cosf, sinf), kco_ref, g)
        to_block_rows(vc_ref[:, cols].astype(f32), vco_ref, g)
        ksa_ref[g, :, 0:DK] = _rope(ks_ref[:, cols], cosf, sinf).astype(bf16)
        ksa_ref[g, :, DK:DK + NSB] = onehot
        vso_ref[g, 0:DK] = vs_ref[:, cols].astype(f32).T.astype(bf16)
        vso_ref[g, DK:DKX] = ones_rows
        kwo_ref[g] = _rope(kw_ref[:, cols], cosf, sinf).astype(bf16)
        vwo_ref[g, 0:DK] = vw_ref[:, cols].astype(f32).T.astype(bf16)
        vwo_ref[g, DK:DKX] = ones_rows


def _prep(proj, cosf, sinf, B, T, tt):
    nt = T // tt
    gw = G * DK
    rows = tt // CMP_STRIDE

    def col(name):
        base = _blk(name, gw)
        return pl.BlockSpec((tt, gw), lambda b, i: (b * nt + i, base))

    tab = pl.BlockSpec((tt, DK), lambda b, i: (b * nt + i, 0))

    def out(n, w):
        return pl.BlockSpec((None, G, n, w), lambda b, i: (b, 0, i, 0))

    out_t = pl.BlockSpec((None, G, None, DKX, tt), lambda b, i: (b, 0, i, 0, 0))
    return pl.pallas_call(
        functools.partial(_prep_kernel, tt=tt),
        grid=(B, nt),
        in_specs=[col('k_c'), col('v_c'), col('k_s'), col('v_s'), col('k_w'), col('v_w'), tab, tab],
        out_specs=[out(rows, CMP_STRIDE * DK), out(rows, CMP_STRIDE * DK), out(tt, DK + NSB), out_t,
                   out(tt, DK), out_t],
        out_shape=[jax.ShapeDtypeStruct((B, G, T // CMP_STRIDE, CMP_STRIDE * DK), bf16),
                   jax.ShapeDtypeStruct((B, G, T // CMP_STRIDE, CMP_STRIDE * DK), bf16),
                   jax.ShapeDtypeStruct((B, G, T, DK + NSB), bf16),
                   jax.ShapeDtypeStruct((B, G, nt, DKX, tt), bf16),
                   jax.ShapeDtypeStruct((B, G, T, DK), bf16),
                   jax.ShapeDtypeStruct((B, G, nt, DKX, tt), bf16)],
        scratch_shapes=[pltpu.VMEM((tt, DK), f32)],
        compiler_params=_cparams(("parallel", "parallel")),
        name="kv_prep",
    )(proj, proj, proj, proj, proj, proj, cosf, sinf)


def _compress_one(x_ref, pe_ref, w1_ref, b1_ref, w2_ref, b2_ref, o_ref, transpose):
    nc = x_ref.shape[0]
    half = CMP_STRIDE * DK
    x = x_ref[...]
    first = _dot(x, w1_ref[0:half, :])
    second = _dot(x, w1_ref[half:2 * half, :])
    second = pltpu.roll(second, nc - 1, axis=0)
    pe = _dot(pe_ref[...], w1_ref[...])[0:1, :]
    h = first + second + pe + b1_ref[...]
    h = h * _sigmoid(h)
    y = _dot(h.astype(bf16), w2_ref[...]) + b2_ref[...]
    o_ref[...] = (y.T if transpose else y).astype(bf16)


def _compress_kernel(k_ref, v_ref, pek, w1k, b1k, w2k, b2k, pev, w1v, b1v, w2v, b2v, ko_ref, vo_ref):
    _compress_one(k_ref, pek, w1k, b1k, w2k, b2k, ko_ref, False)
    _compress_one(v_ref, pev, w1v, b1v, w2v, b2v, vo_ref, True)


def _compress(kc2, vc2, wk, wv):
    B, _, nc, width = kc2.shape
    xin = pl.BlockSpec((None, None, nc, width), lambda b, g: (b, g, 0, 0))

    def full(a):
        return pl.BlockSpec(a.shape, lambda b, g: (0,) * a.ndim)

    out = pl.BlockSpec((None, None, nc, DK), lambda b, g: (b, g, 0, 0))
    out_t = pl.BlockSpec((None, None, DK, nc), lambda b, g: (b, g, 0, 0))
    return pl.pallas_call(
        _compress_kernel,
        grid=(B, G),
        in_specs=[xin, xin] + [full(a) for a in wk] + [full(a) for a in wv],
        out_specs=[out, out_t],
        out_shape=[jax.ShapeDtypeStruct((B, G, nc, DK), bf16),
                   jax.ShapeDtypeStruct((B, G, DK, nc), bf16)],
        compiler_params=_cparams(("parallel", "parallel")),
        name="compress",
    )(kc2, vc2, *wk, *wv)


def _nsa_kernel(q_ref, cos_ref, sin_ref, gate_ref, ovl_ref, kcmp_ref, vcmp_ref,
                ksa_ref, vs_ref, kw_ref, vw_ref, o_ref,
                qa_ref, ms_ref, ls_ref, accs_ref, mw_ref, lw_ref, accw_ref, out_ref,
                sc_ref, mx_ref, sd_ref, md_ref,
                *, tq, n_sel):
    qi = pl.program_id(2)
    q0 = qi * tq
    heads = [slice(r * tq, (r + 1) * tq) for r in range(R)]
    qscale = DK ** -0.5 * LOG2E
    cosf = cos_ref[...]
    sinf = sin_ref[...]
    for r in range(R):
        qr = _rope(q_ref[:, r * DK:(r + 1) * DK], cosf, sinf) * qscale
        qa_ref[0:DK, heads[r]] = qr.T.astype(bf16)
    t1 = q0 + lax.broadcasted_iota(jnp.int32, (1, tq), 1)
    gate_t = _sigmoid(gate_ref[...]).T

    def gate_row(c, r):
        return gate_t[3 * r + c:3 * r + c + 1, :]

    nc = kcmp_ref.shape[0]
    n_idx = lax.broadcasted_iota(jnp.int32, (nc, 1), 0)
    vis = (n_idx * CMP_STRIDE + (CMP_BLOCK - 1)) <= t1
    has_any = t1 >= CMP_BLOCK - 1
    psum = jnp.zeros((nc, tq), f32)
    for r in range(R):
        s = jnp.where(vis, _dot(kcmp_ref[...], qa_ref[0:DK, heads[r]]), NEG)
        e = jnp.exp2(s - jnp.max(s, axis=0, keepdims=True))
        inv = jnp.where(has_any, 1.0 / jnp.sum(e, axis=0, keepdims=True), 0.0)
        out_ref[:, heads[r]] = (gate_row(0, r) * inv) * _dot(vcmp_ref[...], e.astype(bf16))
        psum = psum + e * inv
    p_hi, p_lo = _split2(psum)
    imp = _dot(ovl_ref[...], p_hi) + _dot(ovl_ref[...], p_lo)

    def reset(m_ref, l_ref, acc_ref):
        m_ref[...] = jnp.full(m_ref.shape, NEG, f32)
        l_ref[...] = jnp.zeros(l_ref.shape, f32)
        acc_ref[...] = jnp.zeros(acc_ref.shape, f32)

    def scores_to(buf, k_ref, kt, q_rows, mask_fn, lanes, krows):
        n = krows.stop - krows.start
        start = pl.multiple_of(kt * tq + krows.start, n)
        s = _dot(k_ref[pl.ds(start, n), :], qa_ref[q_rows, lanes])
        if mask_fn is not None:
            s = jnp.where(mask_fn(krows, lanes), s, NEG)
        buf[0][krows, lanes] = s
        buf[1][:, lanes] = jnp.max(s, axis=0, keepdims=True)

    def softmax_pv(buf, state, v_ref, kt, lanes, krows):
        m_ref, l_ref, acc_ref = state
        m_prev = m_ref[:, lanes]
        m_new = jnp.maximum(m_prev, buf[1][:, lanes])
        alpha = jnp.exp2(m_prev - m_new)
        pt = jnp.exp2(buf[0][krows, lanes] - m_new)
        acc_ref[:, lanes] = alpha * acc_ref[:, lanes] + _dot(v_ref[kt, :, krows], pt.astype(bf16))
        m_ref[:, lanes] = m_new

    def finish(state, c):
        m_ref, l_ref, acc_ref = state
        for r in range(R):
            out_ref[:, heads[r]] += (gate_row(c, r) * (1.0 / acc_ref[DK:DK + 1, heads[r]])) * acc_ref[0:DK, heads[r]]

    plain = slice(0, DK)
    aug = slice(0, DK + NSB)

    buf_a = (sd_ref, md_ref)
    buf_b = (sc_ref, mx_ref)

    half = tq // 2
    every = slice(0, tq)
    full_units = [(heads[r], every) for r in range(R)]

    def half_units(rows_lo, rows_hi):
        return [(slice(r * tq + h * half, r * tq + (h + 1) * half), rows)
                for r in range(R) for h, rows in enumerate((rows_lo, rows_hi))]

    diag_units = half_units(slice(0, half), every)
    prev_units = half_units(every, slice(half, tq))

    def tile_scores(buf, k_ref, kt, q_rows, mask_fn, units):
        for lanes, krows in units:
            scores_to(buf, k_ref, kt, q_rows, mask_fn, lanes, krows)

    def tile_softmax(buf, state, v_ref, kt, units):
        for lanes, krows in units:
            softmax_pv(buf, state, v_ref, kt, lanes, krows)

    kpos = lax.broadcasted_iota(jnp.int32, (tq, 1), 0)
    t_all = q0 + (lax.broadcasted_iota(jnp.int32, (1, R * tq), 1) & (tq - 1))

    def causal(krows, lanes):
        return q0 + kpos[krows] <= t_all[:, lanes]

    def inside(krows, lanes):
        return ((q0 - tq) + kpos[krows] > t_all[:, lanes] - WINDOW) & (qi >= 1)

    win = (mw_ref, lw_ref, accw_ref)
    reset(*win)
    prev = jnp.maximum(qi - 1, 0)
    tile_scores(buf_a, kw_ref, qi, plain, causal, diag_units)
    tile_scores(buf_b, kw_ref, prev, plain, inside, prev_units)
    tile_softmax(buf_a, win, vw_ref, qi, diag_units)
    tile_softmax(buf_b, win, vw_ref, prev, prev_units)
    finish(win, 2)

    j_idx = lax.broadcasted_iota(jnp.int32, (NSB, 1), 0)
    j_f = j_idx.astype(f32)
    cur = t1 // SLC
    valid = (j_idx * SLC) <= t1
    forced = (j_idx == 0) | (j_idx == cur) | (j_idx == cur - 1)
    score = jnp.where(valid, imp + jnp.where(forced, FORCE_BONUS, 0.0), -jnp.inf)
    for _ in range(n_sel):
        mx = jnp.max(score, axis=0, keepdims=True)
        first = jnp.min(jnp.where(score == mx, j_f, float(NSB)), axis=0, keepdims=True)
        score = jnp.where(j_f == first, -jnp.inf, score)
    nsel = jnp.where(score == -jnp.inf, 0.0, NEG).astype(bf16)
    for r in range(R):
        qa_ref[DK:DK + NSB, heads[r]] = nsel

    selst = (ms_ref, ls_ref, accs_ref)
    reset(*selst)
    tile_scores(buf_a, ksa_ref, qi, aug, causal, diag_units)
    tile_scores(buf_b, ksa_ref, 0, aug, None, full_units)
    tile_softmax(buf_a, selst, vs_ref, qi, diag_units)

    def sel_body(kt, carry):
        nxt = jnp.minimum(kt + 1, jnp.maximum(qi - 1, 0))
        for lanes, krows in full_units:
            softmax_pv(buf_b, selst, vs_ref, kt, lanes, krows)
            scores_to(buf_b, ksa_ref, nxt, aug, None, lanes, krows)
        return carry

    lax.fori_loop(0, qi, sel_body, 0)
    finish(selst, 1)

    for r in range(R):
        o_ref[:, r * DK:(r + 1) * DK] = out_ref[:, heads[r]].T.astype(bf16)


def _nsa(proj, small, cosf, sinf, ovl_t, kcmp, vcmp_t, ksa, vs_t, kw, vw_t, B, T, tq):
    nq = T // tq
    nc = kcmp.shape[2]
    rows = R * tq
    n_sel = min(N_SELECT, T // SLC)
    assert tq >= WINDOW and nq >= 2

    def per_bg(*shape):
        return pl.BlockSpec((None, None) + shape, lambda b, g, i: (b, g) + (0,) * len(shape))

    tab = pl.BlockSpec((tq, DK), lambda b, g, i: (b * nq + i, 0))
    return pl.pallas_call(
        functools.partial(_nsa_kernel, tq=tq, n_sel=n_sel),
        grid=(B, G, nq),
        in_specs=[
            pl.BlockSpec((tq, R * DK), lambda b, g, i: (b * nq + i, g)),
            tab, tab,
            pl.BlockSpec((tq, 128), lambda b, g, i: (b * nq + i, g)),
            pl.BlockSpec((NSB, nc), lambda b, g, i: (0, 0)),
            per_bg(nc, DK), per_bg(DK, nc),
            per_bg(T, DK + NSB), per_bg(nq, DKX, tq), per_bg(T, DK), per_bg(nq, DKX, tq),
        ],
        out_specs=pl.BlockSpec((tq, R * DK), lambda b, g, i: (b * nq + i, g)),
        out_shape=jax.ShapeDtypeStruct((B * T, D), bf16),
        scratch_shapes=[pltpu.VMEM((DK + NSB, rows), bf16)]
        + [pltpu.VMEM((1, rows), f32), pltpu.VMEM((1, rows), f32), pltpu.VMEM((DKX, rows), f32)] * 2
        + [pltpu.VMEM((DK, rows), f32)]
        + [pltpu.VMEM((tq, rows), f32), pltpu.VMEM((1, rows), f32)] * 2,
        compiler_params=_cparams(("parallel", "parallel", "arbitrary")),
        name="nsa_attention",
    )(proj, cosf, sinf, small, ovl_t, kcmp, vcmp_t, ksa, vs_t, kw, vw_t)


def _ssd_kernel(z_ref, xs_ref, bm_ref, cm_ref, dt_ref,
                wx_ref, bx_ref, wb_ref, bb_ref, wc_ref, bc_ref,
                dtb_ref, alog_ref, dskip_ref, ng_ref, eh_ref, shift_ref,
                o_ref, st_ref, prev_ref):
    c = pl.program_id(1)
    L = SSD_L
    HP = 2 * SSD_P
    GW = D // SSD_G

    @pl.when(c == 0)
    def _():
        st_ref[...] = jnp.zeros_like(st_ref)
        prev_ref[...] = jnp.zeros_like(prev_ref)

    def conv_silu(u_ref, ucols, col, w_ref, b_ref):
        width = ucols.stop - ucols.start
        cur = u_ref[:, ucols]
        ext = jnp.concatenate([prev_ref[:, col:col + width], cur], axis=0)
        sh = _dot(shift_ref[...], ext)
        y = b_ref[:, ucols] + w_ref[SSD_CONV - 1:SSD_CONV, ucols] * cur.astype(f32)
        for j in range(SSD_CONV - 1):
            y = y + w_ref[j:j + 1, ucols] * sh[j * L:(j + 1) * L]
        prev_ref[:, col:col + width] = cur
        return y * _sigmoid(y)

    lane = lax.broadcasted_iota(jnp.int32, (1, 128), 1)
    xdt_in = dt_ref[...] + dtb_ref[...]
    dt = jnp.maximum(xdt_in, 0.0) + jnp.log1p(jnp.exp(-jnp.abs(xdt_in)))
    dt = jnp.where(lane < SSD_HEADS, dt, 0.0)
    da = dt * (-jnp.exp(alog_ref[...]))

    row = lax.broadcasted_iota(jnp.int32, (L, L), 0)
    colm = lax.broadcasted_iota(jnp.int32, (L, L), 1)
    causal = colm <= row
    tril = jnp.where(causal, 1.0, 0.0).astype(bf16)
    d_hi, d_mid, d_lo = _split3(da)
    acs = _dot(tril, d_hi) + _dot(tril, d_mid) + _dot(tril, d_lo)
    acs_t = acs.T
    last = acs[L - 1:L, :]
    dt_b = dt.astype(bf16)
    eacs_b = jnp.exp(acs).astype(bf16)
    dec_b = jnp.exp(last - acs).astype(bf16)
    c_hi, c_lo = _split2(jnp.exp(jnp.broadcast_to(last, (8, 128))))
    lane_hp = lax.broadcasted_iota(jnp.int32, (L, HP), 1)
    gn = SSD_G * SSD_N

    for g in range(SSD_G):
        cols = slice(g * GW, (g + 1) * GW)
        ncols = slice(g * SSD_N, (g + 1) * SSD_N)
        xs = conv_silu(xs_ref, cols, g * GW, wx_ref, bx_ref)
        bm_g = conv_silu(bm_ref, ncols, D + g * SSD_N, wb_ref, bb_ref)
        cm_g = conv_silu(cm_ref, ncols, D + gn + g * SSD_N, wc_ref, bc_ref).astype(bf16)
        eh = eh_ref[:, cols]
        xdt = xs * _dot(dt_b, eh)
        xdt_b = xdt.astype(bf16)
        xdec_b = (xdt * _dot(dec_b, eh)).astype(bf16)
        eacs_x = _dot(eacs_b, eh)
        cdec_x = (_dot(c_hi, eh) + _dot(c_lo, eh))[0:1, :]

        cb = _dot_nt(cm_g, bm_g.astype(bf16))
        s_in = st_ref[g]
        y_off = _dot(cm_g, s_in.astype(bf16))
        st_ref[g] = s_in * cdec_x + _dot(bm_g.T.astype(bf16), xdec_b)
        ys = []
        for pp in range(GW // HP):
            h0 = g * (GW // SSD_P) + 2 * pp
            pc = slice(pp * HP, (pp + 1) * HP)
            ms = []
            for h in (h0, h0 + 1):
                diff = acs[:, h:h + 1] - acs_t[h:h + 1, :]
                ms.append(jnp.where(causal, cb * jnp.exp(jnp.where(causal, diff, NEG)), 0.0))
            lhs = jnp.concatenate(ms, axis=1).astype(bf16)
            xp = xdt_b[:, pc]
            zero = jnp.zeros_like(xp)
            rhs = jnp.concatenate([jnp.where(lane_hp < SSD_P, xp, zero),
                                   jnp.where(lane_hp >= SSD_P, xp, zero)], axis=0)
            ys.append(_dot(lhs, rhs) + eacs_x[:, pc] * y_off[:, pc]
                      + dskip_ref[:, g * GW + pp * HP:g * GW + (pp + 1) * HP] * xs[:, pc])
        z = z_ref[:, cols].astype(f32)
        yg = jnp.concatenate(ys, axis=1) * (z * _sigmoid(z))
        ms = jnp.mean(yg * yg, axis=-1, keepdims=True)
        o_ref[:, cols] = (yg * lax.rsqrt(ms + EPS) * ng_ref[:, cols]).astype(bf16)


def _ssd(proj, small, wts, B, T):
    nch = T // SSD_L
    L = SSD_L

    def main(name, w):
        base = _blk(name, w)
        return pl.BlockSpec((L, w), lambda b, c: (b * nch + c, base))

    def full(a):
        return pl.BlockSpec(a.shape, lambda b, c: (0,) * a.ndim)

    gn = SSD_G * SSD_N
    return pl.pallas_call(
        _ssd_kernel,
        grid=(B, nch),
        in_specs=[main('s_z', D), main('xs', D), main('bm', gn), main('cm', gn),
                  pl.BlockSpec((L, 128), lambda b, c: (b * nch + c, G))] + [full(a) for a in wts],
        out_specs=pl.BlockSpec((L, D), lambda b, c: (b * nch + c, 0)),
        out_shape=jax.ShapeDtypeStruct((B * T, D), bf16),
        scratch_shapes=[pltpu.VMEM((SSD_G, SSD_N, D // SSD_G), f32),
                        pltpu.VMEM((L, D + 2 * gn), bf16)],
        compiler_params=_cparams(("parallel", "arbitrary")),
        name="ssd",
    )(proj, proj, proj, proj, small, *wts)


def _merge_kernel(ya_ref, cb_ref, cc_ref, cu_ref, cch_ref, cuh_ref, yc_ref,
                  g0_ref, g1_ref, g2_ref, x_ref, cw_ref, wo_ref, o_ref, ext_ref, *, tm, tiles_per_seq):
    first = (pl.program_id(0) % tiles_per_seq) == 0
    up = lambda ref: ref[...].astype(f32)
    ext_ref[0:HALO, :] = jnp.where(first, 0.0, up(cch_ref) * up(cuh_ref))
    ext_ref[HALO:HALO + tm, :] = up(cc_ref) * up(cu_ref)
    lo = HALO - (CONV_W - 1)
    conv = cw_ref[0:1, :] * ext_ref[lo:lo + tm, :]
    for j in range(1, CONV_W):
        conv = conv + cw_ref[j:j + 1, :] * ext_ref[lo + j:lo + j + tm, :]
    yb = up(cb_ref) * conv
    merged = (_sigmoid(up(g0_ref)) * up(ya_ref) + _sigmoid(up(g1_ref)) * yb
              + _sigmoid(up(g2_ref)) * up(yc_ref))
    o_ref[...] = x_ref[...] + _dot(merged.astype(bf16), wo_ref[...])


def _merge(ya, yc, proj, x2, cw, wo, layer, T, tm):
    M = x2.shape[0]

    def col(name):
        base = _blk(name, D)
        return pl.BlockSpec((tm, D), lambda i: (i, base))

    def halo(name):
        base = _blk(name, D)
        return pl.BlockSpec((HALO, D), lambda i: (jnp.maximum(i * (tm // HALO) - 1, 0), base))

    row = pl.BlockSpec((tm, D), lambda i: (i, 0))
    return pl.pallas_call(
        functools.partial(_merge_kernel, tm=tm, tiles_per_seq=T // tm),
        grid=(M // tm,),
        in_specs=[row, col('cb'), col('cc'), col('cu'), halo('cc'), halo('cu'), row,
                  col('gm0'), col('gm1'), col('gm2'), row,
                  pl.BlockSpec((CONV_W, D), lambda i: (0, 0)),
                  pl.BlockSpec((None, D, D), lambda i: (layer, 0, 0))],
        out_specs=row,
        out_shape=jax.ShapeDtypeStruct((M, D), f32),
        scratch_shapes=[pltpu.VMEM((HALO + tm, D), f32)],
        compiler_params=_cparams(("parallel",)),
        name="merge_oproj",
    )(ya, proj, proj, proj, proj, proj, yc, proj, proj, proj, x2, cw, wo)


def _ffn_kernel(x_ref, g_ref, wu_ref, wd_ref, o_ref, h_ref):
    @pl.when(pl.program_id(1) == 0)
    def _():
        x = x_ref[...]
        h_ref[...] = _rms(x, g_ref[...]).astype(bf16)
        o_ref[...] = x

    u = jnp.maximum(_dot(h_ref[...], wu_ref[...]), 0.0)
    o_ref[...] += _dot((u * u).astype(bf16), wd_ref[...])


def _ffn(x2, g, wu, wd, layer, tm, tf):
    M = x2.shape[0]
    return pl.pallas_call(
        _ffn_kernel,
        grid=(M // tm, D_FF // tf),
        in_specs=[pl.BlockSpec((tm, D), lambda i, j: (i, 0)),
                  pl.BlockSpec((1, D), lambda i, j: (0, 0)),
                  pl.BlockSpec((None, D, tf), lambda i, j: (layer, 0, j)),
                  pl.BlockSpec((None, tf, D), lambda i, j: (layer, j, 0))],
        out_specs=pl.BlockSpec((tm, D), lambda i, j: (i, 0)),
        out_shape=jax.ShapeDtypeStruct((M, D), f32),
        scratch_shapes=[pltpu.VMEM((tm, D), bf16)],
        compiler_params=_cparams(("parallel", "arbitrary")),
        name="ffn",
    )(x2, g, wu, wd)


def _ple_kernel(x_ref, p_ref, g_ref, wp_ref, wg_ref, gf_ref, o_ref, *, final):
    x = x_ref[...]
    gate = _sigmoid(_dot(_rms(x, g_ref[...]).astype(bf16), wg_ref[...]))
    y = x + _dot(p_ref[...].astype(bf16), wp_ref[...]) * gate
    if final:
        y = _rms(y, gf_ref[...])
    o_ref[...] = y


def _ple(x2, p2, layer, g, wp, wg, gf, tm, final):
    M = x2.shape[0]
    base = layer * (M // tm)
    row = pl.BlockSpec((tm, D), lambda i: (i, 0))
    vec = pl.BlockSpec((1, D), lambda i: (0, 0))
    return pl.pallas_call(
        functools.partial(_ple_kernel, final=final),
        grid=(M // tm,),
        in_specs=[row, pl.BlockSpec((tm, PLE), lambda i: (base + i, 0)), vec,
                  pl.BlockSpec((None, PLE, D), lambda i: (layer, 0, 0)),
                  pl.BlockSpec((None, D, D), lambda i: (layer, 0, 0)), vec],
        out_specs=row,
        out_shape=jax.ShapeDtypeStruct((M, D), f32),
        compiler_params=_cparams(("parallel",)),
        name="ple",
    )(x2, p2, g, wp, wg, gf)


def _source_runs():
    runs = []
    for n in _ORDER:
        a, wd = _SRC[n]
        if runs and runs[-1][1] == a:
            runs[-1][1] = a + wd
        else:
            runs.append([a, a + wd])
    return runs


def _prep_w_in(w):
    depth = w.shape[0]
    w = w.astype(bf16)
    main = jnp.concatenate([w[:, :, a:b] for a, b in _source_runs()], axis=2)
    a, wd = _SRC['g_nsa']
    gn = w[:, :, a:a + wd].reshape(depth, D, G, R * 3)
    gn = jnp.pad(gn, ((0, 0), (0, 0), (0, 0), (0, 128 - R * 3))).reshape(depth, D, G * 128)
    a, wd = _SRC['s_dt']
    dt = jnp.pad(w[:, :, a:a + wd], ((0, 0), (0, 0), (0, 128 - wd)))
    return main, jnp.concatenate([gn, dt], axis=2)


def _overlap_matrix(nc):
    i = np.arange(nc)[:, None]
    j = np.arange(NSB)[None, :]
    ovl = (i * CMP_STRIDE < j * SLC + SLC) & (i * CMP_STRIDE + CMP_BLOCK > j * SLC)
    return jnp.asarray(ovl.T.astype(np.float32), dtype=bf16)


def _head_expand():
    h = np.arange(128)[:, None]
    ch = np.arange(D)[None, :]
    return jnp.asarray((ch // SSD_P == h).astype(np.float32), dtype=bf16)


def _conv_shift():
    i = np.arange((SSD_CONV - 1) * SSD_L)[:, None]
    r = np.arange(2 * SSD_L)[None, :]
    pick = r == SSD_L + i % SSD_L - (SSD_CONV - 1) + i // SSD_L
    return jnp.asarray(pick.astype(np.float32), dtype=bf16)


def kernel(x, p, positions, g_mix, w_in, nsa_pe_k, nsa_pe_v, phi_k_w1, phi_k_b1, phi_k_w2, phi_k_b2,
           phi_v_w1, phi_v_b1, phi_v_w2, phi_v_b2, sconv_w, ssd_conv_w, ssd_conv_b, ssd_dt_bias,
           ssd_a_log, ssd_d, ssd_norm_g, w_o, g_mlp, w_up, w_down, g_ple, w_ple, w_ple_gate, g_final):
    B, T, _ = x.shape
    depth = w_in.shape[0]
    M = B * T
    assert T % 256 == 0 and T // SLC <= NSB
    tq = 512
    tm_proj = 1024 if M % 1024 == 0 else 256
    tm = 512 if T % 512 == 0 else 256
    nc = T // CMP_STRIDE

    inv_freq = 1.0 / (10000.0 ** (jnp.arange(0, DK, 2, dtype=f32) / DK))
    ang = positions.astype(f32)[..., None] * inv_freq
    cosf = jnp.concatenate([jnp.cos(ang), jnp.cos(ang)], axis=-1).reshape(M, DK)
    sinf = jnp.concatenate([-jnp.sin(ang), jnp.sin(ang)], axis=-1).reshape(M, DK)
    ovl = _overlap_matrix(nc)
    eh = _head_expand()
    shift = _conv_shift()

    def vec(a, n=None):
        a = a.reshape(1, -1).astype(f32)
        return a if n is None else jnp.pad(a, ((0, 0), (0, n - a.shape[1])))

    x2 = x.reshape(M, D)
    p2 = p.reshape(depth * M, PLE)
    w_main, w_small = _prep_w_in(w_in)
    w_o16, w_up16, w_down16 = w_o.astype(bf16), w_up.astype(bf16), w_down.astype(bf16)
    w_ple16, w_gate16 = w_ple.astype(bf16), w_ple_gate.astype(bf16)
    for i in range(depth):
        proj, small = _proj(x2, vec(g_mix[i]), w_main, w_small, i, tm_proj, 2048)

        kc, vc, ksa, vs, kw, vw = _prep(proj, cosf, sinf, B, T, tq)

        def phi(pe, w1, b1, w2, b2):
            pe8 = jnp.broadcast_to(pe.reshape(1, -1), (8, CMP_BLOCK * DK)).astype(bf16)
            return (pe8, w1.astype(bf16), vec(b1), w2.astype(bf16), vec(b2))

        kcmp, vcmp = _compress(kc, vc,
                               phi(nsa_pe_k[i], phi_k_w1[i], phi_k_b1[i], phi_k_w2[i], phi_k_b2[i]),
                               phi(nsa_pe_v[i], phi_v_w1[i], phi_v_b1[i], phi_v_w2[i], phi_v_b2[i]))
        ya = _nsa(proj, small, cosf, sinf, ovl, kcmp, vcmp, ksa, vs, kw, vw, B, T, tq)

        cw, cbias = ssd_conv_w[i], ssd_conv_b[i]
        gn = SSD_G * SSD_N
        ssd_w = (cw[:, :D], vec(cbias[:D]), cw[:, D:D + gn], vec(cbias[D:D + gn]),
                 cw[:, D + gn:], vec(cbias[D + gn:]),
                 vec(ssd_dt_bias[i], 128), vec(ssd_a_log[i], 128),
                 vec(jnp.repeat(ssd_d[i], SSD_P)), vec(ssd_norm_g[i]), eh, shift)
        yc = _ssd(proj, small, ssd_w, B, T)

        x2 = _merge(ya, yc, proj, x2, sconv_w[i], w_o16, i, T, 256)
        x2 = _ffn(x2, vec(g_mlp[i]), w_up16, w_down16, i, tm_proj, 512)
        x2 = _ple(x2, p2, i, vec(g_ple[i]), w_ple16, w_gate16, vec(g_final), tm, final=(i == depth - 1))
    return x2.reshape(B, T, D)
```
